```python
import math
import jax, jax.numpy as jnp
from jax import lax
import numpy as np

D_MODEL = 4096
BATCH = 4
SEQ = 2048
DEPTH = 4
DEC_BATCH = 8
DEC_SEQ = 8
PAST_LEN = 8192
PAGE_SIZE = 128

N_MIXERS = 3
N_HEADS = 32
HEAD_DIM = D_MODEL // N_HEADS
N_KV_HEADS = 8
GROUP = N_HEADS // N_KV_HEADS
D_ATTN = N_HEADS * HEAD_DIM
D_KV = N_KV_HEADS * HEAD_DIM
N_IDX_HEADS = 32
D_IDX = 128
TOPK_MAX = 256
CONV_WIDTH = 3
D_FF = 256 * ((8 * D_MODEL // 3 + 255) // 256)
N_BUCKETS = 32
MAX_DISTANCE = 128
QBLOCK = 128
RMS_EPS = 1e-6
NEG_INF = -1e30
ATTN_SCALE = HEAD_DIM ** -0.5
INDEX_SCALE = (D_IDX * N_IDX_HEADS) ** -0.5
N_FOX = (DEPTH + 2) // 3
N_SCONV = (DEPTH + 1) // 3
N_DSA = DEPTH // 3

kernel_name = 'fox_sconv_dsa_hybrid_step'


def rmsnorm(x, g):
    xf = x.astype(jnp.float32)
    y = xf * lax.rsqrt(jnp.mean(xf * xf, axis=-1, keepdims=True) + RMS_EPS)
    return (y * g.astype(jnp.float32)).astype(x.dtype)


def gather_pages(pool, page_table):
    rows = pool[page_table]
    return rows.reshape(rows.shape[0], rows.shape[1] * rows.shape[2], *rows.shape[3:])


def causal_dwconv(u, prefix, w):
    T = u.shape[1]
    xp = jnp.concatenate([prefix.astype(u.dtype), u], axis=1)
    y = w[0] * xp[:, 0:T]
    for j in range(1, CONV_WIDTH):
        y = y + w[j] * xp[:, j:j + T]
    return y, xp[:, T:]


def sweep_query_blocks(fn, q_arrays, q_pos):
    T = q_pos.shape[0]
    nb = T // QBLOCK
    blocks = tuple(jnp.swapaxes(a.reshape(a.shape[0], nb, QBLOCK, *a.shape[2:]), 0, 1) for a in q_arrays)
    out = lax.map(lambda xs: fn(*xs), blocks + (q_pos.reshape(nb, QBLOCK),))
    out = jnp.swapaxes(out, 0, 1)
    return out.reshape(out.shape[0], T, *out.shape[3:])


def t5_bucket(dist):
    max_exact = N_BUCKETS // 2
    d = jnp.maximum(dist, 0)
    ratio = jnp.log(jnp.maximum(d, 1).astype(jnp.float32) / max_exact) / math.log(MAX_DISTANCE / max_exact)
    large = jnp.minimum(max_exact + (ratio * (N_BUCKETS - max_exact)).astype(jnp.int32), N_BUCKETS - 1)
    return jnp.where(d < max_exact, d, large)


def fox_attend(q, k, v, dq, dk, q_pos, k_pos):
    Bn, Tq = q.shape[0], q.shape[1]
    Tk = k.shape[1]
    qg = q.reshape(Bn, Tq, N_KV_HEADS, GROUP, HEAD_DIM)
    logits = jnp.einsum('bqkgd,bskd->bkgqs', qg, k).astype(jnp.float32) * ATTN_SCALE
    dq_h = jnp.transpose(dq.reshape(Bn, Tq, N_KV_HEADS, GROUP), (0, 2, 3, 1)).astype(jnp.float32)
    dk_h = jnp.transpose(dk.reshape(Bn, Tk, N_KV_HEADS, GROUP), (0, 2, 3, 1)).astype(jnp.float32)
    logits = logits + (dq_h[..., :, None] - dk_h[..., None, :])
    causal = k_pos[None, :] <= q_pos[:, None]
    p = jax.nn.softmax(jnp.where(causal, logits, NEG_INF), axis=-1)
    out = jnp.einsum('bkgqs,bskd->bqkgd', p.astype(v.dtype), v)
    return out.reshape(Bn, Tq, N_HEADS, HEAD_DIM)


def fox_project(u, w_q, w_k, w_v, w_f, b_f):
    Bn, T = u.shape[0], u.shape[1]
    q = (u @ w_q).reshape(Bn, T, N_HEADS, HEAD_DIM)
    k = (u @ w_k).reshape(Bn, T, N_KV_HEADS, HEAD_DIM)
    v = (u @ w_v).reshape(Bn, T, N_KV_HEADS, HEAD_DIM)
    logf = jax.nn.log_sigmoid((u @ w_f + b_f).astype(jnp.float32))
    return q, k, v, logf


def fox_mixer(up, us, pool_k, pool_v, pool_lf, page_table, w_q, w_k, w_v, w_f, b_f, w_o):
    q, k, v, lf = fox_project(up, w_q, w_k, w_v, w_f, b_f)
    cum = jnp.cumsum(lf, axis=1)
    pos = jnp.arange(up.shape[1])
    o = sweep_query_blocks(lambda qb, cb, pb: fox_attend(qb, k, v, cb, cum, pb, pos), (q, cum), pos)
    yp = o.reshape(up.shape[0], up.shape[1], D_ATTN) @ w_o
    qs, ks, vs, lfs = fox_project(us, w_q, w_k, w_v, w_f, b_f)
    k_past = gather_pages(pool_k, page_table)
    past = k_past.shape[1]
    k_all = jnp.concatenate([k_past, ks], axis=1)
    v_all = jnp.concatenate([gather_pages(pool_v, page_table), vs], axis=1)
    lf_all = jnp.concatenate([gather_pages(pool_lf, page_table).astype(jnp.float32), lfs], axis=1)
    cum_all = jnp.cumsum(lf_all, axis=1)
    k_pos = jnp.arange(past + us.shape[1])
    q_pos = past + jnp.arange(us.shape[1])
    o_s = fox_attend(qs, k_all, v_all, cum_all[:, past:], cum_all, q_pos, k_pos)
    ys = o_s.reshape(us.shape[0], us.shape[1], D_ATTN) @ w_o
    return yp, ys, (k, v, lf, ks, vs, lfs)


def sconv_branch(u, prefix, w_in, w_conv, w_out):
    b, c, h = jnp.split(u @ w_in, 3, axis=-1)
    y, state = causal_dwconv(c * h, prefix, w_conv)
    return (b * y) @ w_out, state


def dsa_attend(q, qi, wi, q_pos, k, v, ki, k_pos, rel_bias, n_top):
    Bn, Tq = q.shape[0], q.shape[1]
    dots = jnp.einsum('bqhe,bse->bqhs', qi, ki).astype(jnp.float32)
    score = jnp.einsum('bqh,bqhs->bqs', wi.astype(jnp.float32), jax.nn.relu(dots)) * INDEX_SCALE
    admissible = k_pos[None, :] <= q_pos[:, None]
    score = jnp.where(admissible[None], score, NEG_INF)
    _, sel = lax.top_k(score, n_top)
    bidx = jnp.arange(Bn)[:, None, None]
    k_sel = k[bidx, sel]
    v_sel = v[bidx, sel]
    sel_pos = k_pos[sel]
    valid = sel_pos <= q_pos[None, :, None]
    qg = q.reshape(Bn, Tq, N_KV_HEADS, GROUP, HEAD_DIM)
    logits = jnp.einsum('bqkgd,bqnkd->bqkgn', qg, k_sel).astype(jnp.float32) * ATTN_SCALE
    bias = rel_bias[t5_bucket(q_pos[None, :, None] - sel_pos)].astype(jnp.float32)
    bias = jnp.transpose(bias.reshape(Bn, Tq, n_top, N_KV_HEADS, GROUP), (0, 1, 3, 4, 2))
    logits = jnp.where(valid[:, :, None, None, :], logits + bias, NEG_INF)
    p = jax.nn.softmax(logits, axis=-1)
    out = jnp.einsum('bqkgn,bqnkd->bqkgd', p.astype(v.dtype), v_sel)
    return out.reshape(Bn, Tq, N_HEADS, HEAD_DIM)


def dsa_project(u, w_q, w_k, w_v, w_qi, w_ki, w_wi):
    Bn, T = u.shape[0], u.shape[1]
    q = (u @ w_q).reshape(Bn, T, N_HEADS, HEAD_DIM)
    k = (u @ w_k).reshape(Bn, T, N_KV_HEADS, HEAD_DIM)
    v = (u @ w_v).reshape(Bn, T, N_KV_HEADS, HEAD_DIM)
    qi = (u @ w_qi).reshape(Bn, T, N_IDX_HEADS, D_IDX)
    ki = u @ w_ki
    wi = u @ w_wi
    return q, k, v, qi, ki, wi


def dsa_mixer(up, us, pool_k, pool_v, pool_ki, page_table, rel_bias, w_q, w_k, w_v, w_o, w_qi, w_ki, w_wi):
    q, k, v, qi, ki, wi = dsa_project(up, w_q, w_k, w_v, w_qi, w_ki, w_wi)
    pos = jnp.arange(up.shape[1])
    n_top_p = min(TOPK_MAX, up.shape[1] // 4)
    o = sweep_query_blocks(lambda qb, qib, wib, pb: dsa_attend(qb, qib, wib, pb, k, v, ki, pos, rel_bias, n_top_p), (q, qi, wi), pos)
    yp = o.reshape(up.shape[0], up.shape[1], D_ATTN) @ w_o
    qs, ks, vs, qis, kis, wis = dsa_project(us, w_q, w_k, w_v, w_qi, w_ki, w_wi)
    k_past = gather_pages(pool_k, page_table)
    past = k_past.shape[1]
    k_all = jnp.concatenate([k_past, ks], axis=1)
    v_all = jnp.concatenate([gather_pages(pool_v, page_table), vs], axis=1)
    ki_all = jnp.concatenate([gather_pages(pool_ki, page_table), kis], axis=1)
    k_pos = jnp.arange(past + us.shape[1])
    q_pos = past + jnp.arange(us.shape[1])
    n_top_s = min(TOPK_MAX, k_all.shape[1] // 4)
    o_s = dsa_attend(qs, qis, wis, q_pos, k_all, v_all, ki_all, k_pos, rel_bias, n_top_s)
    ys = o_s.reshape(us.shape[0], us.shape[1], D_ATTN) @ w_o
    return yp, ys, (k, v, ki, ks, vs, kis)


def conv_ffn(u, prefix, w_up, w_conv, w_down):
    g, val = jnp.split(u @ w_up, 2, axis=-1)
    gc, state = causal_dwconv(g, prefix, w_conv)
    return (jax.nn.silu(gc) * val) @ w_down, state


def setup_inputs(seed: int = 0) -> dict:
    key = jax.random.key(seed)
    keys = jax.random.split(key, 48)
    cnt = [0]

    def nxt():
        k = keys[cnt[0]]
        cnt[0] += 1
        return k

    def nrm(shape, scale=1.0):
        return jax.random.normal(nxt(), shape, jnp.float32) * scale

    n_pages = PAST_LEN // PAGE_SIZE
    n_pool = (5 * DEC_BATCH * n_pages + 3) // 4
    sd = D_MODEL ** -0.5
    inputs = {}
    inputs['x_prompt'] = nrm((BATCH, SEQ, D_MODEL))
    inputs['x_sample'] = nrm((DEC_BATCH, DEC_SEQ, D_MODEL))
    inputs['cache_fox_k'] = nrm((N_FOX, n_pool, PAGE_SIZE, N_KV_HEADS, HEAD_DIM))
    inputs['cache_fox_v'] = nrm((N_FOX, n_pool, PAGE_SIZE, N_KV_HEADS, HEAD_DIM))
    inputs['cache_fox_logf'] = jax.nn.log_sigmoid(jax.random.uniform(nxt(), (N_FOX, n_pool, PAGE_SIZE, N_HEADS), jnp.float32, 1.0, 6.0))
    inputs['state_sconv'] = nrm((N_SCONV, DEC_BATCH, CONV_WIDTH - 1, D_MODEL))
    inputs['cache_dsa_k'] = nrm((N_DSA, n_pool, PAGE_SIZE, N_KV_HEADS, HEAD_DIM))
    inputs['cache_dsa_v'] = nrm((N_DSA, n_pool, PAGE_SIZE, N_KV_HEADS, HEAD_DIM))
    inputs['cache_dsa_kidx'] = nrm((N_DSA, n_pool, PAGE_SIZE, D_IDX))
    inputs['state_ffn_conv'] = nrm((DEPTH, DEC_BATCH, CONV_WIDTH - 1, D_FF))
    inputs['page_table'] = jax.random.permutation(nxt(), n_pool)[:DEC_BATCH * n_pages].reshape(DEC_BATCH, n_pages).astype(jnp.int32)
    inputs['rel_bias'] = nrm((N_BUCKETS, N_HEADS), 0.5)
    inputs['norm_mix'] = 1.0 + nrm((DEPTH, D_MODEL), 0.02)
    inputs['norm_ffn'] = 1.0 + nrm((DEPTH, D_MODEL), 0.02)
    inputs['norm_final'] = 1.0 + nrm((D_MODEL,), 0.02)
    inputs['fox_w_q'] = nrm((N_FOX, D_MODEL, D_ATTN), sd)
    inputs['fox_w_k'] = nrm((N_FOX, D_MODEL, D_KV), sd)
    inputs['fox_w_v'] = nrm((N_FOX, D_MODEL, D_KV), sd)
    inputs['fox_w_f'] = nrm((N_FOX, D_MODEL, N_HEADS), 0.1 * sd)
    inputs['fox_b_f'] = jax.random.uniform(nxt(), (N_FOX, N_HEADS), jnp.float32, 1.0, 6.0)
    inputs['fox_w_o'] = nrm((N_FOX, D_ATTN, D_MODEL), D_ATTN ** -0.5)
    inputs['sc_w_in'] = nrm((N_SCONV, D_MODEL, 3 * D_MODEL), sd)
    inputs['sc_w_conv'] = nrm((N_SCONV, CONV_WIDTH, D_MODEL), CONV_WIDTH ** -0.5)
    inputs['sc_w_out'] = nrm((N_SCONV, D_MODEL, D_MODEL), sd)
    inputs['dsa_w_q'] = nrm((N_DSA, D_MODEL, D_ATTN), sd)
    inputs['dsa_w_k'] = nrm((N_DSA, D_MODEL, D_KV), sd)
    inputs['dsa_w_v'] = nrm((N_DSA, D_MODEL, D_KV), sd)
    inputs['dsa_w_o'] = nrm((N_DSA, D_ATTN, D_MODEL), D_ATTN ** -0.5)
    inputs['dsa_w_qi'] = nrm((N_DSA, D_MODEL, N_IDX_HEADS * D_IDX), sd)
    inputs['dsa_w_ki'] = nrm((N_DSA, D_MODEL, D_IDX), sd)
    inputs['dsa_w_wi'] = nrm((N_DSA, D_MODEL, N_IDX_HEADS), sd)
    inputs['ffn_w_up'] = nrm((DEPTH, D_MODEL, 2 * D_FF), sd)
    inputs['ffn_w_conv'] = nrm((DEPTH, CONV_WIDTH, D_FF), CONV_WIDTH ** -0.5)
    inputs['ffn_w_down'] = nrm((DEPTH, D_FF, D_MODEL), D_FF ** -0.5)
    return inputs


def reference(x_prompt, x_sample, cache_fox_k, cache_fox_v, cache_fox_logf, state_sconv,
              cache_dsa_k, cache_dsa_v, cache_dsa_kidx, state_ffn_conv, page_table, rel_bias,
              norm_mix, norm_ffn, norm_final,
              fox_w_q, fox_w_k, fox_w_v, fox_w_f, fox_b_f, fox_w_o,
              sc_w_in, sc_w_conv, sc_w_out,
              dsa_w_q, dsa_w_k, dsa_w_v, dsa_w_o, dsa_w_qi, dsa_w_ki, dsa_w_wi,
              ffn_w_up, ffn_w_conv, ffn_w_down):
    hp, hs = x_prompt, x_sample
    fox_new = ([], [], [], [], [], [])
    sc_new = ([], [])
    dsa_new = ([], [], [], [], [], [])
    ffn_new = ([], [])
    for i in range(DEPTH):
        j = i // N_MIXERS
        kind = i % N_MIXERS
        up = rmsnorm(hp, norm_mix[i])
        us = rmsnorm(hs, norm_mix[i])
        if kind == 0:
            mp, ms, rows = fox_mixer(up, us, cache_fox_k[j], cache_fox_v[j], cache_fox_logf[j], page_table,
                                     fox_w_q[j], fox_w_k[j], fox_w_v[j], fox_w_f[j], fox_b_f[j], fox_w_o[j])
            for lst, a in zip(fox_new, rows):
                lst.append(a)
        elif kind == 1:
            zero_prefix = jnp.zeros((up.shape[0], CONV_WIDTH - 1, D_MODEL), up.dtype)
            mp, sp = sconv_branch(up, zero_prefix, sc_w_in[j], sc_w_conv[j], sc_w_out[j])
            ms, ss = sconv_branch(us, state_sconv[j], sc_w_in[j], sc_w_conv[j], sc_w_out[j])
            sc_new[0].append(sp)
            sc_new[1].append(ss)
        else:
            mp, ms, rows = dsa_mixer(up, us, cache_dsa_k[j], cache_dsa_v[j], cache_dsa_kidx[j], page_table, rel_bias,
                                     dsa_w_q[j], dsa_w_k[j], dsa_w_v[j], dsa_w_o[j], dsa_w_qi[j], dsa_w_ki[j], dsa_w_wi[j])
            for lst, a in zip(dsa_new, rows):
                lst.append(a)
        hp = hp + mp
        hs = hs + ms
        vp = rmsnorm(hp, norm_ffn[i])
        vs = rmsnorm(hs, norm_ffn[i])
        zero_ffn = jnp.zeros((vp.shape[0], CONV_WIDTH - 1, D_FF), vp.dtype)
        fp, cp = conv_ffn(vp, zero_ffn, ffn_w_up[i], ffn_w_conv[i], ffn_w_down[i])
        fs, cs = conv_ffn(vs, state_ffn_conv[i], ffn_w_up[i], ffn_w_conv[i], ffn_w_down[i])
        ffn_new[0].append(cp)
        ffn_new[1].append(cs)
        hp = hp + fp
        hs = hs + fs
    y_prompt = rmsnorm(hp, norm_final)
    y_sample = rmsnorm(hs, norm_final)
    return (y_prompt, y_sample,
            jnp.stack(fox_new[0]), jnp.stack(fox_new[1]), jnp.stack(fox_new[2]),
            jnp.stack(fox_new[3]), jnp.stack(fox_new[4]), jnp.stack(fox_new[5]),
            jnp.stack(sc_new[0]), jnp.stack(sc_new[1]),
            jnp.stack(dsa_new[0]), jnp.stack(dsa_new[1]), jnp.stack(dsa_new[2]),
            jnp.stack(dsa_new[3]), jnp.stack(dsa_new[4]), jnp.stack(dsa_new[5]),
            jnp.stack(ffn_new[0]), jnp.stack(ffn_new[1]))
```

```python
import functools
import math

import numpy as np
import jax
import jax.numpy as jnp
from jax import lax
from jax.experimental import pallas as pl
from jax.experimental.pallas import tpu as pltpu

F32 = jnp.float32
BF16 = jnp.bfloat16
I32 = jnp.int32

RMS_EPS = 1e-6
NEG_INF = -1e30
TOPK_MAX = 256
N_BUCKETS = 32
MAX_DISTANCE = 128
CONV_WIDTH = 3

LANE = 128
SUBLANE = 8
VMEM_LIMIT_BYTES = 56 * 1024 * 1024
PAGES_PER_STEP = 8

_NT = (((1,), (1,)), ((), ()))


def _cparams(*sem):
    return pltpu.CompilerParams(dimension_semantics=sem, vmem_limit_bytes=VMEM_LIMIT_BYTES)


def _log_sigmoid(x):
    return jnp.minimum(x, 0.0) - jnp.log1p(jnp.exp(-jnp.abs(x)))


def _split3(x):
    hi = x.astype(BF16)
    r1 = x - hi.astype(F32)
    mid = r1.astype(BF16)
    lo = (r1 - mid.astype(F32)).astype(BF16)
    return hi, mid, lo


def _tri_cumsum(x):
    n = x.shape[-1]
    r = lax.broadcasted_iota(I32, (n, n), 0)
    c = lax.broadcasted_iota(I32, (n, n), 1)
    tri = jnp.where(r <= c, 1.0, 0.0).astype(BF16)
    hi, mid, lo = _split3(x)
    out = jnp.dot(hi, tri, preferred_element_type=F32)
    out = out + jnp.dot(mid, tri, preferred_element_type=F32)
    out = out + jnp.dot(lo, tri, preferred_element_type=F32)
    return out


def _rms_kernel(x_ref, g_ref, o_ref):
    x = x_ref[...]
    ms = jnp.mean(x * x, axis=-1, keepdims=True)
    o_ref[...] = ((x * lax.rsqrt(ms + RMS_EPS)) * g_ref[...]).astype(o_ref.dtype)


def _rmsnorm(h, g, out_dtype):
    M, D = h.shape
    tm = min(512, M)
    return pl.pallas_call(
        _rms_kernel,
        grid=(M // tm,),
        in_specs=[pl.BlockSpec((tm, D), lambda i: (i, 0)), pl.BlockSpec((1, D), lambda i: (0, 0))],
        out_specs=pl.BlockSpec((tm, D), lambda i: (i, 0)),
        out_shape=jax.ShapeDtypeStruct((M, D), out_dtype),
        compiler_params=_cparams("parallel"),
        name="rmsnorm",
    )(h, g.reshape(1, D).astype(F32))


def _mm_kernel(*refs, nk, has_res, has_ls, emit_f32, emit_bf16, head_major):
    it = iter(refs)
    x_ref = next(it)
    w_ref = next(it)
    b_ref = next(it) if has_ls else None
    r_ref = next(it) if has_res else None
    o32_ref = next(it) if emit_f32 else None
    o16_ref = next(it) if emit_bf16 else None
    acc_ref = next(it) if nk > 1 else None

    def finish(acc):
        if has_ls:
            acc = _log_sigmoid(acc + b_ref[...])
        if has_res:
            acc = r_ref[...] + acc
        if emit_f32:
            o32_ref[...] = acc
        if emit_bf16:
            if head_major:
                for hh in range(o16_ref.shape[0]):
                    o16_ref[hh] = acc[:, hh * LANE:(hh + 1) * LANE].astype(BF16)
            else:
                o16_ref[...] = acc.astype(BF16)

    part = jnp.dot(x_ref[...], w_ref[...], preferred_element_type=F32)
    if nk == 1:
        finish(part)
    else:
        k = pl.program_id(2)

        @pl.when(k == 0)
        def _():
            acc_ref[...] = part

        @pl.when(k > 0)
        def _():
            acc_ref[...] += part

        @pl.when(k == nk - 1)
        def _():
            finish(acc_ref[...])


def _mm(x, w, *, res=None, ls_bias=None, emit_f32=True, emit_bf16=False, head_major=False,
        tm=1024, tn=512, tk=None, name="mm"):
    M, K = x.shape
    N = w.shape[1]
    tm = min(tm, M)
    tn = min(tn, N)
    tk = K if tk is None else min(tk, K)
    assert M % tm == 0 and N % tn == 0 and K % tk == 0, (M, N, K, tm, tn, tk)
    nk = K // tk
    in_specs = [pl.BlockSpec((tm, tk), lambda i, j, k: (i, k)), pl.BlockSpec((tk, tn), lambda i, j, k: (k, j))]
    args = [x, w]
    if ls_bias is not None:
        in_specs.append(pl.BlockSpec((1, tn), lambda i, j, k: (0, j)))
        args.append(ls_bias.reshape(1, N).astype(F32))
    if res is not None:
        in_specs.append(pl.BlockSpec((tm, tn), lambda i, j, k: (i, j)))
        args.append(res)
    out_specs, out_shape = [], []
    if emit_f32:
        out_specs.append(pl.BlockSpec((tm, tn), lambda i, j, k: (i, j)))
        out_shape.append(jax.ShapeDtypeStruct((M, N), F32))
    if emit_bf16:
        if head_major:
            out_specs.append(pl.BlockSpec((tn // LANE, tm, LANE), lambda i, j, k: (j, i, 0)))
            out_shape.append(jax.ShapeDtypeStruct((N // LANE, M, LANE), BF16))
        else:
            out_specs.append(pl.BlockSpec((tm, tn), lambda i, j, k: (i, j)))
            out_shape.append(jax.ShapeDtypeStruct((M, N), BF16))
    scratch = [pltpu.VMEM((tm, tn), F32)] if nk > 1 else []
    outs = pl.pallas_call(
        functools.partial(_mm_kernel, nk=nk, has_res=res is not None, has_ls=ls_bias is not None,
                          emit_f32=emit_f32, emit_bf16=emit_bf16, head_major=head_major),
        grid=(M // tm, N // tn, nk),
        in_specs=in_specs,
        out_specs=out_specs,
        out_shape=out_shape,
        scratch_shapes=scratch,
        compiler_params=_cparams("parallel", "parallel", "arbitrary"),
        name=name,
    )(*args)
    return outs[0] if len(outs) == 1 else tuple(outs)


def _gconv_kernel(*refs, mode, nw, tm, sample, seq_blocks, seq_len):
    x_ref = refs[0]
    w_refs = refs[1:1 + nw]
    wc_ref = refs[1 + nw]
    pos = 2 + nw
    if sample:
        p1_ref, p2_ref = refs[pos], refs[pos + 1]
        pos += 2
    o_ref, st_ref, buf_ref = refs[pos], refs[pos + 1], refs[pos + 2]
    carry_ref = None if sample else refs[pos + 3]

    i = pl.program_id(0)
    j = pl.program_id(1)
    x = x_ref[...]
    ys = [jnp.dot(x, w[...], preferred_element_type=F32) for w in w_refs]
    if mode == "ffn":
        cin, other = ys[0], ys[1]
    else:
        other, cin = ys[0], ys[1] * ys[2]
    buf_ref[SUBLANE:SUBLANE + tm, :] = cin
    if sample:
        buf_ref[0:SUBLANE, :] = jnp.zeros((SUBLANE, cin.shape[1]), F32)
        rmod = lax.broadcasted_iota(I32, (tm, 1), 0) % seq_len
        x1 = jnp.where(rmod >= 1, buf_ref[SUBLANE - 1:SUBLANE - 1 + tm, :], p1_ref[...])
        x2 = jnp.where(rmod >= 2, buf_ref[SUBLANE - 2:SUBLANE - 2 + tm, :], p2_ref[...])
        st_ref[...] = cin
    else:
        first = (i % seq_blocks) == 0

        @pl.when(first)
        def _():
            buf_ref[0:SUBLANE, :] = jnp.zeros((SUBLANE, cin.shape[1]), F32)

        @pl.when(jnp.logical_not(first))
        def _():
            buf_ref[0:SUBLANE, :] = carry_ref[j]

        tail = cin[tm - SUBLANE:tm, :]
        carry_ref[j] = tail
        st_ref[...] = tail
        x1 = buf_ref[SUBLANE - 1:SUBLANE - 1 + tm, :]
        x2 = buf_ref[SUBLANE - 2:SUBLANE - 2 + tm, :]
    wc = wc_ref[...]
    y = wc[0:1, :] * x2 + wc[1:2, :] * x1 + wc[2:3, :] * cin
    if mode == "ffn":
        out = (y * (1.0 / (1.0 + jnp.exp(-y)))) * other
    else:
        out = other * y
    o_ref[...] = out.astype(BF16)


def _gconv(x, w, col_offsets, nc, wconv, *, mode, seq_len, prefix=None, tm=1024, tn=512, name="gconv"):
    M, K = x.shape
    sample = prefix is not None
    tm = min(tm, M) if sample else min(tm, seq_len)
    tn = min(tn, nc)
    assert M % tm == 0 and nc % tn == 0
    if sample:
        assert tm == M
    else:
        assert seq_len % tm == 0
    nj = nc // tn
    nw = len(col_offsets)
    in_specs = [pl.BlockSpec((tm, K), lambda i, j: (i, 0))]
    args = [x]
    for off in col_offsets:
        assert off % tn == 0
        in_specs.append(pl.BlockSpec((K, tn), functools.partial(lambda i, j, o: (0, o + j), o=off // tn)))
        args.append(w)
    in_specs.append(pl.BlockSpec((CONV_WIDTH, tn), lambda i, j: (0, j)))
    args.append(wconv)
    if sample:
        for p in prefix:
            in_specs.append(pl.BlockSpec((tm, tn), lambda i, j: (i, j)))
            args.append(p)
        st_spec = pl.BlockSpec((tm, tn), lambda i, j: (i, j))
        st_shape = jax.ShapeDtypeStruct((M, nc), F32)
    else:
        st_spec = pl.BlockSpec((None, SUBLANE, tn), lambda i, j: (i, 0, j))
        st_shape = jax.ShapeDtypeStruct((M // tm, SUBLANE, nc), F32)
    scratch = [pltpu.VMEM((SUBLANE + tm, tn), F32)]
    if not sample:
        scratch.append(pltpu.VMEM((nj, SUBLANE, tn), F32))
    out, st = pl.pallas_call(
        functools.partial(_gconv_kernel, mode=mode, nw=nw, tm=tm, sample=sample,
                          seq_blocks=max(seq_len // tm, 1), seq_len=seq_len),
        grid=(M // tm, nj),
        in_specs=in_specs,
        out_specs=[pl.BlockSpec((tm, tn), lambda i, j: (i, j)), st_spec],
        out_shape=[jax.ShapeDtypeStruct((M, nc), BF16), st_shape],
        scratch_shapes=scratch,
        compiler_params=_cparams("arbitrary", "arbitrary"),
        name=name,
    )(*args)
    return out, st


def _conv_prefix(state, seq_len):
    Bn, _, C = state.shape
    z = jnp.zeros((Bn, seq_len, C), F32)
    p1 = z.at[:, 0].set(state[:, 1])
    p2 = z.at[:, 0].set(state[:, 0]).at[:, 1].set(state[:, 1])
    return p1.reshape(Bn * seq_len, C), p2.reshape(Bn * seq_len, C)


def _cum_kernel(u_ref, wft_ref, bf_ref, o_ref, carry_ref):
    t = pl.program_id(1)

    @pl.when(t == 0)
    def _():
        carry_ref[...] = jnp.zeros_like(carry_ref)

    z = lax.dot_general(wft_ref[...], u_ref[...], _NT, preferred_element_type=F32) + bf_ref[...]
    c = _tri_cumsum(_log_sigmoid(z)) + carry_ref[...]
    o_ref[...] = c
    carry_ref[...] = c[:, c.shape[1] - 1:c.shape[1]]


def _fox_cum(u, wft, bf, Bn, T, tk):
    M, D = u.shape
    H = wft.shape[0]
    nt = T // tk
    return pl.pallas_call(
        _cum_kernel,
        grid=(Bn, nt),
        in_specs=[pl.BlockSpec((tk, D), lambda b, t: (b * nt + t, 0)),
                  pl.BlockSpec((H, D), lambda b, t: (0, 0)),
                  pl.BlockSpec((H, 1), lambda b, t: (0, 0))],
        out_specs=pl.BlockSpec((None, None, H, tk), lambda b, t: (b, t, 0, 0)),
        out_shape=jax.ShapeDtypeStruct((Bn, nt, H, tk), F32),
        scratch_shapes=[pltpu.VMEM((H, 1), F32)],
        compiler_params=_cparams("arbitrary", "arbitrary"),
        name="fox_cum",
    )(u, wft, bf.reshape(H, 1).astype(F32))


def _fox_attn_kernel(q_ref, k_ref, v_ref, cum_ref, o_ref, m_ref, l_ref, acc_ref, *, G, tq, tk, scale):
    qi = pl.program_id(2)
    hd = q_ref.shape[-1]
    q = q_ref[...].reshape(G * tq, hd)
    m_ref[...] = jnp.full(m_ref.shape, NEG_INF, F32)
    l_ref[...] = jnp.zeros(l_ref.shape, F32)
    acc_ref[...] = jnp.zeros(acc_ref.shape, F32)
    row = lax.broadcasted_iota(I32, (tq, tk), 0) + qi * tq
    col = lax.broadcasted_iota(I32, (tq, tk), 1)

    def body(j, carry):
        start = pl.multiple_of(j * tk, tk)
        kt = k_ref[pl.ds(start, tk), :]
        vt = v_ref[pl.ds(start, tk), :]
        s = lax.dot_general(q, kt, _NT, preferred_element_type=F32) * scale
        s = s.reshape(G, tq, tk) - cum_ref[j]
        mask = (col + j * tk) <= row
        s = jnp.where(mask[None], s, NEG_INF)
        m_prev = m_ref[...]
        m_new = jnp.maximum(m_prev, jnp.max(s, axis=-1, keepdims=True))
        alpha = jnp.exp(m_prev - m_new)
        p = jnp.exp(s - m_new)
        l_ref[...] = alpha * l_ref[...] + jnp.sum(p, axis=-1, keepdims=True)
        pv = jnp.dot(p.reshape(G * tq, tk).astype(BF16), vt, preferred_element_type=F32)
        acc_ref[...] = alpha * acc_ref[...] + pv.reshape(G, tq, hd)
        m_ref[...] = m_new
        return carry

    lax.fori_loop(0, qi + 1, body, 0)
    out = acc_ref[...] / l_ref[...]
    for g in range(G):
        o_ref[:, g * hd:(g + 1) * hd] = out[g].astype(o_ref.dtype)


def _fox_attn_prompt(q_hm, k16, v16, cum, Bn, T, G, scale, tq=256):
    H, M, hd = q_hm.shape
    KVH = H // G
    tq = min(tq, T)
    nq = T // tq
    return pl.pallas_call(
        functools.partial(_fox_attn_kernel, G=G, tq=tq, tk=tq, scale=scale),
        grid=(Bn, KVH, nq),
        in_specs=[pl.BlockSpec((G, tq, hd), lambda b, h, i: (h, b * nq + i, 0)),
                  pl.BlockSpec((T, hd), lambda b, h, i: (b, h)),
                  pl.BlockSpec((T, hd), lambda b, h, i: (b, h)),
                  pl.BlockSpec((None, None, nq, G, 1, tq), lambda b, h, i: (b, h, 0, 0, 0, 0))],
        out_specs=pl.BlockSpec((tq, G * hd), lambda b, h, i: (b * nq + i, h)),
        out_shape=jax.ShapeDtypeStruct((M, H * hd), BF16),
        scratch_shapes=[pltpu.VMEM((G, tq, 1), F32), pltpu.VMEM((G, tq, 1), F32), pltpu.VMEM((G, tq, hd), F32)],
        compiler_params=_cparams("parallel", "parallel", "arbitrary"),
        name="fox_attn_prompt",
    )(q_hm, k16, v16, cum)


def _paged_attn_kernel(*refs, kind, P, KVH, G, Q, page, hd, scale):
    pt_ref = refs[0]
    del pt_ref
    pos = 1
    q_ref = refs[pos]; pos += 1
    kp_refs = refs[pos:pos + P]; pos += P
    vp_refs = refs[pos:pos + P]; pos += P
    kn_ref, vn_ref = refs[pos], refs[pos + 1]; pos += 2
    if kind == "fox":
        lf_refs = refs[pos:pos + P]; pos += P
        lfn_ref = refs[pos]; pos += 1
    else:
        mp_ref, mn_ref, tl_ref, tn_ref = refs[pos:pos + 4]; pos += 4
    o_ref = refs[pos]; pos += 1
    kb_ref, vb_ref, m_ref, l_ref, acc_ref = refs[pos:pos + 5]; pos += 5
    if kind == "fox":
        cum_ref, carry_ref = refs[pos:pos + 2]

    jb = pl.program_id(1)
    nb = pl.num_programs(1)
    R = G * Q

    @pl.when(jb == 0)
    def _():
        m_ref[...] = jnp.full(m_ref.shape, NEG_INF, F32)
        l_ref[...] = jnp.zeros(l_ref.shape, F32)
        acc_ref[...] = jnp.zeros(acc_ref.shape, F32)
        if kind == "fox":
            carry_ref[...] = jnp.zeros(carry_ref.shape, F32)

    for p in range(P):
        kb_ref[p * page:(p + 1) * page, :] = kp_refs[p][...].astype(BF16)
        vb_ref[p * page:(p + 1) * page, :] = vp_refs[p][...].astype(BF16)
        if kind == "fox":
            c = _tri_cumsum(lf_refs[p][...]) + carry_ref[...]
            cum_ref[:, p * page:(p + 1) * page] = c
            carry_ref[...] = c[:, page - 1:page]

    def update(kvh, s, vt):
        m_prev = m_ref[kvh]
        m_new = jnp.maximum(m_prev, jnp.max(s, axis=-1, keepdims=True))
        alpha = jnp.exp(m_prev - m_new)
        pr = jnp.exp(s - m_new)
        l_ref[kvh] = alpha * l_ref[kvh] + jnp.sum(pr, axis=-1, keepdims=True)
        acc_ref[kvh] = alpha * acc_ref[kvh] + jnp.dot(pr.astype(BF16), vt, preferred_element_type=F32)
        m_ref[kvh] = m_new

    def head_rows(tile, kvh):
        return jnp.concatenate(
            [jnp.broadcast_to(tile[kvh * G + g:kvh * G + g + 1, :], (Q, tile.shape[1])) for g in range(G)], axis=0)

    def query_rows(tile):
        return jnp.concatenate([tile] * G, axis=0)

    n = P * page
    if kind == "fox":
        cum = cum_ref[...]
    else:
        madd = mp_ref[...]
        is_last = jnp.where(jb == nb - 1, 1.0, 0.0)
    for kvh in range(KVH):
        q = q_ref[kvh]
        s = lax.dot_general(q, kb_ref[:, kvh * hd:(kvh + 1) * hd], _NT, preferred_element_type=F32) * scale
        if kind == "fox":
            s = s - head_rows(cum, kvh)
        else:
            s = s + query_rows(madd)
            tail = s[:, n - page:] + is_last * tl_ref[kvh]
            s = tail if n == page else jnp.concatenate([s[:, :n - page], tail], axis=1)
        update(kvh, s, vb_ref[:, kvh * hd:(kvh + 1) * hd])

    @pl.when(jb == nb - 1)
    def _():
        kn = kn_ref[...].astype(BF16)
        vn = vn_ref[...].astype(BF16)
        qpos = lax.broadcasted_iota(I32, (R, page), 0) % Q
        kpos = lax.broadcasted_iota(I32, (R, page), 1)
        causal = kpos <= qpos
        if kind == "fox":
            cn = _tri_cumsum(lfn_ref[...]) + carry_ref[...]
        else:
            mnew = mn_ref[...]
        for kvh in range(KVH):
            q = q_ref[kvh]
            s = lax.dot_general(q, kn[:, kvh * hd:(kvh + 1) * hd], _NT, preferred_element_type=F32) * scale
            if kind == "fox":
                s = s - head_rows(cn, kvh)
            else:
                s = s + query_rows(mnew) + tn_ref[kvh]
            s = jnp.where(causal, s, NEG_INF)
            update(kvh, s, vn[:, kvh * hd:(kvh + 1) * hd])
        for kvh in range(KVH):
            o_ref[kvh] = acc_ref[kvh] / l_ref[kvh]


def _paged_attn(kind, q, pool_k, pool_v, k_new, v_new, page_table, extras, scale):
    Bn, KVH, R, hd = q.shape
    page = pool_k.shape[1]
    n_pages = page_table.shape[1]
    P = min(PAGES_PER_STEP, n_pages)
    assert n_pages % P == 0
    nb = n_pages // P
    H = None
    if kind == "fox":
        H = extras[0].shape[1]
        G = H // KVH
    else:
        G = extras[2].shape[1] // extras[0].shape[1]
    Q = R // G

    def page_map(p):
        return lambda b, j, pt: (pt[b, j * P + p], 0, 0)

    in_specs = [pl.BlockSpec((None, KVH, R, hd), lambda b, j, pt: (b, 0, 0, 0))]
    args = [q]
    for pool in (pool_k, pool_v):
        for p in range(P):
            in_specs.append(pl.BlockSpec((None, page, KVH * hd), page_map(p)))
            args.append(pool)
    for new in (k_new, v_new):
        in_specs.append(pl.BlockSpec((None, page, KVH * hd), lambda b, j, pt: (b, 0, 0)))
        args.append(new)
    scratch = [pltpu.VMEM((P * page, KVH * hd), BF16), pltpu.VMEM((P * page, KVH * hd), BF16),
               pltpu.VMEM((KVH, R, 1), F32), pltpu.VMEM((KVH, R, 1), F32), pltpu.VMEM((KVH, R, hd), F32)]
    if kind == "fox":
        pool_lft, lf_newt = extras
        for p in range(P):
            in_specs.append(pl.BlockSpec((None, H, page), page_map(p)))
            args.append(pool_lft)
        in_specs.append(pl.BlockSpec((None, H, page), lambda b, j, pt: (b, 0, 0)))
        args.append(lf_newt)
        scratch += [pltpu.VMEM((H, P * page), F32), pltpu.VMEM((H, 1), F32)]
    else:
        madd_past, madd_new, bias_last, bias_new = extras
        in_specs.append(pl.BlockSpec((None, Q, P * page), lambda b, j, pt: (b, 0, j)))
        args.append(madd_past)
        in_specs.append(pl.BlockSpec((None, Q, page), lambda b, j, pt: (b, 0, 0)))
        args.append(madd_new)
        for t in (bias_last, bias_new):
            in_specs.append(pl.BlockSpec((KVH, R, page), lambda b, j, pt: (0, 0, 0)))
            args.append(t)
    grid_spec = pltpu.PrefetchScalarGridSpec(
        num_scalar_prefetch=1,
        grid=(Bn, nb),
        in_specs=in_specs,
        out_specs=pl.BlockSpec((None, KVH, R, hd), lambda b, j, pt: (b, 0, 0, 0)),
        scratch_shapes=scratch,
    )
    return pl.pallas_call(
        functools.partial(_paged_attn_kernel, kind=kind, P=P, KVH=KVH, G=G, Q=Q, page=page, hd=hd, scale=scale),
        grid_spec=grid_spec,
        out_shape=jax.ShapeDtypeStruct((Bn, KVH, R, hd), F32),
        compiler_params=_cparams("arbitrary", "arbitrary"),
        name=kind + "_attn_sample",
    )(page_table, *args)


def _sortable(s):
    bits = pltpu.bitcast(s + 0.0, I32)
    return jnp.where(bits >= 0, bits, bits ^ jnp.int32(0x7FFFFFFF))


def _kth_largest_key(count_ge, rows, k):
    sign = jnp.int32(-2 ** 31)

    def body(i, t):
        cand = t | jnp.left_shift(jnp.int32(1), 31 - i)
        cnt = count_ge(cand ^ sign)
        return jnp.where(cnt >= k, cand, t)

    t = lax.fori_loop(0, 32, body, jnp.zeros((rows, 1), I32))
    return t ^ sign


def _last_tie_index(ties_before, need, rows, n_keys):
    nbits = max(int(n_keys - 1).bit_length(), 1)

    def body(i, m):
        cand = m | jnp.left_shift(jnp.int32(1), nbits - 1 - i)
        return jnp.where(ties_before(cand) < need, cand, m)

    return lax.fori_loop(0, nbits, body, jnp.zeros((rows, 1), I32))


def _dsa_score_kernel(qi_ref, ki_ref, wi_ref, o_ref, acc_ref, wb_ref, *, HI, tq, T, k_top, scale):
    iq = pl.program_id(1)
    wi = wi_ref[...]
    for h in range(HI):
        wb_ref[h] = jnp.broadcast_to(wi[:, h:h + 1], (tq, LANE))
    acc_ref[...] = jnp.zeros(acc_ref.shape, F32)
    ki = ki_ref[...]
    reps = T // LANE

    def body(h, carry):
        d = lax.dot_general(qi_ref[h], ki, _NT, preferred_element_type=F32)
        w = jnp.concatenate([wb_ref[h]] * reps, axis=1)
        acc_ref[...] += w * jnp.maximum(d, 0.0)
        return carry

    lax.fori_loop(0, HI, body, 0)
    qpos = lax.broadcasted_iota(I32, (tq, T), 0) + iq * tq
    kpos = lax.broadcasted_iota(I32, (tq, T), 1)
    adm = kpos <= qpos
    key = _sortable(jnp.where(adm, acc_ref[...] * scale, NEG_INF))

    def count_ge(thr):
        return jnp.sum(jnp.where(key >= thr, 1, 0), axis=-1, keepdims=True)

    thr = _kth_largest_key(count_ge, tq, k_top)
    sel = jnp.logical_and(key >= thr, adm)
    o_ref[...] = jnp.where(sel, 0.0, NEG_INF)

    @pl.when(jnp.max(jnp.sum(jnp.where(sel, 1, 0), axis=-1, keepdims=True)) > k_top)
    def _():
        gt = key > thr
        tie = key == thr
        need = k_top - jnp.sum(jnp.where(gt, 1, 0), axis=-1, keepdims=True)

        def ties_before(m):
            return jnp.sum(jnp.where(jnp.logical_and(tie, kpos < m), 1, 0), axis=-1, keepdims=True)

        last = _last_tie_index(ties_before, need, tq, T)
        keep = jnp.logical_or(gt, jnp.logical_and(tie, kpos <= last))
        o_ref[...] = jnp.where(jnp.logical_and(keep, adm), 0.0, NEG_INF)


def _dsa_select_prompt(qi_hm, ki16, wi, Bn, T, k_top, scale, tq=128):
    HI, M, DI = qi_hm.shape
    tq = min(tq, T)
    nq = T // tq
    wl = wi.shape[1]
    return pl.pallas_call(
        functools.partial(_dsa_score_kernel, HI=HI, tq=tq, T=T, k_top=k_top, scale=scale),
        grid=(Bn, nq),
        in_specs=[pl.BlockSpec((HI, tq, DI), lambda b, i: (0, b * nq + i, 0)),
                  pl.BlockSpec((T, DI), lambda b, i: (b, 0)),
                  pl.BlockSpec((tq, wl), lambda b, i: (b * nq + i, 0))],
        out_specs=pl.BlockSpec((tq, T), lambda b, i: (b * nq + i, 0)),
        out_shape=jax.ShapeDtypeStruct((M, T), F32),
        scratch_shapes=[pltpu.VMEM((tq, T), F32), pltpu.VMEM((HI, tq, LANE), F32)],
        compiler_params=_cparams("parallel", "arbitrary"),
        name="dsa_select_prompt",
    )(qi_hm, ki16, wi)


def _dsa_attn_kernel(q_ref, k_ref, v_ref, madd_ref, td_ref, tl_ref, o_ref, s_ref, *, G, tq, scale):
    iq = pl.program_id(1)
    hd = q_ref.shape[-1]
    k = k_ref[...]
    v = v_ref[...]
    madd = madd_ref[...]
    diag = pl.multiple_of(iq * tq, tq)
    left = pl.multiple_of(jnp.maximum(iq - 1, 0) * tq, tq)
    for g in range(G):
        s_ref[...] = lax.dot_general(q_ref[g], k, _NT, preferred_element_type=F32) * scale + madd
        s_ref[:, pl.ds(diag, tq)] += td_ref[g]

        @pl.when(iq > 0)
        def _():
            s_ref[:, pl.ds(left, tq)] += tl_ref[g]

        s = s_ref[...]
        m = jnp.max(s, axis=-1, keepdims=True)
        p = jnp.exp(s - m)
        l = jnp.sum(p, axis=-1, keepdims=True)
        o = jnp.dot(p.astype(BF16), v, preferred_element_type=F32) / l
        o_ref[:, g * hd:(g + 1) * hd] = o.astype(o_ref.dtype)


def _dsa_attn_prompt(q_hm, k16, v16, madd, t_diag, t_left, Bn, T, G, scale, tq=128):
    H, M, hd = q_hm.shape
    KVH = H // G
    tq = min(tq, T)
    nq = T // tq
    return pl.pallas_call(
        functools.partial(_dsa_attn_kernel, G=G, tq=tq, scale=scale),
        grid=(Bn, nq, KVH),
        in_specs=[pl.BlockSpec((G, tq, hd), lambda b, i, h: (h, b * nq + i, 0)),
                  pl.BlockSpec((T, hd), lambda b, i, h: (b, h)),
                  pl.BlockSpec((T, hd), lambda b, i, h: (b, h)),
                  pl.BlockSpec((tq, T), lambda b, i, h: (b * nq + i, 0)),
                  pl.BlockSpec((G, tq, tq), lambda b, i, h: (h, 0, 0)),
                  pl.BlockSpec((G, tq, tq), lambda b, i, h: (h, 0, 0))],
        out_specs=pl.BlockSpec((tq, G * hd), lambda b, i, h: (b * nq + i, h)),
        out_shape=jax.ShapeDtypeStruct((M, H * hd), BF16),
        scratch_shapes=[pltpu.VMEM((tq, T), F32)],
        compiler_params=_cparams("parallel", "parallel", "arbitrary"),
        name="dsa_attn_prompt",
    )(q_hm, k16, v16, madd, t_diag, t_left)


def _dsa_score_sample_kernel(*refs, P, HI, Q, page, k_top, scale):
    pt_ref = refs[0]
    del pt_ref
    qi_ref, wi_ref = refs[1], refs[2]
    kp_refs = refs[3:3 + P]
    kn_ref = refs[3 + P]
    mp_ref, mn_ref = refs[4 + P], refs[5 + P]
    kb_ref, sp_ref, sn_ref = refs[6 + P:9 + P]

    jb = pl.program_id(1)
    nb = pl.num_programs(1)
    n = P * page
    qi = qi_ref[...]
    wcol = wi_ref[...]

    def score(keys16):
        d = lax.dot_general(qi, keys16, _NT, preferred_element_type=F32)
        d = jnp.broadcast_to(wcol, d.shape) * jnp.maximum(d, 0.0)
        tot = d[0:Q]
        for h in range(1, HI):
            tot = tot + d[h * Q:(h + 1) * Q]
        return tot * scale

    for p in range(P):
        kb_ref[p * page:(p + 1) * page, :] = kp_refs[p][...].astype(BF16)
    sp_ref[jb] = score(kb_ref[...])

    @pl.when(jb == nb - 1)
    def _():
        qpos = lax.broadcasted_iota(I32, (Q, page), 0)
        kpos = lax.broadcasted_iota(I32, (Q, page), 1)
        adm = kpos <= qpos
        sn_ref[...] = jnp.where(adm, score(kn_ref[...].astype(BF16)), NEG_INF)
        key_p = _sortable(sp_ref[...])
        key_n = _sortable(sn_ref[...])

        def count_ge(thr):
            cp = jnp.sum(jnp.where(key_p >= thr[None], 1, 0), axis=-1, keepdims=True)
            cn = jnp.sum(jnp.where(key_n >= thr, 1, 0), axis=-1, keepdims=True)
            return jnp.sum(cp, axis=0) + cn

        thr = _kth_largest_key(count_ge, Q, k_top)
        sel_n = jnp.logical_and(key_n >= thr, adm)
        mp_ref[...] = jnp.where(key_p >= thr[None], 0.0, NEG_INF)
        mn_ref[...] = jnp.where(sel_n, 0.0, NEG_INF)

        def count(mask_p, mask_n):
            cp = jnp.sum(jnp.where(mask_p, 1, 0), axis=-1, keepdims=True)
            return jnp.sum(cp, axis=0) + jnp.sum(jnp.where(mask_n, 1, 0), axis=-1, keepdims=True)

        @pl.when(jnp.max(count(key_p >= thr[None], sel_n)) > k_top)
        def _():
            idx_p = (lax.broadcasted_iota(I32, key_p.shape, 0) * n + lax.broadcasted_iota(I32, key_p.shape, 2))
            n_past = key_p.shape[0] * n
            idx_n = kpos + n_past
            gt_p, gt_n = key_p > thr[None], key_n > thr
            tie_p, tie_n = key_p == thr[None], key_n == thr
            need = k_top - count(gt_p, gt_n)

            def ties_before(m):
                return count(jnp.logical_and(tie_p, idx_p < m[None]), jnp.logical_and(tie_n, idx_n < m))

            last = _last_tie_index(ties_before, need, Q, n_past + page)
            keep_p = jnp.logical_or(gt_p, jnp.logical_and(tie_p, idx_p <= last[None]))
            keep_n = jnp.logical_or(gt_n, jnp.logical_and(tie_n, idx_n <= last))
            mp_ref[...] = jnp.where(keep_p, 0.0, NEG_INF)
            mn_ref[...] = jnp.where(jnp.logical_and(keep_n, adm), 0.0, NEG_INF)


def _dsa_select_sample(qi, wi, pool_ki, ki_new, page_table, Q, k_top, scale):
    Bn, RQ, DI = qi.shape
    page = pool_ki.shape[1]
    n_pages = page_table.shape[1]
    P = min(PAGES_PER_STEP, n_pages)
    nb = n_pages // P
    HI = RQ // Q
    in_specs = [pl.BlockSpec((None, RQ, DI), lambda b, j, pt: (b, 0, 0)),
                pl.BlockSpec((None, RQ, 1), lambda b, j, pt: (b, 0, 0))]
    args = [qi, wi]
    for p in range(P):
        in_specs.append(pl.BlockSpec((None, page, DI), functools.partial(
            lambda b, j, pt, p: (pt[b, j * P + p], 0, 0), p=p)))
        args.append(pool_ki)
    in_specs.append(pl.BlockSpec((None, page, DI), lambda b, j, pt: (b, 0, 0)))
    args.append(ki_new)
    grid_spec = pltpu.PrefetchScalarGridSpec(
        num_scalar_prefetch=1,
        grid=(Bn, nb),
        in_specs=in_specs,
        out_specs=[pl.BlockSpec((None, nb, Q, P * page), lambda b, j, pt: (b, 0, 0, 0)),
                   pl.BlockSpec((None, Q, page), lambda b, j, pt: (b, 0, 0))],
        scratch_shapes=[pltpu.VMEM((P * page, DI), BF16), pltpu.VMEM((nb, Q, P * page), F32),
                        pltpu.VMEM((Q, page), F32)],
    )
    return pl.pallas_call(
        functools.partial(_dsa_score_sample_kernel, P=P, HI=HI, Q=Q, page=page, k_top=k_top, scale=scale),
        grid_spec=grid_spec,
        out_shape=[jax.ShapeDtypeStruct((Bn, nb, Q, P * page), F32), jax.ShapeDtypeStruct((Bn, Q, page), F32)],
        compiler_params=_cparams("arbitrary", "arbitrary"),
        name="dsa_select_sample",
    )(page_table, *args)


def _t5_bucket_np(dist):
    max_exact = N_BUCKETS // 2
    d = np.maximum(dist, 0)
    ratio = np.log(np.maximum(d, 1).astype(np.float32) / np.float32(max_exact)) / np.float32(
        math.log(MAX_DISTANCE / max_exact))
    large = np.minimum(max_exact + (ratio * (N_BUCKETS - max_exact)).astype(np.int32), N_BUCKETS - 1)
    return np.where(d < max_exact, d, large).astype(np.int32)


def _pad_cols(w, n):
    return jnp.pad(w, ((0, 0), (0, n - w.shape[1])))


def _to_rows(o, Bn, Q, KVH, G, hd):
    o = o.reshape(Bn, KVH, G, Q, hd)
    return jnp.transpose(o, (0, 3, 1, 2, 4)).reshape(Bn * Q, KVH * G * hd)


def _to_heads(q, Bn, Q, KVH, G, hd):
    q = q.reshape(Bn, Q, KVH, G, hd)
    return jnp.transpose(q, (0, 2, 3, 1, 4)).reshape(Bn, KVH, G * Q, hd)


def _pad_page(x, page):
    return jnp.pad(x, ((0, 0), (0, page - x.shape[1]), (0, 0)))


def _fox_layer(hp, hs, up, us, dims, pool_k, pool_v, pool_lf, page_table, w_q, w_k, w_v, w_f, b_f, w_o):
    Bp, T, Bs, Q, H, KVH, hd = dims
    G = H // KVH
    scale = hd ** -0.5
    page = pool_k.shape[1]
    wq, wk, wv, wo = (w.astype(BF16) for w in (w_q, w_k, w_v, w_o))
    wf_pad = _pad_cols(w_f, LANE).astype(BF16)
    bf_pad = jnp.pad(b_f, (0, LANE - H))
    wft = w_f.T.astype(BF16)
    tq = min(256, T)

    q_hm = _mm(up, wq, emit_f32=False, emit_bf16=True, head_major=True, name="fox_q")
    k32, k16 = _mm(up, wk, emit_bf16=True, name="fox_k")
    v32, v16 = _mm(up, wv, emit_bf16=True, name="fox_v")
    lf_pad = _mm(up, wf_pad, ls_bias=bf_pad, tn=LANE, name="fox_logf")
    cum = _fox_cum(up, wft, b_f, Bp, T, tq)
    cum = jnp.transpose(cum.reshape(Bp, T // tq, KVH, G, 1, tq), (0, 2, 1, 3, 4, 5))
    o = _fox_attn_prompt(q_hm, k16, v16, cum, Bp, T, G, scale, tq=tq)
    hp = _mm(o, wo, res=hp, name="fox_o")

    qs = _mm(us, wq, name="fox_q_s")
    ks = _mm(us, wk, name="fox_k_s")
    vs = _mm(us, wv, name="fox_v_s")
    lfs_pad = _mm(us, wf_pad, ls_bias=bf_pad, tn=LANE, name="fox_logf_s")
    lfs = lfs_pad[:, :H]
    q_t = _to_heads(qs, Bs, Q, KVH, G, hd).astype(BF16)
    lf_newt = jnp.transpose(_pad_page(lfs.reshape(Bs, Q, H), page), (0, 2, 1))
    o_s = _paged_attn("fox", q_t, pool_k.reshape(pool_k.shape[0], page, KVH * hd),
                      pool_v.reshape(pool_v.shape[0], page, KVH * hd),
                      _pad_page(ks.reshape(Bs, Q, KVH * hd), page), _pad_page(vs.reshape(Bs, Q, KVH * hd), page),
                      page_table, (jnp.transpose(pool_lf, (0, 2, 1)), lf_newt), scale)
    hs = _mm(_to_rows(o_s, Bs, Q, KVH, G, hd).astype(BF16), wo, res=hs, name="fox_o_s")

    rows = (k32.reshape(Bp, T, KVH, hd), v32.reshape(Bp, T, KVH, hd), lf_pad[:, :H].reshape(Bp, T, H),
            ks.reshape(Bs, Q, KVH, hd), vs.reshape(Bs, Q, KVH, hd), lfs.reshape(Bs, Q, H))
    return hp, hs, rows


def _sconv_layer(hp, hs, up, us, dims, state, w_in, w_conv, w_out):
    Bp, T, Bs, Q = dims[:4]
    D = w_out.shape[0]
    win = w_in.astype(BF16)
    wout = w_out.astype(BF16)
    offs = (0, D, 2 * D)
    zp, stp = _gconv(up, win, offs, D, w_conv, mode="sconv", seq_len=T, name="sconv_in")
    hp = _mm(zp, wout, res=hp, name="sconv_out")
    zs, sts = _gconv(us, win, offs, D, w_conv, mode="sconv", seq_len=Q, prefix=_conv_prefix(state, Q),
                     name="sconv_in_s")
    hs = _mm(zs, wout, res=hs, name="sconv_out_s")
    nblk = stp.shape[0] // Bp
    sp = stp.reshape(Bp, nblk, SUBLANE, D)[:, -1, SUBLANE - 2:, :]
    ss = sts.reshape(Bs, Q, D)[:, Q - 2:, :]
    return hp, hs, sp, ss


def _dsa_layer(hp, hs, up, us, dims, pool_k, pool_v, pool_ki, page_table, rel_bias,
               w_q, w_k, w_v, w_o, w_qi, w_ki, w_wi):
    Bp, T, Bs, Q, H, KVH, hd = dims
    G = H // KVH
    scale = hd ** -0.5
    page = pool_k.shape[1]
    DI = w_ki.shape[1]
    HI = w_wi.shape[1]
    idx_scale = (DI * HI) ** -0.5
    past = page_table.shape[1] * page
    wq, wk, wv, wo, wqi = (w.astype(BF16) for w in (w_q, w_k, w_v, w_o, w_qi))
    wkw = jnp.concatenate([w_ki, _pad_cols(w_wi, LANE)], axis=1).astype(BF16)
    c_far = rel_bias[N_BUCKETS - 1]

    tq = min(128, T)
    q_hm = _mm(up, wq, emit_f32=False, emit_bf16=True, head_major=True, name="dsa_q")
    k32, k16 = _mm(up, wk, emit_bf16=True, name="dsa_k")
    v32, v16 = _mm(up, wv, emit_bf16=True, name="dsa_v")
    qi_hm = _mm(up, wqi, emit_f32=False, emit_bf16=True, head_major=True, name="dsa_qi")
    kw = _mm(up, wkw, tn=DI + LANE, name="dsa_kiwi")
    ki32 = kw[:, :DI]
    wi = kw[:, DI:]
    madd = _dsa_select_prompt(qi_hm, ki32.astype(BF16), wi, Bp, T, min(TOPK_MAX, T // 4), idx_scale, tq=tq)
    ii = np.arange(tq)[:, None]
    jj = np.arange(tq)[None, :]
    t_diag = jnp.transpose(rel_bias[_t5_bucket_np(ii - jj)] - c_far, (2, 0, 1))
    t_left = jnp.transpose(rel_bias[_t5_bucket_np(tq + ii - jj)] - c_far, (2, 0, 1))
    o = _dsa_attn_prompt(q_hm, k16, v16, madd, t_diag, t_left, Bp, T, G, scale, tq=tq)
    hp = _mm(o, wo, res=hp, name="dsa_o")

    qs = _mm(us, wq, name="dsa_q_s")
    ks = _mm(us, wk, name="dsa_k_s")
    vs = _mm(us, wv, name="dsa_v_s")
    qis = _mm(us, wqi, name="dsa_qi_s")
    kws = _mm(us, wkw, tn=DI + LANE, name="dsa_kiwi_s")
    kis = kws[:, :DI]
    wis = kws[:, DI:DI + HI]
    qi_t = jnp.transpose(qis.reshape(Bs, Q, HI, DI), (0, 2, 1, 3)).reshape(Bs, HI * Q, DI).astype(BF16)
    wi_t = jnp.transpose(wis.reshape(Bs, Q, HI), (0, 2, 1)).reshape(Bs, HI * Q, 1)
    mp, mn = _dsa_select_sample(qi_t, wi_t, pool_ki, _pad_page(kis.reshape(Bs, Q, DI), page), page_table,
                                Q, min(TOPK_MAX, (past + Q) // 4), idx_scale)
    nb, n = mp.shape[1], mp.shape[3]
    madd_past = jnp.transpose(mp, (0, 2, 1, 3)).reshape(Bs, Q, nb * n)
    qq = np.arange(Q)[:, None]
    cc = np.arange(page)[None, :]

    def sample_bias(dist):
        t = rel_bias[_t5_bucket_np(dist)] - c_far
        t = jnp.transpose(t.reshape(Q, page, KVH, G), (2, 3, 0, 1))
        return t.reshape(KVH, G * Q, page)

    bias_last = sample_bias(past + qq - (past - page + cc))
    bias_new = sample_bias(qq - cc)
    q_t = _to_heads(qs, Bs, Q, KVH, G, hd).astype(BF16)
    o_s = _paged_attn("dsa", q_t, pool_k.reshape(pool_k.shape[0], page, KVH * hd),
                      pool_v.reshape(pool_v.shape[0], page, KVH * hd),
                      _pad_page(ks.reshape(Bs, Q, KVH * hd), page), _pad_page(vs.reshape(Bs, Q, KVH * hd), page),
                      page_table, (madd_past, mn, bias_last, bias_new), scale)
    hs = _mm(_to_rows(o_s, Bs, Q, KVH, G, hd).astype(BF16), wo, res=hs, name="dsa_o_s")

    rows = (k32.reshape(Bp, T, KVH, hd), v32.reshape(Bp, T, KVH, hd), ki32.reshape(Bp, T, DI),
            ks.reshape(Bs, Q, KVH, hd), vs.reshape(Bs, Q, KVH, hd), kis.reshape(Bs, Q, DI))
    return hp, hs, rows


def _ffn_layer(hp, hs, vp, vs, dims, state, w_up, w_conv, w_down):
    Bp, T, Bs, Q = dims[:4]
    DFF = w_down.shape[0]
    ncp = ((DFF + 1023) // 1024) * 1024 if DFF > 1024 else DFF
    wup = jnp.concatenate([_pad_cols(w_up[:, :DFF], ncp), _pad_cols(w_up[:, DFF:], ncp)], axis=1).astype(BF16)
    wdown = jnp.pad(w_down, ((0, ncp - DFF), (0, 0))).astype(BF16)
    wc = _pad_cols(w_conv, ncp)
    tk = ncp // 4 if ncp % (4 * LANE) == 0 and ncp > 2048 else ncp
    ap, stp = _gconv(vp, wup, (0, ncp), ncp, wc, mode="ffn", seq_len=T, name="ffn_up")
    hp = _mm(ap, wdown, res=hp, tn=1024, tk=tk, name="ffn_down")
    prefix = _conv_prefix(jnp.pad(state, ((0, 0), (0, 0), (0, ncp - DFF))), Q)
    a_s, sts = _gconv(vs, wup, (0, ncp), ncp, wc, mode="ffn", seq_len=Q, prefix=prefix, name="ffn_up_s")
    hs = _mm(a_s, wdown, res=hs, tn=1024, tk=tk, name="ffn_down_s")
    nblk = stp.shape[0] // Bp
    cp = stp.reshape(Bp, nblk, SUBLANE, ncp)[:, -1, SUBLANE - 2:, :DFF]
    cs = sts.reshape(Bs, Q, ncp)[:, Q - 2:, :DFF]
    return hp, hs, cp, cs


def kernel(x_prompt, x_sample, cache_fox_k, cache_fox_v, cache_fox_logf, state_sconv, cache_dsa_k, cache_dsa_v,
           cache_dsa_kidx, state_ffn_conv, page_table, rel_bias, norm_mix, norm_ffn, norm_final,
           fox_w_q, fox_w_k, fox_w_v, fox_w_f, fox_b_f, fox_w_o, sc_w_in, sc_w_conv, sc_w_out,
           dsa_w_q, dsa_w_k, dsa_w_v, dsa_w_o, dsa_w_qi, dsa_w_ki, dsa_w_wi, ffn_w_up, ffn_w_conv, ffn_w_down):
    Bp, T, D = x_prompt.shape
    Bs, Q, _ = x_sample.shape
    depth = norm_mix.shape[0]
    KVH, hd = cache_fox_k.shape[3], cache_fox_k.shape[4]
    H = fox_w_f.shape[2]
    dims = (Bp, T, Bs, Q, H, KVH, hd)
    hp = x_prompt.reshape(Bp * T, D)
    hs = x_sample.reshape(Bs * Q, D)
    fox_new = ([], [], [], [], [], [])
    sc_new = ([], [])
    dsa_new = ([], [], [], [], [], [])
    ffn_new = ([], [])
    for i in range(depth):
        j, kind = i // 3, i % 3
        up = _rmsnorm(hp, norm_mix[i], BF16)
        us = _rmsnorm(hs, norm_mix[i], BF16)
        if kind == 0:
            hp, hs, rows = _fox_layer(hp, hs, up, us, dims, cache_fox_k[j], cache_fox_v[j], cache_fox_logf[j],
                                      page_table, fox_w_q[j], fox_w_k[j], fox_w_v[j], fox_w_f[j], fox_b_f[j],
                                      fox_w_o[j])
            for lst, a in zip(fox_new, rows):
                lst.append(a)
        elif kind == 1:
            hp, hs, sp, ss = _sconv_layer(hp, hs, up, us, dims, state_sconv[j], sc_w_in[j], sc_w_conv[j],
                                          sc_w_out[j])
            sc_new[0].append(sp)
            sc_new[1].append(ss)
        else:
            hp, hs, rows = _dsa_layer(hp, hs, up, us, dims, cache_dsa_k[j], cache_dsa_v[j], cache_dsa_kidx[j],
                                      page_table, rel_bias, dsa_w_q[j], dsa_w_k[j], dsa_w_v[j], dsa_w_o[j],
                                      dsa_w_qi[j], dsa_w_ki[j], dsa_w_wi[j])
            for lst, a in zip(dsa_new, rows):
                lst.append(a)
        vp = _rmsnorm(hp, norm_ffn[i], BF16)
        vs = _rmsnorm(hs, norm_ffn[i], BF16)
        hp, hs, cp, cs = _ffn_layer(hp, hs, vp, vs, dims, state_ffn_conv[i], ffn_w_up[i], ffn_w_conv[i],
                                    ffn_w_down[i])
        ffn_new[0].append(cp)
        ffn_new[1].append(cs)
    y_prompt = _rmsnorm(hp, norm_final, F32).reshape(Bp, T, D)
    y_sample = _rmsnorm(hs, norm_final, F32).reshape(Bs, Q, D)
    return (y_prompt, y_sample,
            jnp.stack(fox_new[0]), jnp.stack(fox_new[1]), jnp.stack(fox_new[2]),
            jnp.stack(fox_new[3]), jnp.stack(fox_new[4]), jnp.stack(fox_new[5]),
            jnp.stack(sc_new[0]), jnp.stack(sc_new[1]),
            jnp.stack(dsa_new[0]), jnp.stack(dsa_new[1]), jnp.stack(dsa_new[2]),
            jnp.stack(dsa_new[3]), jnp.stack(dsa_new[4]), jnp.stack(dsa_new[5]),
            jnp.stack(ffn_new[0]), jnp.stack(ffn_new[1]))
```

```python
import functools
import math

import numpy as np
import jax
import jax.numpy as jnp
from jax import lax
from jax.experimental import pallas as pl
from jax.experimental.pallas import tpu as pltpu

F32 = jnp.float32
BF16 = jnp.bfloat16
I32 = jnp.int32

RMS_EPS = 1e-6
NEG_INF = -1e30
TOPK_MAX = 256
N_BUCKETS = 32
MAX_DISTANCE = 128
CONV_WIDTH = 3
LOG2E = 1.4426950408889634

LANE = 128
SUBLANE = 8
VMEM_LIMIT_BYTES = 56 * 1024 * 1024
PAGES_PER_STEP = 8

_NT = (((1,), (1,)), ((), ()))


def _cparams(*sem):
    return pltpu.CompilerParams(dimension_semantics=sem, vmem_limit_bytes=VMEM_LIMIT_BYTES)


def _log_sigmoid(x):
    return jnp.minimum(x, 0.0) - jnp.log1p(jnp.exp(-jnp.abs(x)))


def _split3(x):
    hi = x.astype(BF16)
    r1 = x - hi.astype(F32)
    mid = r1.astype(BF16)
    lo = (r1 - mid.astype(F32)).astype(BF16)
    return hi, mid, lo


def _tri_cumsum(x):
    n = x.shape[-1]
    r = lax.broadcasted_iota(I32, (n, n), 0)
    c = lax.broadcasted_iota(I32, (n, n), 1)
    tri = jnp.where(r <= c, 1.0, 0.0).astype(BF16)
    hi, mid, lo = _split3(x)
    out = jnp.dot(hi, tri, preferred_element_type=F32)
    out = out + jnp.dot(mid, tri, preferred_element_type=F32)
    out = out + jnp.dot(lo, tri, preferred_element_type=F32)
    return out


def _rms_kernel(x_ref, g_ref, o_ref):
    x = x_ref[...]
    ms = jnp.mean(x * x, axis=-1, keepdims=True)
    o_ref[...] = ((x * lax.rsqrt(ms + RMS_EPS)) * g_ref[...]).astype(o_ref.dtype)


def _rmsnorm(h, g, out_dtype):
    M, D = h.shape
    tm = min(512, M)
    return pl.pallas_call(
        _rms_kernel,
        grid=(M // tm,),
        in_specs=[pl.BlockSpec((tm, D), lambda i: (i, 0)), pl.BlockSpec((1, D), lambda i: (0, 0))],
        out_specs=pl.BlockSpec((tm, D), lambda i: (i, 0)),
        out_shape=jax.ShapeDtypeStruct((M, D), out_dtype),
        compiler_params=_cparams("parallel"),
        name="rmsnorm",
    )(h, g.reshape(1, D).astype(F32))


def _mm_kernel(x_ref, w_ref, r_ref, o_ref):
    o_ref[...] = r_ref[...] + jnp.dot(x_ref[...], w_ref[...], preferred_element_type=F32)


def _mm(x, w, layer, res, *, tm=512, tn=512, name="mm"):
    M, K = x.shape
    N = w.shape[2]
    tm = min(tm, M)
    tn = min(tn, N)
    assert M % tm == 0 and N % tn == 0, (M, N, tm, tn)
    return pl.pallas_call(
        _mm_kernel,
        grid=(M // tm, N // tn),
        in_specs=[pl.BlockSpec((tm, K), lambda i, j: (i, 0)),
                  pl.BlockSpec((None, K, tn), lambda i, j: (layer, 0, j)),
                  pl.BlockSpec((tm, tn), lambda i, j: (i, j))],
        out_specs=pl.BlockSpec((tm, tn), lambda i, j: (i, j)),
        out_shape=jax.ShapeDtypeStruct((M, N), F32),
        compiler_params=_cparams("parallel", "parallel"),
        name=name,
    )(x, w, res)


def _mmw_kernel(*refs, has_res, has_ls, emit_f32, emit_bf16, head_major, has_tail):
    it = iter(refs)
    x_ref = next(it)
    w_ref = next(it)
    b_ref = next(it) if has_ls else None
    r_ref = next(it) if has_res else None
    xs_ref = next(it) if has_tail else None
    rs_ref = next(it) if has_tail and has_res else None
    o32_ref = next(it) if emit_f32 else None
    o16_ref = next(it) if emit_bf16 else None
    os_ref = next(it) if has_tail else None
    wbf_ref = next(it)

    def project(x, r):
        acc = jnp.dot(x, wbf_ref[...], preferred_element_type=F32)
        if has_ls:
            acc = _log_sigmoid(acc + b_ref[...])
        if r is not None:
            acc = r[...] + acc
        return acc

    @pl.when(pl.program_id(1) == 0)
    def _():
        wbf_ref[...] = w_ref[...].astype(BF16)
        if has_tail:
            os_ref[...] = project(xs_ref[...], rs_ref)

    acc = project(x_ref[...], r_ref)
    if emit_f32:
        o32_ref[...] = acc
    if emit_bf16:
        if head_major:
            for hh in range(o16_ref.shape[0]):
                o16_ref[hh] = acc[:, hh * LANE:(hh + 1) * LANE].astype(BF16)
        else:
            o16_ref[...] = acc.astype(BF16)


def _mmw(x, w, layer, *, xs=None, res=None, res_s=None, ls_bias=None, emit_f32=True, emit_bf16=False,
         head_major=False, tm=1024, tn=512, name="mmw"):
    M, K = x.shape
    N = w.shape[2]
    tm = min(tm, M)
    tn = min(tn, N)
    assert M % tm == 0 and N % tn == 0, (M, N, tm, tn)
    has_tail = xs is not None
    in_specs = [pl.BlockSpec((tm, K), lambda j, i: (i, 0)), pl.BlockSpec((None, K, tn), lambda j, i: (layer, 0, j))]
    args = [x, w]
    if ls_bias is not None:
        in_specs.append(pl.BlockSpec((1, tn), lambda j, i: (0, j)))
        args.append(ls_bias.reshape(1, N).astype(F32))
    if res is not None:
        in_specs.append(pl.BlockSpec((tm, tn), lambda j, i: (i, j)))
        args.append(res)
    if has_tail:
        Ms = xs.shape[0]
        in_specs.append(pl.BlockSpec((Ms, K), lambda j, i: (0, 0)))
        args.append(xs)
        if res is not None:
            in_specs.append(pl.BlockSpec((Ms, tn), lambda j, i: (0, j)))
            args.append(res_s)
    out_specs, out_shape = [], []
    if emit_f32:
        out_specs.append(pl.BlockSpec((tm, tn), lambda j, i: (i, j)))
        out_shape.append(jax.ShapeDtypeStruct((M, N), F32))
    if emit_bf16:
        if head_major:
            out_specs.append(pl.BlockSpec((tn // LANE, tm, LANE), lambda j, i: (j, i, 0)))
            out_shape.append(jax.ShapeDtypeStruct((N // LANE, M, LANE), BF16))
        else:
            out_specs.append(pl.BlockSpec((tm, tn), lambda j, i: (i, j)))
            out_shape.append(jax.ShapeDtypeStruct((M, N), BF16))
    if has_tail:
        out_specs.append(pl.BlockSpec((Ms, tn), lambda j, i: (0, j)))
        out_shape.append(jax.ShapeDtypeStruct((Ms, N), F32))
    outs = pl.pallas_call(
        functools.partial(_mmw_kernel, has_res=res is not None, has_ls=ls_bias is not None, emit_f32=emit_f32,
                          emit_bf16=emit_bf16, head_major=head_major, has_tail=has_tail),
        grid=(N // tn, M // tm),
        in_specs=in_specs,
        out_specs=out_specs,
        out_shape=out_shape,
        scratch_shapes=[pltpu.VMEM((K, tn), BF16)],
        compiler_params=_cparams("arbitrary", "arbitrary"),
        name=name,
    )(*args)
    return outs[0] if len(outs) == 1 else tuple(outs)


def _gconv_kernel(x_ref, *refs, mode, nw, tm, seq_blocks, tail_len):
    w_refs = refs[:nw]
    wc_ref, xs_ref, p1_ref, p2_ref, o_ref, st_ref, os_ref, sts_ref, wbf_ref, buf_ref, carry_ref, bufs_ref = refs[nw:]
    i = pl.program_id(1)
    ms = xs_ref.shape[0]

    def gated(x):
        ys = [jnp.dot(x, wbf_ref[k], preferred_element_type=F32) for k in range(nw)]
        if mode == "ffn":
            return ys[0], ys[1]
        return ys[1] * ys[2], ys[0]

    def finish(cin, other, x1, x2):
        wc = wc_ref[...]
        y = wc[0:1, :] * x2 + wc[1:2, :] * x1 + wc[2:3, :] * cin
        if mode == "ffn":
            return ((y * (1.0 / (1.0 + jnp.exp(-y)))) * other).astype(BF16)
        return (other * y).astype(BF16)

    @pl.when(i == 0)
    def _():
        for k in range(nw):
            wbf_ref[k] = w_refs[k][...].astype(BF16)
        cin, other = gated(xs_ref[...])
        bufs_ref[0:SUBLANE, :] = jnp.zeros((SUBLANE, cin.shape[1]), F32)
        bufs_ref[SUBLANE:SUBLANE + ms, :] = cin
        rmod = lax.broadcasted_iota(I32, (ms, 1), 0) % tail_len
        x1 = jnp.where(rmod >= 1, bufs_ref[SUBLANE - 1:SUBLANE - 1 + ms, :], p1_ref[...])
        x2 = jnp.where(rmod >= 2, bufs_ref[SUBLANE - 2:SUBLANE - 2 + ms, :], p2_ref[...])
        sts_ref[...] = cin
        os_ref[...] = finish(cin, other, x1, x2)

    cin, other = gated(x_ref[...])
    first = (i % seq_blocks) == 0

    @pl.when(first)
    def _():
        buf_ref[0:SUBLANE, :] = jnp.zeros((SUBLANE, cin.shape[1]), F32)

    @pl.when(jnp.logical_not(first))
    def _():
        buf_ref[0:SUBLANE, :] = carry_ref[...]

    buf_ref[SUBLANE:SUBLANE + tm, :] = cin
    tail = cin[tm - SUBLANE:tm, :]
    carry_ref[...] = tail
    st_ref[...] = tail
    o_ref[...] = finish(cin, other, buf_ref[SUBLANE - 1:SUBLANE - 1 + tm, :], buf_ref[SUBLANE - 2:SUBLANE - 2 + tm, :])


def _gconv(x, xs, w, layer, col_offsets, nc, wconv, prefix, *, mode, seq_len, tail_len, tm, tn=256, name="gconv"):
    M, K = x.shape
    Ms = xs.shape[0]
    tm = min(tm, seq_len)
    tn = min(tn, nc)
    assert M % tm == 0 and nc % tn == 0 and seq_len % tm == 0
    nw = len(col_offsets)
    in_specs = [pl.BlockSpec((tm, K), lambda j, i: (i, 0))]
    args = [x]
    for off in col_offsets:
        assert off % tn == 0
        in_specs.append(pl.BlockSpec((None, K, tn), functools.partial(lambda j, i, o: (layer, 0, o + j), o=off // tn)))
        args.append(w)
    in_specs.append(pl.BlockSpec((None, CONV_WIDTH, tn), lambda j, i: (layer, 0, j)))
    args.append(wconv)
    in_specs.append(pl.BlockSpec((Ms, K), lambda j, i: (0, 0)))
    args.append(xs)
    for p in prefix:
        in_specs.append(pl.BlockSpec((Ms, tn), lambda j, i: (0, j)))
        args.append(p)
    return pl.pallas_call(
        functools.partial(_gconv_kernel, mode=mode, nw=nw, tm=tm, seq_blocks=seq_len // tm, tail_len=tail_len),
        grid=(nc // tn, M // tm),
        in_specs=in_specs,
        out_specs=[pl.BlockSpec((tm, tn), lambda j, i: (i, j)),
                   pl.BlockSpec((None, SUBLANE, tn), lambda j, i: (i, 0, j)),
                   pl.BlockSpec((Ms, tn), lambda j, i: (0, j)),
                   pl.BlockSpec((Ms, tn), lambda j, i: (0, j))],
        out_shape=[jax.ShapeDtypeStruct((M, nc), BF16), jax.ShapeDtypeStruct((M // tm, SUBLANE, nc), F32),
                   jax.ShapeDtypeStruct((Ms, nc), BF16), jax.ShapeDtypeStruct((Ms, nc), F32)],
        scratch_shapes=[pltpu.VMEM((nw, K, tn), BF16), pltpu.VMEM((SUBLANE + tm, tn), F32),
                        pltpu.VMEM((SUBLANE, tn), F32), pltpu.VMEM((SUBLANE + Ms, tn), F32)],
        compiler_params=_cparams("arbitrary", "arbitrary"),
        name=name,
    )(*args)


def _conv_prefix(state, seq_len):
    Bn, _, C = state.shape
    z = jnp.zeros((Bn, seq_len, C), F32)
    p1 = z.at[:, 0].set(state[:, 1])
    p2 = z.at[:, 0].set(state[:, 0]).at[:, 1].set(state[:, 1])
    return p1.reshape(Bn * seq_len, C), p2.reshape(Bn * seq_len, C)


def _cum_kernel(u_ref, wft_ref, bf_ref, o_ref, carry_ref):
    t = pl.program_id(1)

    @pl.when(t == 0)
    def _():
        carry_ref[...] = jnp.zeros_like(carry_ref)

    z = lax.dot_general(wft_ref[...], u_ref[...], _NT, preferred_element_type=F32) + bf_ref[...]
    c = _tri_cumsum(_log_sigmoid(z)) + carry_ref[...]
    o_ref[...] = c
    carry_ref[...] = c[:, c.shape[1] - 1:c.shape[1]]


def _fox_cum(u, wft, bf, Bn, T, tk):
    M, D = u.shape
    H = wft.shape[0]
    nt = T // tk
    return pl.pallas_call(
        _cum_kernel,
        grid=(Bn, nt),
        in_specs=[pl.BlockSpec((tk, D), lambda b, t: (b * nt + t, 0)),
                  pl.BlockSpec((H, D), lambda b, t: (0, 0)),
                  pl.BlockSpec((H, 1), lambda b, t: (0, 0))],
        out_specs=pl.BlockSpec((None, None, H, tk), lambda b, t: (b, t, 0, 0)),
        out_shape=jax.ShapeDtypeStruct((Bn, nt, H, tk), F32),
        scratch_shapes=[pltpu.VMEM((H, 1), F32)],
        compiler_params=_cparams("arbitrary", "arbitrary"),
        name="fox_cum",
    )(u, wft, bf.reshape(H, 1).astype(F32))


def _fox_attn_kernel(q_ref, k_ref, v_ref, cum_ref, o_ref, m_ref, l_ref, acc_ref, *, G, tq, tk, scale):
    qi = pl.program_id(2)
    hd = q_ref.shape[-1]
    reps = tk // LANE
    m_ref[...] = jnp.full(m_ref.shape, NEG_INF, F32)
    l_ref[...] = jnp.zeros(l_ref.shape, F32)
    acc_ref[...] = jnp.zeros(acc_ref.shape, F32)
    n_full = (qi * tq) // tk

    def tile(j, masked):
        start = pl.multiple_of(j * tk, tk)
        kt = k_ref[pl.ds(start, tk), :]
        vt = v_ref[pl.ds(start, tk), :]
        if masked:
            row = lax.broadcasted_iota(I32, (tq, tk), 0) + qi * tq
            col = lax.broadcasted_iota(I32, (tq, tk), 1) + j * tk
            visible = col <= row
        for g in range(G):
            s = lax.dot_general(q_ref[g], kt, _NT, preferred_element_type=F32) * (scale * LOG2E)
            s = s - cum_ref[j, g] * LOG2E
            if masked:
                s = jnp.where(visible, s, NEG_INF)
            m_prev = m_ref[g]
            m_new = jnp.maximum(m_prev, jnp.max(s, axis=1, keepdims=True))
            p = jnp.exp2(s - jnp.concatenate([m_new] * reps, axis=1))
            alpha = jnp.exp2(m_prev - m_new)
            l_ref[g] = alpha * l_ref[g] + jnp.sum(p, axis=1, keepdims=True)
            acc_ref[g] = alpha * acc_ref[g] + jnp.dot(p.astype(BF16), vt, preferred_element_type=F32)
            m_ref[g] = m_new

    def body(j, carry):
        tile(j, False)
        return carry

    lax.fori_loop(0, n_full, body, 0)
    tile(n_full, True)
    for g in range(G):
        o_ref[:, g * hd:(g + 1) * hd] = (acc_ref[g] / l_ref[g]).astype(o_ref.dtype)


def _fox_attn_prompt(q_hm, k16, v16, cum, Bn, T, G, scale, tq, tk):
    H, M, hd = q_hm.shape
    assert hd == LANE and tk % tq == 0
    KVH = H // G
    nq = T // tq
    return pl.pallas_call(
        functools.partial(_fox_attn_kernel, G=G, tq=tq, tk=tk, scale=scale),
        grid=(Bn, KVH, nq),
        in_specs=[pl.BlockSpec((G, tq, hd), lambda b, h, i: (h, b * nq + i, 0)),
                  pl.BlockSpec((T, hd), lambda b, h, i: (b, h)),
                  pl.BlockSpec((T, hd), lambda b, h, i: (b, h)),
                  pl.BlockSpec((None, None, T // tk, G, 1, tk), lambda b, h, i: (b, h, 0, 0, 0, 0))],
        out_specs=pl.BlockSpec((tq, G * hd), lambda b, h, i: (b * nq + i, h)),
        out_shape=jax.ShapeDtypeStruct((M, H * hd), BF16),
        scratch_shapes=[pltpu.VMEM((G, tq, LANE), F32), pltpu.VMEM((G, tq, LANE), F32),
                        pltpu.VMEM((G, tq, hd), F32)],
        compiler_params=_cparams("parallel", "parallel", "arbitrary"),
        name="fox_attn_prompt",
    )(q_hm, k16, v16, cum)


def _paged_attn_kernel(*refs, kind, P, KVH, G, Q, page, hd, scale):
    pt_ref = refs[0]
    del pt_ref
    pos = 1
    q_ref = refs[pos]; pos += 1
    kp_refs = refs[pos:pos + P]; pos += P
    vp_refs = refs[pos:pos + P]; pos += P
    kn_ref, vn_ref = refs[pos], refs[pos + 1]; pos += 2
    if kind == "fox":
        lf_refs = refs[pos:pos + P]; pos += P
        lfn_ref = refs[pos]; pos += 1
    else:
        mp_ref, mn_ref, tl_ref, tn_ref = refs[pos:pos + 4]; pos += 4
    o_ref = refs[pos]; pos += 1
    kb_ref, vb_ref, m_ref, l_ref, acc_ref = refs[pos:pos + 5]; pos += 5
    if kind == "fox":
        cum_ref, carry_ref = refs[pos:pos + 2]

    jb = pl.program_id(1)
    nb = pl.num_programs(1)
    R = G * Q

    @pl.when(jb == 0)
    def _():
        m_ref[...] = jnp.full(m_ref.shape, NEG_INF, F32)
        l_ref[...] = jnp.zeros(l_ref.shape, F32)
        acc_ref[...] = jnp.zeros(acc_ref.shape, F32)
        if kind == "fox":
            carry_ref[...] = jnp.zeros(carry_ref.shape, F32)

    for p in range(P):
        kb_ref[p * page:(p + 1) * page, :] = kp_refs[p][...].astype(BF16)
        vb_ref[p * page:(p + 1) * page, :] = vp_refs[p][...].astype(BF16)
        if kind == "fox":
            c = _tri_cumsum(lf_refs[p][...]) + carry_ref[...]
            cum_ref[:, p * page:(p + 1) * page] = c
            carry_ref[...] = c[:, page - 1:page]

    def update(kvh, s, vt):
        m_prev = m_ref[kvh]
        m_new = jnp.maximum(m_prev, jnp.max(s, axis=-1, keepdims=True))
        alpha = jnp.exp(m_prev - m_new)
        pr = jnp.exp(s - m_new)
        l_ref[kvh] = alpha * l_ref[kvh] + jnp.sum(pr, axis=-1, keepdims=True)
        acc_ref[kvh] = alpha * acc_ref[kvh] + jnp.dot(pr.astype(BF16), vt, preferred_element_type=F32)
        m_ref[kvh] = m_new

    def head_rows(tile, kvh):
        return jnp.concatenate(
            [jnp.broadcast_to(tile[kvh * G + g:kvh * G + g + 1, :], (Q, tile.shape[1])) for g in range(G)], axis=0)

    def query_rows(tile):
        return jnp.concatenate([tile] * G, axis=0)

    n = P * page
    if kind == "fox":
        cum = cum_ref[...]
    else:
        madd = mp_ref[...]
        is_last = jnp.where(jb == nb - 1, 1.0, 0.0)
    for kvh in range(KVH):
        q = q_ref[kvh]
        s = lax.dot_general(q, kb_ref[:, kvh * hd:(kvh + 1) * hd], _NT, preferred_element_type=F32) * scale
        if kind == "fox":
            s = s - head_rows(cum, kvh)
        else:
            s = s + query_rows(madd)
            tail = s[:, n - page:] + is_last * tl_ref[kvh]
            s = tail if n == page else jnp.concatenate([s[:, :n - page], tail], axis=1)
        update(kvh, s, vb_ref[:, kvh * hd:(kvh + 1) * hd])

    @pl.when(jb == nb - 1)
    def _():
        kn = kn_ref[...].astype(BF16)
        vn = vn_ref[...].astype(BF16)
        qpos = lax.broadcasted_iota(I32, (R, page), 0) % Q
        kpos = lax.broadcasted_iota(I32, (R, page), 1)
        causal = kpos <= qpos
        if kind == "fox":
            cn = _tri_cumsum(lfn_ref[...]) + carry_ref[...]
        else:
            mnew = mn_ref[...]
        for kvh in range(KVH):
            q = q_ref[kvh]
            s = lax.dot_general(q, kn[:, kvh * hd:(kvh + 1) * hd], _NT, preferred_element_type=F32) * scale
            if kind == "fox":
                s = s - head_rows(cn, kvh)
            else:
                s = s + query_rows(mnew) + tn_ref[kvh]
            s = jnp.where(causal, s, NEG_INF)
            update(kvh, s, vn[:, kvh * hd:(kvh + 1) * hd])
        for kvh in range(KVH):
            o_ref[kvh] = acc_ref[kvh] / l_ref[kvh]


def _paged_attn(kind, q, pool_k, pool_v, layer, k_new, v_new, page_table, extras, scale):
    Bn, KVH, R, hd = q.shape
    page = pool_k.shape[2]
    n_pages = page_table.shape[1]
    P = min(PAGES_PER_STEP, n_pages)
    assert n_pages % P == 0
    nb = n_pages // P
    H = None
    if kind == "fox":
        H = extras[0].shape[2]
        G = H // KVH
    else:
        G = extras[2].shape[1] // extras[0].shape[1]
    Q = R // G

    def page_map(p):
        return lambda b, j, pt: (layer, pt[b, j * P + p], 0, 0)

    in_specs = [pl.BlockSpec((None, KVH, R, hd), lambda b, j, pt: (b, 0, 0, 0))]
    args = [q]
    for pool in (pool_k, pool_v):
        for p in range(P):
            in_specs.append(pl.BlockSpec((None, None, page, KVH * hd), page_map(p)))
            args.append(pool)
    for new in (k_new, v_new):
        in_specs.append(pl.BlockSpec((None, page, KVH * hd), lambda b, j, pt: (b, 0, 0)))
        args.append(new)
    scratch = [pltpu.VMEM((P * page, KVH * hd), BF16), pltpu.VMEM((P * page, KVH * hd), BF16),
               pltpu.VMEM((KVH, R, 1), F32), pltpu.VMEM((KVH, R, 1), F32), pltpu.VMEM((KVH, R, hd), F32)]
    if kind == "fox":
        pool_lft, lf_newt = extras
        for p in range(P):
            in_specs.append(pl.BlockSpec((None, None, H, page), page_map(p)))
            args.append(pool_lft)
        in_specs.append(pl.BlockSpec((None, H, page), lambda b, j, pt: (b, 0, 0)))
        args.append(lf_newt)
        scratch += [pltpu.VMEM((H, P * page), F32), pltpu.VMEM((H, 1), F32)]
    else:
        madd_past, madd_new, bias_last, bias_new = extras
        in_specs.append(pl.BlockSpec((None, Q, P * page), lambda b, j, pt: (b, 0, j)))
        args.append(madd_past)
        in_specs.append(pl.BlockSpec((None, Q, page), lambda b, j, pt: (b, 0, 0)))
        args.append(madd_new)
        for t in (bias_last, bias_new):
            in_specs.append(pl.BlockSpec((KVH, R, page), lambda b, j, pt: (0, 0, 0)))
            args.append(t)
    grid_spec = pltpu.PrefetchScalarGridSpec(
        num_scalar_prefetch=1,
        grid=(Bn, nb),
        in_specs=in_specs,
        out_specs=pl.BlockSpec((None, KVH, R, hd), lambda b, j, pt: (b, 0, 0, 0)),
        scratch_shapes=scratch,
    )
    return pl.pallas_call(
        functools.partial(_paged_attn_kernel, kind=kind, P=P, KVH=KVH, G=G, Q=Q, page=page, hd=hd, scale=scale),
        grid_spec=grid_spec,
        out_shape=jax.ShapeDtypeStruct((Bn, KVH, R, hd), F32),
        compiler_params=_cparams("arbitrary", "arbitrary"),
        name=kind + "_attn_sample",
    )(page_table, *args)


def _sortable(s):
    bits = pltpu.bitcast(s + 0.0, I32)
    return jnp.where(bits >= 0, bits, bits ^ jnp.int32(0x7FFFFFFF))


def _kth_largest_key(count_ge, rows, k):
    sign = jnp.int32(-2 ** 31)

    def body(i, t):
        cand = t | jnp.left_shift(jnp.int32(1), 31 - i)
        cnt = count_ge(cand ^ sign)
        return jnp.where(cnt >= k, cand, t)

    t = lax.fori_loop(0, 32, body, jnp.zeros((rows, 1), I32))
    return t ^ sign


def _last_tie_index(ties_before, need, rows, n_keys):
    nbits = max(int(n_keys - 1).bit_length(), 1)

    def body(i, m):
        cand = m | jnp.left_shift(jnp.int32(1), nbits - 1 - i)
        return jnp.where(ties_before(cand) < need, cand, m)

    return lax.fori_loop(0, nbits, body, jnp.zeros((rows, 1), I32))


def _dsa_score_kernel(qi_ref, ki_ref, wi_ref, o_ref, acc_ref, wb_ref, *, HI, tq, T, q0, k_top, scale):
    wi = wi_ref[...]
    for h in range(HI):
        wb_ref[h] = jnp.broadcast_to(wi[:, h:h + 1], (tq, LANE))
    acc_ref[...] = jnp.zeros(acc_ref.shape, F32)
    ki = ki_ref[...]
    reps = T // LANE

    def body(h, carry):
        d = lax.dot_general(qi_ref[h], ki, _NT, preferred_element_type=F32)
        w = jnp.concatenate([wb_ref[h]] * reps, axis=1)
        acc_ref[...] += w * jnp.maximum(d, 0.0)
        return carry

    lax.fori_loop(0, HI, body, 0)
    qpos = lax.broadcasted_iota(I32, (tq, T), 0) + q0
    kpos = lax.broadcasted_iota(I32, (tq, T), 1)
    adm = kpos <= qpos
    key = _sortable(jnp.where(adm, acc_ref[...] * scale, NEG_INF))

    def count_ge(thr):
        return jnp.sum(jnp.where(key >= thr, 1, 0), axis=-1, keepdims=True)

    thr = _kth_largest_key(count_ge, tq, k_top)
    sel = jnp.logical_and(key >= thr, adm)
    o_ref[...] = jnp.where(sel, 0.0, NEG_INF)

    @pl.when(jnp.max(jnp.sum(jnp.where(sel, 1, 0), axis=-1, keepdims=True)) > k_top)
    def _():
        gt = key > thr
        tie = key == thr
        need = k_top - jnp.sum(jnp.where(gt, 1, 0), axis=-1, keepdims=True)

        def ties_before(m):
            return jnp.sum(jnp.where(jnp.logical_and(tie, kpos < m), 1, 0), axis=-1, keepdims=True)

        last = _last_tie_index(ties_before, need, tq, T)
        keep = jnp.logical_or(gt, jnp.logical_and(tie, kpos <= last))
        o_ref[...] = jnp.where(jnp.logical_and(keep, adm), 0.0, NEG_INF)


def _dsa_select_prompt(qi_hm, ki16, wi, Bn, T, c, tq, k_top, scale):
    HI, M, DI = qi_hm.shape
    nq = T // tq
    tk = (c + 1) * tq
    wl = wi.shape[1]
    return pl.pallas_call(
        functools.partial(_dsa_score_kernel, HI=HI, tq=tq, T=tk, q0=c * tq, k_top=k_top, scale=scale),
        grid=(Bn,),
        in_specs=[pl.BlockSpec((HI, tq, DI), lambda b: (0, b * nq + c, 0)),
                  pl.BlockSpec((None, tk, DI), lambda b: (b, 0, 0)),
                  pl.BlockSpec((tq, wl), lambda b: (b * nq + c, 0))],
        out_specs=pl.BlockSpec((None, tq, tk), lambda b: (b, 0, 0)),
        out_shape=jax.ShapeDtypeStruct((Bn, tq, tk), F32),
        scratch_shapes=[pltpu.VMEM((tq, tk), F32), pltpu.VMEM((HI, tq, LANE), F32)],
        compiler_params=_cparams("parallel"),
        name="dsa_select_prompt",
    )(qi_hm, ki16, wi)


def _dsa_attn_kernel(q_ref, k_ref, v_ref, madd_ref, td_ref, tl_ref, o_ref, s_ref, *, G, tq, iq0, scale):
    iq = iq0 + pl.program_id(1)
    hd = q_ref.shape[-1]
    k = k_ref[...]
    v = v_ref[...]
    madd = madd_ref[...]
    diag = pl.multiple_of(iq * tq, tq)
    left = pl.multiple_of(jnp.maximum(iq - 1, 0) * tq, tq)
    for g in range(G):
        s_ref[...] = lax.dot_general(q_ref[g], k, _NT, preferred_element_type=F32) * scale + madd
        s_ref[:, pl.ds(diag, tq)] += td_ref[g]

        @pl.when(iq > 0)
        def _():
            s_ref[:, pl.ds(left, tq)] += tl_ref[g]

        s = s_ref[...]
        m = jnp.max(s, axis=-1, keepdims=True)
        p = jnp.exp(s - m)
        l = jnp.sum(p, axis=-1, keepdims=True)
        o = jnp.dot(p.astype(BF16), v, preferred_element_type=F32) / l
        o_ref[:, g * hd:(g + 1) * hd] = o.astype(o_ref.dtype)


def _dsa_attn_prompt(q_hm, k16, v16, madd, t_diag, t_left, Bn, T, c, tqs, tq, G, scale):
    H, M, hd = q_hm.shape
    KVH = H // G
    tk = (c + 1) * tqs
    nsub = tqs // tq
    nq = T // tq
    return pl.pallas_call(
        functools.partial(_dsa_attn_kernel, G=G, tq=tq, iq0=c * nsub, scale=scale),
        grid=(Bn, nsub, KVH),
        in_specs=[pl.BlockSpec((G, tq, hd), lambda b, i, h: (h, b * nq + c * nsub + i, 0)),
                  pl.BlockSpec((None, tk, hd), lambda b, i, h: (b, 0, h)),
                  pl.BlockSpec((None, tk, hd), lambda b, i, h: (b, 0, h)),
                  pl.BlockSpec((None, tq, tk), lambda b, i, h: (b, i, 0)),
                  pl.BlockSpec((G, tq, tq), lambda b, i, h: (h, 0, 0)),
                  pl.BlockSpec((G, tq, tq), lambda b, i, h: (h, 0, 0))],
        out_specs=pl.BlockSpec((None, tq, G * hd), lambda b, i, h: (b, i, h)),
        out_shape=jax.ShapeDtypeStruct((Bn, tqs, H * hd), BF16),
        scratch_shapes=[pltpu.VMEM((tq, tk), F32)],
        compiler_params=_cparams("parallel", "parallel", "arbitrary"),
        name="dsa_attn_prompt",
    )(q_hm, k16, v16, madd, t_diag, t_left)


def _dsa_score_sample_kernel(*refs, P, HI, Q, page, k_top, scale):
    pt_ref = refs[0]
    del pt_ref
    qi_ref, wi_ref = refs[1], refs[2]
    kp_refs = refs[3:3 + P]
    kn_ref = refs[3 + P]
    mp_ref, mn_ref = refs[4 + P], refs[5 + P]
    kb_ref, sp_ref, sn_ref = refs[6 + P:9 + P]

    jb = pl.program_id(1)
    nb = pl.num_programs(1)
    n = P * page
    qi = qi_ref[...]
    wcol = wi_ref[...]

    def score(keys16):
        d = lax.dot_general(qi, keys16, _NT, preferred_element_type=F32)
        d = jnp.broadcast_to(wcol, d.shape) * jnp.maximum(d, 0.0)
        tot = d[0:Q]
        for h in range(1, HI):
            tot = tot + d[h * Q:(h + 1) * Q]
        return tot * scale

    for p in range(P):
        kb_ref[p * page:(p + 1) * page, :] = kp_refs[p][...].astype(BF16)
    sp_ref[jb] = score(kb_ref[...])

    @pl.when(jb == nb - 1)
    def _():
        qpos = lax.broadcasted_iota(I32, (Q, page), 0)
        kpos = lax.broadcasted_iota(I32, (Q, page), 1)
        adm = kpos <= qpos
        sn_ref[...] = jnp.where(adm, score(kn_ref[...].astype(BF16)), NEG_INF)
        key_p = _sortable(sp_ref[...])
        key_n = _sortable(sn_ref[...])

        def count_ge(thr):
            cp = jnp.sum(jnp.where(key_p >= thr[None], 1, 0), axis=-1, keepdims=True)
            cn = jnp.sum(jnp.where(key_n >= thr, 1, 0), axis=-1, keepdims=True)
            return jnp.sum(cp, axis=0) + cn

        thr = _kth_largest_key(count_ge, Q, k_top)
        sel_n = jnp.logical_and(key_n >= thr, adm)
        mp_ref[...] = jnp.where(key_p >= thr[None], 0.0, NEG_INF)
        mn_ref[...] = jnp.where(sel_n, 0.0, NEG_INF)

        def count(mask_p, mask_n):
            cp = jnp.sum(jnp.where(mask_p, 1, 0), axis=-1, keepdims=True)
            return jnp.sum(cp, axis=0) + jnp.sum(jnp.where(mask_n, 1, 0), axis=-1, keepdims=True)

        @pl.when(jnp.max(count(key_p >= thr[None], sel_n)) > k_top)
        def _():
            idx_p = (lax.broadcasted_iota(I32, key_p.shape, 0) * n + lax.broadcasted_iota(I32, key_p.shape, 2))
            n_past = key_p.shape[0] * n
            idx_n = kpos + n_past
            gt_p, gt_n = key_p > thr[None], key_n > thr
            tie_p, tie_n = key_p == thr[None], key_n == thr
            need = k_top - count(gt_p, gt_n)

            def ties_before(m):
                return count(jnp.logical_and(tie_p, idx_p < m[None]), jnp.logical_and(tie_n, idx_n < m))

            last = _last_tie_index(ties_before, need, Q, n_past + page)
            keep_p = jnp.logical_or(gt_p, jnp.logical_and(tie_p, idx_p <= last[None]))
            keep_n = jnp.logical_or(gt_n, jnp.logical_and(tie_n, idx_n <= last))
            mp_ref[...] = jnp.where(keep_p, 0.0, NEG_INF)
            mn_ref[...] = jnp.where(jnp.logical_and(keep_n, adm), 0.0, NEG_INF)


def _dsa_select_sample(qi, wi, pool_ki, layer, ki_new, page_table, Q, k_top, scale):
    Bn, RQ, DI = qi.shape
    page = pool_ki.shape[2]
    n_pages = page_table.shape[1]
    P = min(PAGES_PER_STEP, n_pages)
    nb = n_pages // P
    HI = RQ // Q
    in_specs = [pl.BlockSpec((None, RQ, DI), lambda b, j, pt: (b, 0, 0)),
                pl.BlockSpec((None, RQ, 1), lambda b, j, pt: (b, 0, 0))]
    args = [qi, wi]
    for p in range(P):
        in_specs.append(pl.BlockSpec((None, None, page, DI), functools.partial(
            lambda b, j, pt, p: (layer, pt[b, j * P + p], 0, 0), p=p)))
        args.append(pool_ki)
    in_specs.append(pl.BlockSpec((None, page, DI), lambda b, j, pt: (b, 0, 0)))
    args.append(ki_new)
    grid_spec = pltpu.PrefetchScalarGridSpec(
        num_scalar_prefetch=1,
        grid=(Bn, nb),
        in_specs=in_specs,
        out_specs=[pl.BlockSpec((None, nb, Q, P * page), lambda b, j, pt: (b, 0, 0, 0)),
                   pl.BlockSpec((None, Q, page), lambda b, j, pt: (b, 0, 0))],
        scratch_shapes=[pltpu.VMEM((P * page, DI), BF16), pltpu.VMEM((nb, Q, P * page), F32),
                        pltpu.VMEM((Q, page), F32)],
    )
    return pl.pallas_call(
        functools.partial(_dsa_score_sample_kernel, P=P, HI=HI, Q=Q, page=page, k_top=k_top, scale=scale),
        grid_spec=grid_spec,
        out_shape=[jax.ShapeDtypeStruct((Bn, nb, Q, P * page), F32), jax.ShapeDtypeStruct((Bn, Q, page), F32)],
        compiler_params=_cparams("arbitrary", "arbitrary"),
        name="dsa_select_sample",
    )(page_table, *args)


def _t5_bucket_np(dist):
    max_exact = N_BUCKETS // 2
    d = np.maximum(dist, 0)
    ratio = np.log(np.maximum(d, 1).astype(np.float32) / np.float32(max_exact)) / np.float32(
        math.log(MAX_DISTANCE / max_exact))
    large = np.minimum(max_exact + (ratio * (N_BUCKETS - max_exact)).astype(np.int32), N_BUCKETS - 1)
    return np.where(d < max_exact, d, large).astype(np.int32)


def _pad_cols(w, n):
    return jnp.pad(w, ((0, 0), (0, n - w.shape[1])))


def _to_rows(o, Bn, Q, KVH, G, hd):
    o = o.reshape(Bn, KVH, G, Q, hd)
    return jnp.transpose(o, (0, 3, 1, 2, 4)).reshape(Bn * Q, KVH * G * hd)


def _to_heads(q, Bn, Q, KVH, G, hd):
    q = q.reshape(Bn, Q, KVH, G, hd)
    return jnp.transpose(q, (0, 2, 3, 1, 4)).reshape(Bn, KVH, G * Q, hd)


def _pad_page(x, page):
    return jnp.pad(x, ((0, 0), (0, page - x.shape[1]), (0, 0)))


def _fox_layer(hp, hs, up, us, dims, layer, pool_k, pool_v, pool_lft, page_table, w_q, w_k, w_v, w_f, b_f, w_o):
    Bp, T, Bs, Q, H, KVH, hd = dims
    G = H // KVH
    scale = hd ** -0.5
    page = pool_k.shape[2]
    wf_pad = _pad_cols(w_f[layer], LANE)[None]
    bf_pad = jnp.pad(b_f[layer], (0, LANE - H))
    wft = w_f[layer].T.astype(BF16)
    tq, tk = min(256, T), min(512, T)

    q_hm, qs = _mmw(up, w_q, layer, xs=us, emit_f32=False, emit_bf16=True, head_major=True, name="fox_q")
    k32, k16, ks = _mmw(up, w_k, layer, xs=us, emit_bf16=True, name="fox_k")
    v32, v16, vs = _mmw(up, w_v, layer, xs=us, emit_bf16=True, name="fox_v")
    lf_pad, lfs_pad = _mmw(up, wf_pad, 0, xs=us, ls_bias=bf_pad, tn=LANE, name="fox_logf")
    lfs = lfs_pad[:, :H]

    cum = _fox_cum(up, wft, b_f[layer], Bp, T, tk)
    cum = jnp.transpose(cum.reshape(Bp, T // tk, KVH, G, 1, tk), (0, 2, 1, 3, 4, 5))
    o = _fox_attn_prompt(q_hm, k16, v16, cum, Bp, T, G, scale, tq, tk)

    q_t = _to_heads(qs, Bs, Q, KVH, G, hd).astype(BF16)
    lf_newt = jnp.transpose(_pad_page(lfs.reshape(Bs, Q, H), page), (0, 2, 1))
    o_s = _paged_attn("fox", q_t, pool_k, pool_v, layer,
                      _pad_page(ks.reshape(Bs, Q, KVH * hd), page), _pad_page(vs.reshape(Bs, Q, KVH * hd), page),
                      page_table, (pool_lft, lf_newt), scale)
    o_s = _to_rows(o_s, Bs, Q, KVH, G, hd).astype(BF16)

    hp, hs = _mmw(o, w_o, layer, xs=o_s, res=hp, res_s=hs, name="fox_o")
    rows = (k32.reshape(Bp, T, KVH, hd), v32.reshape(Bp, T, KVH, hd), lf_pad[:, :H].reshape(Bp, T, H),
            ks.reshape(Bs, Q, KVH, hd), vs.reshape(Bs, Q, KVH, hd), lfs.reshape(Bs, Q, H))
    return hp, hs, rows


def _sconv_layer(hp, hs, up, us, dims, layer, state, w_in, w_conv, w_out):
    Bp, T, Bs, Q = dims[:4]
    D = w_out.shape[1]
    zp, stp, zs, sts = _gconv(up, us, w_in, layer, (0, D, 2 * D), D, w_conv, _conv_prefix(state[layer], Q),
                              mode="sconv", seq_len=T, tail_len=Q, tm=512, name="sconv_in")
    hp, hs = _mmw(zp, w_out, layer, xs=zs, res=hp, res_s=hs, name="sconv_out")
    nblk = stp.shape[0] // Bp
    sp = stp.reshape(Bp, nblk, SUBLANE, D)[:, -1, SUBLANE - 2:, :]
    ss = sts.reshape(Bs, Q, D)[:, Q - 2:, :]
    return hp, hs, sp, ss


def _dsa_layer(hp, hs, up, us, dims, layer, pool_k, pool_v, pool_ki, page_table, rel_bias,
               w_q, w_k, w_v, w_o, w_qi, w_ki, w_wi):
    Bp, T, Bs, Q, H, KVH, hd = dims
    G = H // KVH
    scale = hd ** -0.5
    page = pool_k.shape[2]
    DI = w_ki.shape[2]
    HI = w_wi.shape[2]
    idx_scale = (DI * HI) ** -0.5
    past = page_table.shape[1] * page
    wkw = jnp.concatenate([w_ki[layer], _pad_cols(w_wi[layer], LANE)], axis=1)[None]
    c_far = rel_bias[N_BUCKETS - 1]

    q_hm, qs = _mmw(up, w_q, layer, xs=us, emit_f32=False, emit_bf16=True, head_major=True, name="dsa_q")
    k32, k16, ks = _mmw(up, w_k, layer, xs=us, emit_bf16=True, name="dsa_k")
    v32, v16, vs = _mmw(up, w_v, layer, xs=us, emit_bf16=True, name="dsa_v")
    qi_hm, qis = _mmw(up, w_qi, layer, xs=us, emit_f32=False, emit_bf16=True, head_major=True, name="dsa_qi")
    kw, kws = _mmw(up, wkw, 0, xs=us, tn=DI + LANE, name="dsa_kiwi")

    tq = min(128, T)
    tqs = min(512, T // 2)
    assert tq >= MAX_DISTANCE and tqs % tq == 0 and T % tqs == 0
    ki32 = kw[:, :DI]
    wi = kw[:, DI:]
    ki16 = ki32.astype(BF16).reshape(Bp, T, DI)
    k16 = k16.reshape(Bp, T, KVH * hd)
    v16 = v16.reshape(Bp, T, KVH * hd)
    ii = np.arange(tq)[:, None]
    jj = np.arange(tq)[None, :]
    t_diag = jnp.transpose(rel_bias[_t5_bucket_np(ii - jj)] - c_far, (2, 0, 1))
    t_left = jnp.transpose(rel_bias[_t5_bucket_np(tq + ii - jj)] - c_far, (2, 0, 1))
    k_top = min(TOPK_MAX, T // 4)
    o_blocks = []
    for c in range(T // tqs):
        madd = _dsa_select_prompt(qi_hm, ki16, wi, Bp, T, c, tqs, k_top, idx_scale)
        o_blocks.append(_dsa_attn_prompt(q_hm, k16, v16, madd, t_diag, t_left, Bp, T, c, tqs, tq, G, scale))
    o = jnp.concatenate(o_blocks, axis=1).reshape(Bp * T, H * hd)

    kis = kws[:, :DI]
    wis = kws[:, DI:DI + HI]
    qi_t = jnp.transpose(qis.reshape(Bs, Q, HI, DI), (0, 2, 1, 3)).reshape(Bs, HI * Q, DI).astype(BF16)
    wi_t = jnp.transpose(wis.reshape(Bs, Q, HI), (0, 2, 1)).reshape(Bs, HI * Q, 1)
    mp, mn = _dsa_select_sample(qi_t, wi_t, pool_ki, layer, _pad_page(kis.reshape(Bs, Q, DI), page), page_table,
                                Q, min(TOPK_MAX, (past + Q) // 4), idx_scale)
    nb, n = mp.shape[1], mp.shape[3]
    madd_past = jnp.transpose(mp, (0, 2, 1, 3)).reshape(Bs, Q, nb * n)
    qq = np.arange(Q)[:, None]
    cc = np.arange(page)[None, :]

    def sample_bias(dist):
        t = rel_bias[_t5_bucket_np(dist)] - c_far
        t = jnp.transpose(t.reshape(Q, page, KVH, G), (2, 3, 0, 1))
        return t.reshape(KVH, G * Q, page)

    bias_last = sample_bias(past + qq - (past - page + cc))
    bias_new = sample_bias(qq - cc)
    q_t = _to_heads(qs, Bs, Q, KVH, G, hd).astype(BF16)
    o_s = _paged_attn("dsa", q_t, pool_k, pool_v, layer,
                      _pad_page(ks.reshape(Bs, Q, KVH * hd), page), _pad_page(vs.reshape(Bs, Q, KVH * hd), page),
                      page_table, (madd_past, mn, bias_last, bias_new), scale)
    o_s = _to_rows(o_s, Bs, Q, KVH, G, hd).astype(BF16)

    hp, hs = _mmw(o, w_o, layer, xs=o_s, res=hp, res_s=hs, name="dsa_o")
    rows = (k32.reshape(Bp, T, KVH, hd), v32.reshape(Bp, T, KVH, hd), ki32.reshape(Bp, T, DI),
            ks.reshape(Bs, Q, KVH, hd), vs.reshape(Bs, Q, KVH, hd), kis.reshape(Bs, Q, DI))
    return hp, hs, rows


def _ffn_layer(hp, hs, vp, vs, dims, layer, state, w_up, w_conv, w_down16):
    Bp, T, Bs, Q = dims[:4]
    DFF = w_down16.shape[1]
    ap, stp, a_s, sts = _gconv(vp, vs, w_up, layer, (0, DFF), DFF, w_conv, _conv_prefix(state[layer], Q),
                               mode="ffn", seq_len=T, tail_len=Q, tm=1024, name="ffn_up")
    hp = _mm(ap, w_down16, layer, hp, name="ffn_down")
    hs = _mm(a_s, w_down16, layer, hs, name="ffn_down_s")
    nblk = stp.shape[0] // Bp
    cp = stp.reshape(Bp, nblk, SUBLANE, DFF)[:, -1, SUBLANE - 2:, :]
    cs = sts.reshape(Bs, Q, DFF)[:, Q - 2:, :]
    return hp, hs, cp, cs


def kernel(x_prompt, x_sample, cache_fox_k, cache_fox_v, cache_fox_logf, state_sconv, cache_dsa_k, cache_dsa_v,
           cache_dsa_kidx, state_ffn_conv, page_table, rel_bias, norm_mix, norm_ffn, norm_final,
           fox_w_q, fox_w_k, fox_w_v, fox_w_f, fox_b_f, fox_w_o, sc_w_in, sc_w_conv, sc_w_out,
           dsa_w_q, dsa_w_k, dsa_w_v, dsa_w_o, dsa_w_qi, dsa_w_ki, dsa_w_wi, ffn_w_up, ffn_w_conv, ffn_w_down):
    Bp, T, D = x_prompt.shape
    Bs, Q, _ = x_sample.shape
    depth = norm_mix.shape[0]
    KVH, hd = cache_fox_k.shape[3], cache_fox_k.shape[4]
    H = fox_w_f.shape[2]
    dims = (Bp, T, Bs, Q, H, KVH, hd)
    hp = x_prompt.reshape(Bp * T, D)
    hs = x_sample.reshape(Bs * Q, D)

    def flat_pool(c):
        return c.reshape(c.shape[0], c.shape[1], c.shape[2], KVH * hd)

    fox_k, fox_v, dsa_k, dsa_v = (flat_pool(c) for c in (cache_fox_k, cache_fox_v, cache_dsa_k, cache_dsa_v))
    fox_lft = jnp.transpose(cache_fox_logf, (0, 1, 3, 2))
    w_down16 = ffn_w_down.astype(BF16)
    fox_new = ([], [], [], [], [], [])
    sc_new = ([], [])
    dsa_new = ([], [], [], [], [], [])
    ffn_new = ([], [])
    for i in range(depth):
        j, kind = i // 3, i % 3
        up = _rmsnorm(hp, norm_mix[i], BF16)
        us = _rmsnorm(hs, norm_mix[i], BF16)
        if kind == 0:
            hp, hs, rows = _fox_layer(hp, hs, up, us, dims, j, fox_k, fox_v, fox_lft, page_table,
                                      fox_w_q, fox_w_k, fox_w_v, fox_w_f, fox_b_f, fox_w_o)
            for lst, a in zip(fox_new, rows):
                lst.append(a)
        elif kind == 1:
            hp, hs, sp, ss = _sconv_layer(hp, hs, up, us, dims, j, state_sconv, sc_w_in, sc_w_conv, sc_w_out)
            sc_new[0].append(sp)
            sc_new[1].append(ss)
        else:
            hp, hs, rows = _dsa_layer(hp, hs, up, us, dims, j, dsa_k, dsa_v, cache_dsa_kidx, page_table, rel_bias,
                                      dsa_w_q, dsa_w_k, dsa_w_v, dsa_w_o, dsa_w_qi, dsa_w_ki, dsa_w_wi)
            for lst, a in zip(dsa_new, rows):
                lst.append(a)
        vp = _rmsnorm(hp, norm_ffn[i], BF16)
        vs = _rmsnorm(hs, norm_ffn[i], BF16)
        hp, hs, cp, cs = _ffn_layer(hp, hs, vp, vs, dims, i, state_ffn_conv, ffn_w_up, ffn_w_conv, w_down16)
        ffn_new[0].append(cp)
        ffn_new[1].append(cs)
    y_prompt = _rmsnorm(hp, norm_final, F32).reshape(Bp, T, D)
    y_sample = _rmsnorm(hs, norm_final, F32).reshape(Bs, Q, D)
    return (y_prompt, y_sample,
            jnp.stack(fox_new[0]), jnp.stack(fox_new[1]), jnp.stack(fox_new[2]),
            jnp.stack(fox_new[3]), jnp.stack(fox_new[4]), jnp.stack(fox_new[5]),
            jnp.stack(sc_new[0]), jnp.stack(sc_new[1]),
            jnp.stack(dsa_new[0]), jnp.stack(dsa_new[1]), jnp.stack(dsa_new[2]),
            jnp.stack(dsa_new[3]), jnp.stack(dsa_new[4]), jnp.stack(dsa_new[5]),
            jnp.stack(ffn_new[0]), jnp.stack(ffn_new[1]))
```

```python
import functools
import math

import numpy as np
import jax
import jax.numpy as jnp
from jax import lax
from jax.experimental import pallas as pl
from jax.experimental.pallas import tpu as pltpu

F32 = jnp.float32
BF16 = jnp.bfloat16
I32 = jnp.int32

RMS_EPS = 1e-6
NEG_INF = -1e30
TOPK_MAX = 256
N_BUCKETS = 32
MAX_DISTANCE = 128
CONV_WIDTH = 3
LOG2E = 1.4426950408889634

LANE = 128
SUBLANE = 8
VMEM_LIMIT_BYTES = 56 * 1024 * 1024
PAGES_PER_STEP = 8

_NT = (((1,), (1,)), ((), ()))


def _cparams(*sem):
    return pltpu.CompilerParams(dimension_semantics=sem, vmem_limit_bytes=VMEM_LIMIT_BYTES)


def _log_sigmoid(x):
    return jnp.minimum(x, 0.0) - jnp.log1p(jnp.exp(-jnp.abs(x)))


def _split3(x):
    hi = x.astype(BF16)
    r1 = x - hi.astype(F32)
    mid = r1.astype(BF16)
    lo = (r1 - mid.astype(F32)).astype(BF16)
    return hi, mid, lo


def _tri_cumsum(x):
    n = x.shape[-1]
    r = lax.broadcasted_iota(I32, (n, n), 0)
    c = lax.broadcasted_iota(I32, (n, n), 1)
    tri = jnp.where(r <= c, 1.0, 0.0).astype(BF16)
    hi, mid, lo = _split3(x)
    out = jnp.dot(hi, tri, preferred_element_type=F32)
    out = out + jnp.dot(mid, tri, preferred_element_type=F32)
    out = out + jnp.dot(lo, tri, preferred_element_type=F32)
    return out


def _rms_kernel(x_ref, g_ref, o_ref):
    x = x_ref[...]
    ms = jnp.mean(x * x, axis=-1, keepdims=True)
    o_ref[...] = ((x * lax.rsqrt(ms + RMS_EPS)) * g_ref[...]).astype(o_ref.dtype)


def _rmsnorm(h, g, out_dtype):
    M, D = h.shape
    tm = min(512, M)
    return pl.pallas_call(
        _rms_kernel,
        grid=(M // tm,),
        in_specs=[pl.BlockSpec((tm, D), lambda i: (i, 0)), pl.BlockSpec((1, D), lambda i: (0, 0))],
        out_specs=pl.BlockSpec((tm, D), lambda i: (i, 0)),
        out_shape=jax.ShapeDtypeStruct((M, D), out_dtype),
        compiler_params=_cparams("parallel"),
        name="rmsnorm",
    )(h, g.reshape(1, D).astype(F32))


def _mm_kernel(x_ref, w_ref, r_ref, o_ref):
    o_ref[...] = r_ref[...] + jnp.dot(x_ref[...], w_ref[...], preferred_element_type=F32)


def _mm(x, w, layer, res, *, tm=512, tn=512, name="mm"):
    M, K = x.shape
    N = w.shape[2]
    tm = min(tm, M)
    tn = min(tn, N)
    assert M % tm == 0 and N % tn == 0, (M, N, tm, tn)
    return pl.pallas_call(
        _mm_kernel,
        grid=(M // tm, N // tn),
        in_specs=[pl.BlockSpec((tm, K), lambda i, j: (i, 0)),
                  pl.BlockSpec((None, K, tn), lambda i, j: (layer, 0, j)),
                  pl.BlockSpec((tm, tn), lambda i, j: (i, j))],
        out_specs=pl.BlockSpec((tm, tn), lambda i, j: (i, j)),
        out_shape=jax.ShapeDtypeStruct((M, N), F32),
        compiler_params=_cparams("parallel", "parallel"),
        name=name,
    )(x, w, res)


def _mmw_kernel(*refs, has_res, has_ls, emit_f32, emit_bf16, head_major, has_tail):
    it = iter(refs)
    x_ref = next(it)
    w_ref = next(it)
    b_ref = next(it) if has_ls else None
    r_ref = next(it) if has_res else None
    xs_ref = next(it) if has_tail else None
    rs_ref = next(it) if has_tail and has_res else None
    o32_ref = next(it) if emit_f32 else None
    o16_ref = next(it) if emit_bf16 else None
    os_ref = next(it) if has_tail else None
    wbf_ref = next(it)

    def project(x, r):
        acc = jnp.dot(x, wbf_ref[...], preferred_element_type=F32)
        if has_ls:
            acc = _log_sigmoid(acc + b_ref[...])
        if r is not None:
            acc = r[...] + acc
        return acc

    @pl.when(pl.program_id(1) == 0)
    def _():
        wbf_ref[...] = w_ref[...].astype(BF16)
        if has_tail:
            os_ref[...] = project(xs_ref[...], rs_ref)

    acc = project(x_ref[...], r_ref)
    if emit_f32:
        o32_ref[...] = acc
    if emit_bf16:
        if head_major:
            for hh in range(o16_ref.shape[0]):
                o16_ref[hh] = acc[:, hh * LANE:(hh + 1) * LANE].astype(BF16)
        else:
            o16_ref[...] = acc.astype(BF16)


def _mmw(x, w, layer, *, xs=None, res=None, res_s=None, ls_bias=None, emit_f32=True, emit_bf16=False,
         head_major=False, tm=1024, tn=512, name="mmw"):
    M, K = x.shape
    N = w.shape[2]
    tm = min(tm, M)
    tn = min(tn, N)
    assert M % tm == 0 and N % tn == 0, (M, N, tm, tn)
    has_tail = xs is not None
    in_specs = [pl.BlockSpec((tm, K), lambda j, i: (i, 0)), pl.BlockSpec((None, K, tn), lambda j, i: (layer, 0, j))]
    args = [x, w]
    if ls_bias is not None:
        in_specs.append(pl.BlockSpec((1, tn), lambda j, i: (0, j)))
        args.append(ls_bias.reshape(1, N).astype(F32))
    if res is not None:
        in_specs.append(pl.BlockSpec((tm, tn), lambda j, i: (i, j)))
        args.append(res)
    if has_tail:
        Ms = xs.shape[0]
        in_specs.append(pl.BlockSpec((Ms, K), lambda j, i: (0, 0)))
        args.append(xs)
        if res is not None:
            in_specs.append(pl.BlockSpec((Ms, tn), lambda j, i: (0, j)))
            args.append(res_s)
    out_specs, out_shape = [], []
    if emit_f32:
        out_specs.append(pl.BlockSpec((tm, tn), lambda j, i: (i, j)))
        out_shape.append(jax.ShapeDtypeStruct((M, N), F32))
    if emit_bf16:
        if head_major:
            out_specs.append(pl.BlockSpec((tn // LANE, tm, LANE), lambda j, i: (j, i, 0)))
            out_shape.append(jax.ShapeDtypeStruct((N // LANE, M, LANE), BF16))
        else:
            out_specs.append(pl.BlockSpec((tm, tn), lambda j, i: (i, j)))
            out_shape.append(jax.ShapeDtypeStruct((M, N), BF16))
    if has_tail:
        out_specs.append(pl.BlockSpec((Ms, tn), lambda j, i: (0, j)))
        out_shape.append(jax.ShapeDtypeStruct((Ms, N), F32))
    outs = pl.pallas_call(
        functools.partial(_mmw_kernel, has_res=res is not None, has_ls=ls_bias is not None, emit_f32=emit_f32,
                          emit_bf16=emit_bf16, head_major=head_major, has_tail=has_tail),
        grid=(N // tn, M // tm),
        in_specs=in_specs,
        out_specs=out_specs,
        out_shape=out_shape,
        scratch_shapes=[pltpu.VMEM((K, tn), BF16)],
        compiler_params=_cparams("arbitrary", "arbitrary"),
        name=name,
    )(*args)
    return outs[0] if len(outs) == 1 else tuple(outs)


def _gconv_kernel(x_ref, *refs, mode, nw, tm, seq_blocks, tail_len, row_chunks):
    w_refs = refs[:nw]
    wc_ref, xs_ref, p1_ref, p2_ref, o_ref, st_ref, os_ref, sts_ref, wbf_ref, buf_ref, carry_ref, bufs_ref = refs[nw:]
    i = pl.program_id(1)
    ms = xs_ref.shape[0]

    def gated(x):
        ys = [jnp.dot(x, wbf_ref[k], preferred_element_type=F32) for k in range(nw)]
        if mode == "ffn":
            return ys[0], ys[1]
        return ys[1] * ys[2], ys[0]

    def finish(cin, other, x1, x2):
        wc = wc_ref[...]
        y = wc[0:1, :] * x2 + wc[1:2, :] * x1 + wc[2:3, :] * cin
        if mode == "ffn":
            return ((y * (1.0 / (1.0 + jnp.exp(-y)))) * other).astype(BF16)
        return (other * y).astype(BF16)

    @pl.when(i == 0)
    def _():
        for k in range(nw):
            wbf_ref[k] = w_refs[k][...].astype(BF16)
        cin, other = gated(xs_ref[...])
        bufs_ref[0:SUBLANE, :] = jnp.zeros((SUBLANE, cin.shape[1]), F32)
        bufs_ref[SUBLANE:SUBLANE + ms, :] = cin
        rmod = lax.broadcasted_iota(I32, (ms, 1), 0) % tail_len
        x1 = jnp.where(rmod >= 1, bufs_ref[SUBLANE - 1:SUBLANE - 1 + ms, :], p1_ref[...])
        x2 = jnp.where(rmod >= 2, bufs_ref[SUBLANE - 2:SUBLANE - 2 + ms, :], p2_ref[...])
        sts_ref[...] = cin
        os_ref[...] = finish(cin, other, x1, x2)

    first = (i % seq_blocks) == 0

    @pl.when(first)
    def _():
        buf_ref[0:SUBLANE, :] = jnp.zeros((SUBLANE, buf_ref.shape[1]), F32)

    @pl.when(jnp.logical_not(first))
    def _():
        buf_ref[0:SUBLANE, :] = carry_ref[...]

    rc = tm // row_chunks
    for c in range(row_chunks):
        lo = c * rc
        cin, other = gated(x_ref[lo:lo + rc, :])
        buf_ref[SUBLANE + lo:SUBLANE + lo + rc, :] = cin
        o_ref[lo:lo + rc, :] = finish(cin, other, buf_ref[SUBLANE - 1 + lo:SUBLANE - 1 + lo + rc, :],
                                      buf_ref[SUBLANE - 2 + lo:SUBLANE - 2 + lo + rc, :])
    tail = buf_ref[tm:tm + SUBLANE, :]
    carry_ref[...] = tail
    st_ref[...] = tail


def _gconv(x, xs, w, layer, col_offsets, nc, wconv, prefix, *, mode, seq_len, tail_len, tm, tn=256, name="gconv"):
    M, K = x.shape
    Ms = xs.shape[0]
    tm = min(tm, seq_len)
    tn = min(tn, nc)
    assert M % tm == 0 and nc % tn == 0 and seq_len % tm == 0
    nw = len(col_offsets)
    in_specs = [pl.BlockSpec((tm, K), lambda j, i: (i, 0))]
    args = [x]
    for off in col_offsets:
        assert off % tn == 0
        in_specs.append(pl.BlockSpec((None, K, tn), functools.partial(lambda j, i, o: (layer, 0, o + j), o=off // tn)))
        args.append(w)
    in_specs.append(pl.BlockSpec((None, CONV_WIDTH, tn), lambda j, i: (layer, 0, j)))
    args.append(wconv)
    in_specs.append(pl.BlockSpec((Ms, K), lambda j, i: (0, 0)))
    args.append(xs)
    for p in prefix:
        in_specs.append(pl.BlockSpec((Ms, tn), lambda j, i: (0, j)))
        args.append(p)
    return pl.pallas_call(
        functools.partial(_gconv_kernel, mode=mode, nw=nw, tm=tm, seq_blocks=seq_len // tm, tail_len=tail_len,
                          row_chunks=max(tm // 512, 1)),
        grid=(nc // tn, M // tm),
        in_specs=in_specs,
        out_specs=[pl.BlockSpec((tm, tn), lambda j, i: (i, j)),
                   pl.BlockSpec((None, SUBLANE, tn), lambda j, i: (i, 0, j)),
                   pl.BlockSpec((Ms, tn), lambda j, i: (0, j)),
                   pl.BlockSpec((Ms, tn), lambda j, i: (0, j))],
        out_shape=[jax.ShapeDtypeStruct((M, nc), BF16), jax.ShapeDtypeStruct((M // tm, SUBLANE, nc), F32),
                   jax.ShapeDtypeStruct((Ms, nc), BF16), jax.ShapeDtypeStruct((Ms, nc), F32)],
        scratch_shapes=[pltpu.VMEM((nw, K, tn), BF16), pltpu.VMEM((SUBLANE + tm, tn), F32),
                        pltpu.VMEM((SUBLANE, tn), F32), pltpu.VMEM((SUBLANE + Ms, tn), F32)],
        compiler_params=_cparams("arbitrary", "arbitrary"),
        name=name,
    )(*args)


def _conv_prefix(state, seq_len):
    Bn, _, C = state.shape
    z = jnp.zeros((Bn, seq_len, C), F32)
    p1 = z.at[:, 0].set(state[:, 1])
    p2 = z.at[:, 0].set(state[:, 0]).at[:, 1].set(state[:, 1])
    return p1.reshape(Bn * seq_len, C), p2.reshape(Bn * seq_len, C)


def _cum_kernel(u_ref, wft_ref, bf_ref, o_ref, carry_ref):
    t = pl.program_id(1)

    @pl.when(t == 0)
    def _():
        carry_ref[...] = jnp.zeros_like(carry_ref)

    z = lax.dot_general(wft_ref[...], u_ref[...], _NT, preferred_element_type=F32) + bf_ref[...]
    c = _tri_cumsum(_log_sigmoid(z)) + carry_ref[...]
    o_ref[...] = c
    carry_ref[...] = c[:, c.shape[1] - 1:c.shape[1]]


def _fox_cum(u, wft, bf, Bn, T, tk):
    M, D = u.shape
    H = wft.shape[0]
    nt = T // tk
    return pl.pallas_call(
        _cum_kernel,
        grid=(Bn, nt),
        in_specs=[pl.BlockSpec((tk, D), lambda b, t: (b * nt + t, 0)),
                  pl.BlockSpec((H, D), lambda b, t: (0, 0)),
                  pl.BlockSpec((H, 1), lambda b, t: (0, 0))],
        out_specs=pl.BlockSpec((None, None, H, tk), lambda b, t: (b, t, 0, 0)),
        out_shape=jax.ShapeDtypeStruct((Bn, nt, H, tk), F32),
        scratch_shapes=[pltpu.VMEM((H, 1), F32)],
        compiler_params=_cparams("arbitrary", "arbitrary"),
        name="fox_cum",
    )(u, wft, bf.reshape(H, 1).astype(F32))


def _fox_attn_kernel(q_ref, k_ref, v_ref, cum_ref, o_ref, m_ref, l_ref, acc_ref, *, G, tq, tk, scale):
    qi = pl.program_id(2)
    hd = q_ref.shape[-1]
    reps = tk // LANE
    m_ref[...] = jnp.full(m_ref.shape, NEG_INF, F32)
    l_ref[...] = jnp.zeros(l_ref.shape, F32)
    acc_ref[...] = jnp.zeros(acc_ref.shape, F32)
    n_full = (qi * tq) // tk

    q = q_ref[...].reshape(G * tq, hd)

    def tile(j, masked):
        start = pl.multiple_of(j * tk, tk)
        kt = k_ref[pl.ds(start, tk), :]
        vt = v_ref[pl.ds(start, tk), :]
        s = lax.dot_general(q, kt, _NT, preferred_element_type=F32) * (scale * LOG2E)
        s = s.reshape(G, tq, tk) - cum_ref[j] * LOG2E
        if masked:
            row = lax.broadcasted_iota(I32, (tq, tk), 0) + qi * tq
            col = lax.broadcasted_iota(I32, (tq, tk), 1) + j * tk
            s = jnp.where((col <= row)[None], s, NEG_INF)
        s = s.reshape(G * tq, tk)
        m_prev = m_ref[...]
        m_new = jnp.maximum(m_prev, jnp.max(s, axis=1, keepdims=True))
        p = jnp.exp2(s - jnp.concatenate([m_new] * reps, axis=1))
        alpha = jnp.exp2(m_prev - m_new)
        l_ref[...] = alpha * l_ref[...] + jnp.sum(p, axis=1, keepdims=True)
        acc_ref[...] = alpha * acc_ref[...] + jnp.dot(p.astype(BF16), vt, preferred_element_type=F32)
        m_ref[...] = m_new

    def body(j, carry):
        tile(j, False)
        return carry

    lax.fori_loop(0, n_full, body, 0)
    tile(n_full, True)
    out = acc_ref[...] / l_ref[...]
    for g in range(G):
        o_ref[:, g * hd:(g + 1) * hd] = out[g * tq:(g + 1) * tq, :].astype(o_ref.dtype)


def _fox_attn_prompt(q_hm, k16, v16, cum, Bn, T, G, scale, tq, tk):
    H, M, hd = q_hm.shape
    assert hd == LANE and tk % tq == 0
    KVH = H // G
    nq = T // tq
    return pl.pallas_call(
        functools.partial(_fox_attn_kernel, G=G, tq=tq, tk=tk, scale=scale),
        grid=(Bn, KVH, nq),
        in_specs=[pl.BlockSpec((G, tq, hd), lambda b, h, i: (h, b * nq + i, 0)),
                  pl.BlockSpec((T, hd), lambda b, h, i: (b, h)),
                  pl.BlockSpec((T, hd), lambda b, h, i: (b, h)),
                  pl.BlockSpec((None, None, T // tk, G, 1, tk), lambda b, h, i: (b, h, 0, 0, 0, 0))],
        out_specs=pl.BlockSpec((tq, G * hd), lambda b, h, i: (b * nq + i, h)),
        out_shape=jax.ShapeDtypeStruct((M, H * hd), BF16),
        scratch_shapes=[pltpu.VMEM((G * tq, LANE), F32), pltpu.VMEM((G * tq, LANE), F32),
                        pltpu.VMEM((G * tq, hd), F32)],
        compiler_params=_cparams("parallel", "parallel", "arbitrary"),
        name="fox_attn_prompt",
    )(q_hm, k16, v16, cum)


def _paged_attn_kernel(*refs, kind, P, KVH, G, Q, page, hd, scale):
    pt_ref = refs[0]
    del pt_ref
    pos = 1
    q_ref = refs[pos]; pos += 1
    kp_refs = refs[pos:pos + P]; pos += P
    vp_refs = refs[pos:pos + P]; pos += P
    kn_ref, vn_ref = refs[pos], refs[pos + 1]; pos += 2
    if kind == "fox":
        lf_refs = refs[pos:pos + P]; pos += P
        lfn_ref = refs[pos]; pos += 1
    else:
        mp_ref, mn_ref, tl_ref, tn_ref = refs[pos:pos + 4]; pos += 4
    o_ref = refs[pos]; pos += 1
    kb_ref, vb_ref, m_ref, l_ref, acc_ref = refs[pos:pos + 5]; pos += 5
    if kind == "fox":
        cum_ref, carry_ref = refs[pos:pos + 2]

    jb = pl.program_id(1)
    nb = pl.num_programs(1)
    R = G * Q

    @pl.when(jb == 0)
    def _():
        m_ref[...] = jnp.full(m_ref.shape, NEG_INF, F32)
        l_ref[...] = jnp.zeros(l_ref.shape, F32)
        acc_ref[...] = jnp.zeros(acc_ref.shape, F32)
        if kind == "fox":
            carry_ref[...] = jnp.zeros(carry_ref.shape, F32)

    for p in range(P):
        for kvh in range(KVH):
            kb_ref[kvh, p * page:(p + 1) * page, :] = kp_refs[p][pl.ds(kvh, page, stride=KVH), :].astype(BF16)
            vb_ref[kvh, p * page:(p + 1) * page, :] = vp_refs[p][pl.ds(kvh, page, stride=KVH), :].astype(BF16)
        if kind == "fox":
            c = _tri_cumsum(lf_refs[p][...]) + carry_ref[...]
            cum_ref[:, p * page:(p + 1) * page] = c
            carry_ref[...] = c[:, page - 1:page]

    def update(kvh, s, vt):
        m_prev = m_ref[kvh]
        m_new = jnp.maximum(m_prev, jnp.max(s, axis=-1, keepdims=True))
        alpha = jnp.exp(m_prev - m_new)
        pr = jnp.exp(s - m_new)
        l_ref[kvh] = alpha * l_ref[kvh] + jnp.sum(pr, axis=-1, keepdims=True)
        acc_ref[kvh] = alpha * acc_ref[kvh] + jnp.dot(pr.astype(BF16), vt, preferred_element_type=F32)
        m_ref[kvh] = m_new

    def head_rows(tile, kvh):
        return jnp.concatenate(
            [jnp.broadcast_to(tile[kvh * G + g:kvh * G + g + 1, :], (Q, tile.shape[1])) for g in range(G)], axis=0)

    def query_rows(tile):
        return jnp.concatenate([tile] * G, axis=0)

    n = P * page
    if kind == "fox":
        cum = cum_ref[...]
    else:
        madd = mp_ref[...]
        is_last = jnp.where(jb == nb - 1, 1.0, 0.0)
    for kvh in range(KVH):
        q = q_ref[kvh]
        s = lax.dot_general(q, kb_ref[kvh], _NT, preferred_element_type=F32) * scale
        if kind == "fox":
            s = s - head_rows(cum, kvh)
        else:
            s = s + query_rows(madd)
            tail = s[:, n - page:] + is_last * tl_ref[kvh]
            s = tail if n == page else jnp.concatenate([s[:, :n - page], tail], axis=1)
        update(kvh, s, vb_ref[kvh])

    @pl.when(jb == nb - 1)
    def _():
        kn = kn_ref[...].astype(BF16)
        vn = vn_ref[...].astype(BF16)
        qpos = lax.broadcasted_iota(I32, (R, page), 0) % Q
        kpos = lax.broadcasted_iota(I32, (R, page), 1)
        causal = kpos <= qpos
        if kind == "fox":
            cn = _tri_cumsum(lfn_ref[...]) + carry_ref[...]
        else:
            mnew = mn_ref[...]
        for kvh in range(KVH):
            q = q_ref[kvh]
            s = lax.dot_general(q, kn[:, kvh * hd:(kvh + 1) * hd], _NT, preferred_element_type=F32) * scale
            if kind == "fox":
                s = s - head_rows(cn, kvh)
            else:
                s = s + query_rows(mnew) + tn_ref[kvh]
            s = jnp.where(causal, s, NEG_INF)
            update(kvh, s, vn[:, kvh * hd:(kvh + 1) * hd])
        for kvh in range(KVH):
            o_ref[kvh] = acc_ref[kvh] / l_ref[kvh]


def _paged_attn(kind, q, pool_k, pool_v, layer, k_new, v_new, page_table, extras, scale):
    Bn, KVH, R, hd = q.shape
    page = pool_k.shape[2] // KVH
    n_pages = page_table.shape[1]
    P = min(PAGES_PER_STEP, n_pages)
    assert n_pages % P == 0
    nb = n_pages // P
    H = None
    if kind == "fox":
        H = extras[0].shape[2]
        G = H // KVH
    else:
        G = extras[2].shape[1] // extras[0].shape[1]
    Q = R // G

    def page_map(p, nd):
        return lambda b, j, pt: (layer, pt[b, j * P + p]) + (0,) * nd

    in_specs = [pl.BlockSpec((None, KVH, R, hd), lambda b, j, pt: (b, 0, 0, 0))]
    args = [q]
    for pool in (pool_k, pool_v):
        for p in range(P):
            in_specs.append(pl.BlockSpec((None, None, page * KVH, hd), page_map(p, 2)))
            args.append(pool)
    for new in (k_new, v_new):
        in_specs.append(pl.BlockSpec((None, page, KVH * hd), lambda b, j, pt: (b, 0, 0)))
        args.append(new)
    scratch = [pltpu.VMEM((KVH, P * page, hd), BF16), pltpu.VMEM((KVH, P * page, hd), BF16),
               pltpu.VMEM((KVH, R, 1), F32), pltpu.VMEM((KVH, R, 1), F32), pltpu.VMEM((KVH, R, hd), F32)]
    if kind == "fox":
        pool_lft, lf_newt = extras
        for p in range(P):
            in_specs.append(pl.BlockSpec((None, None, H, page), page_map(p, 2)))
            args.append(pool_lft)
        in_specs.append(pl.BlockSpec((None, H, page), lambda b, j, pt: (b, 0, 0)))
        args.append(lf_newt)
        scratch += [pltpu.VMEM((H, P * page), F32), pltpu.VMEM((H, 1), F32)]
    else:
        madd_past, madd_new, bias_last, bias_new = extras
        in_specs.append(pl.BlockSpec((None, Q, P * page), lambda b, j, pt: (b, 0, j)))
        args.append(madd_past)
        in_specs.append(pl.BlockSpec((None, Q, page), lambda b, j, pt: (b, 0, 0)))
        args.append(madd_new)
        for t in (bias_last, bias_new):
            in_specs.append(pl.BlockSpec((KVH, R, page), lambda b, j, pt: (0, 0, 0)))
            args.append(t)
    grid_spec = pltpu.PrefetchScalarGridSpec(
        num_scalar_prefetch=1,
        grid=(Bn, nb),
        in_specs=in_specs,
        out_specs=pl.BlockSpec((None, KVH, R, hd), lambda b, j, pt: (b, 0, 0, 0)),
        scratch_shapes=scratch,
    )
    return pl.pallas_call(
        functools.partial(_paged_attn_kernel, kind=kind, P=P, KVH=KVH, G=G, Q=Q, page=page, hd=hd, scale=scale),
        grid_spec=grid_spec,
        out_shape=jax.ShapeDtypeStruct((Bn, KVH, R, hd), F32),
        compiler_params=_cparams("arbitrary", "arbitrary"),
        name=kind + "_attn_sample",
    )(page_table, *args)


def _sortable(s):
    bits = pltpu.bitcast(s + 0.0, I32)
    return jnp.where(bits >= 0, bits, bits ^ jnp.int32(0x7FFFFFFF))


def _kth_largest_key(count_ge, rows, k):
    sign = jnp.int32(-2 ** 31)

    def body(i, t):
        cand = t | jnp.left_shift(jnp.int32(1), 31 - i)
        cnt = count_ge(cand ^ sign)
        return jnp.where(cnt >= k, cand, t)

    t = lax.fori_loop(0, 32, body, jnp.zeros((rows, 1), I32))
    return t ^ sign


def _last_tie_index(ties_before, need, rows, n_keys):
    nbits = max(int(n_keys - 1).bit_length(), 1)

    def body(i, m):
        cand = m | jnp.left_shift(jnp.int32(1), nbits - 1 - i)
        return jnp.where(ties_before(cand) < need, cand, m)

    return lax.fori_loop(0, nbits, body, jnp.zeros((rows, 1), I32))


def _dsa_score_kernel(qi_ref, ki_ref, wi_ref, o_ref, acc_ref, wb_ref, *, HI, tq, T, q0, k_top, scale):
    wi = wi_ref[...]
    for h in range(HI):
        wb_ref[h] = jnp.broadcast_to(wi[:, h:h + 1], (tq, LANE))
    acc_ref[...] = jnp.zeros(acc_ref.shape, F32)
    ki = ki_ref[...]
    reps = T // LANE

    def body(h, carry):
        d = lax.dot_general(qi_ref[h], ki, _NT, preferred_element_type=F32)
        w = jnp.concatenate([wb_ref[h]] * reps, axis=1)
        acc_ref[...] += w * jnp.maximum(d, 0.0)
        return carry

    lax.fori_loop(0, HI, body, 0)
    qpos = lax.broadcasted_iota(I32, (tq, T), 0) + q0
    kpos = lax.broadcasted_iota(I32, (tq, T), 1)
    adm = kpos <= qpos
    key = _sortable(jnp.where(adm, acc_ref[...] * scale, NEG_INF))

    def count_ge(thr):
        return jnp.sum(jnp.where(key >= thr, 1, 0), axis=-1, keepdims=True)

    thr = _kth_largest_key(count_ge, tq, k_top)
    sel = jnp.logical_and(key >= thr, adm)
    o_ref[...] = jnp.where(sel, 0.0, NEG_INF)

    @pl.when(jnp.max(jnp.sum(jnp.where(sel, 1, 0), axis=-1, keepdims=True)) > k_top)
    def _():
        gt = key > thr
        tie = key == thr
        need = k_top - jnp.sum(jnp.where(gt, 1, 0), axis=-1, keepdims=True)

        def ties_before(m):
            return jnp.sum(jnp.where(jnp.logical_and(tie, kpos < m), 1, 0), axis=-1, keepdims=True)

        last = _last_tie_index(ties_before, need, tq, T)
        keep = jnp.logical_or(gt, jnp.logical_and(tie, kpos <= last))
        o_ref[...] = jnp.where(jnp.logical_and(keep, adm), 0.0, NEG_INF)


def _dsa_select_prompt(qi_hm, ki16, wi, Bn, T, c, tq, k_top, scale):
    HI, M, DI = qi_hm.shape
    nq = T // tq
    tk = (c + 1) * tq
    wl = wi.shape[1]
    return pl.pallas_call(
        functools.partial(_dsa_score_kernel, HI=HI, tq=tq, T=tk, q0=c * tq, k_top=k_top, scale=scale),
        grid=(Bn,),
        in_specs=[pl.BlockSpec((HI, tq, DI), lambda b: (0, b * nq + c, 0)),
                  pl.BlockSpec((None, tk, DI), lambda b: (b, 0, 0)),
                  pl.BlockSpec((tq, wl), lambda b: (b * nq + c, 0))],
        out_specs=pl.BlockSpec((None, tq, tk), lambda b: (b, 0, 0)),
        out_shape=jax.ShapeDtypeStruct((Bn, tq, tk), F32),
        scratch_shapes=[pltpu.VMEM((tq, tk), F32), pltpu.VMEM((HI, tq, LANE), F32)],
        compiler_params=_cparams("parallel"),
        name="dsa_select_prompt",
    )(qi_hm, ki16, wi)


def _dsa_attn_kernel(q_ref, k_ref, v_ref, madd_ref, td_ref, tl_ref, o_ref, s_ref, *, G, tq, iq0, scale):
    iq = iq0 + pl.program_id(1)
    hd = q_ref.shape[-1]
    k = k_ref[...]
    v = v_ref[...]
    madd = madd_ref[...]
    diag = pl.multiple_of(iq * tq, tq)
    left = pl.multiple_of(jnp.maximum(iq - 1, 0) * tq, tq)
    for g in range(G):
        s_ref[g] = lax.dot_general(q_ref[g], k, _NT, preferred_element_type=F32) * (scale * LOG2E) + madd
        s_ref[g, :, pl.ds(diag, tq)] += td_ref[g] * LOG2E

        @pl.when(iq > 0)
        def _():
            s_ref[g, :, pl.ds(left, tq)] += tl_ref[g] * LOG2E

        s = s_ref[g]
        m = jnp.max(s, axis=-1, keepdims=True)
        p = jnp.exp2(s - m)
        l = jnp.sum(p, axis=-1, keepdims=True)
        o = jnp.dot(p.astype(BF16), v, preferred_element_type=F32) / l
        o_ref[:, g * hd:(g + 1) * hd] = o.astype(o_ref.dtype)


def _dsa_attn_prompt(q_hm, k16, v16, madd, t_diag, t_left, Bn, T, c, tqs, tq, G, scale):
    H, M, hd = q_hm.shape
    KVH = H // G
    tk = (c + 1) * tqs
    nsub = tqs // tq
    nq = T // tq
    return pl.pallas_call(
        functools.partial(_dsa_attn_kernel, G=G, tq=tq, iq0=c * nsub, scale=scale),
        grid=(Bn, nsub, KVH),
        in_specs=[pl.BlockSpec((G, tq, hd), lambda b, i, h: (h, b * nq + c * nsub + i, 0)),
                  pl.BlockSpec((None, tk, hd), lambda b, i, h: (b, 0, h)),
                  pl.BlockSpec((None, tk, hd), lambda b, i, h: (b, 0, h)),
                  pl.BlockSpec((None, tq, tk), lambda b, i, h: (b, i, 0)),
                  pl.BlockSpec((G, tq, tq), lambda b, i, h: (h, 0, 0)),
                  pl.BlockSpec((G, tq, tq), lambda b, i, h: (h, 0, 0))],
        out_specs=pl.BlockSpec((None, tq, G * hd), lambda b, i, h: (b, i, h)),
        out_shape=jax.ShapeDtypeStruct((Bn, tqs, H * hd), BF16),
        scratch_shapes=[pltpu.VMEM((G, tq, tk), F32)],
        compiler_params=_cparams("parallel", "parallel", "arbitrary"),
        name="dsa_attn_prompt",
    )(q_hm, k16, v16, madd, t_diag, t_left)


def _dsa_score_sample_kernel(*refs, P, HI, Q, page, k_top, scale):
    pt_ref = refs[0]
    del pt_ref
    qi_ref, wi_ref = refs[1], refs[2]
    kp_refs = refs[3:3 + P]
    kn_ref = refs[3 + P]
    mp_ref, mn_ref = refs[4 + P], refs[5 + P]
    kb_ref, sp_ref, sn_ref = refs[6 + P:9 + P]

    jb = pl.program_id(1)
    nb = pl.num_programs(1)
    n = P * page
    qi = qi_ref[...]
    wcol = wi_ref[...]

    def score(keys16):
        d = lax.dot_general(qi, keys16, _NT, preferred_element_type=F32)
        d = jnp.broadcast_to(wcol, d.shape) * jnp.maximum(d, 0.0)
        tot = d[0:Q]
        for h in range(1, HI):
            tot = tot + d[h * Q:(h + 1) * Q]
        return tot * scale

    for p in range(P):
        kb_ref[p * page:(p + 1) * page, :] = kp_refs[p][...].astype(BF16)
    sp_ref[jb] = score(kb_ref[...])

    @pl.when(jb == nb - 1)
    def _():
        qpos = lax.broadcasted_iota(I32, (Q, page), 0)
        kpos = lax.broadcasted_iota(I32, (Q, page), 1)
        adm = kpos <= qpos
        sn_ref[...] = jnp.where(adm, score(kn_ref[...].astype(BF16)), NEG_INF)
        key_p = _sortable(sp_ref[...])
        key_n = _sortable(sn_ref[...])

        def count_ge(thr):
            cp = jnp.sum(jnp.where(key_p >= thr[None], 1, 0), axis=-1, keepdims=True)
            cn = jnp.sum(jnp.where(key_n >= thr, 1, 0), axis=-1, keepdims=True)
            return jnp.sum(cp, axis=0) + cn

        thr = _kth_largest_key(count_ge, Q, k_top)
        sel_n = jnp.logical_and(key_n >= thr, adm)
        mp_ref[...] = jnp.where(key_p >= thr[None], 0.0, NEG_INF)
        mn_ref[...] = jnp.where(sel_n, 0.0, NEG_INF)

        def count(mask_p, mask_n):
            cp = jnp.sum(jnp.where(mask_p, 1, 0), axis=-1, keepdims=True)
            return jnp.sum(cp, axis=0) + jnp.sum(jnp.where(mask_n, 1, 0), axis=-1, keepdims=True)

        @pl.when(jnp.max(count(key_p >= thr[None], sel_n)) > k_top)
        def _():
            idx_p = (lax.broadcasted_iota(I32, key_p.shape, 0) * n + lax.broadcasted_iota(I32, key_p.shape, 2))
            n_past = key_p.shape[0] * n
            idx_n = kpos + n_past
            gt_p, gt_n = key_p > thr[None], key_n > thr
            tie_p, tie_n = key_p == thr[None], key_n == thr
            need = k_top - count(gt_p, gt_n)

            def ties_before(m):
                return count(jnp.logical_and(tie_p, idx_p < m[None]), jnp.logical_and(tie_n, idx_n < m))

            last = _last_tie_index(ties_before, need, Q, n_past + page)
            keep_p = jnp.logical_or(gt_p, jnp.logical_and(tie_p, idx_p <= last[None]))
            keep_n = jnp.logical_or(gt_n, jnp.logical_and(tie_n, idx_n <= last))
            mp_ref[...] = jnp.where(keep_p, 0.0, NEG_INF)
            mn_ref[...] = jnp.where(jnp.logical_and(keep_n, adm), 0.0, NEG_INF)


def _dsa_select_sample(qi, wi, pool_ki, layer, ki_new, page_table, Q, k_top, scale):
    Bn, RQ, DI = qi.shape
    page = pool_ki.shape[2]
    n_pages = page_table.shape[1]
    P = min(PAGES_PER_STEP, n_pages)
    nb = n_pages // P
    HI = RQ // Q
    in_specs = [pl.BlockSpec((None, RQ, DI), lambda b, j, pt: (b, 0, 0)),
                pl.BlockSpec((None, RQ, 1), lambda b, j, pt: (b, 0, 0))]
    args = [qi, wi]
    for p in range(P):
        in_specs.append(pl.BlockSpec((None, None, page, DI), functools.partial(
            lambda b, j, pt, p: (layer, pt[b, j * P + p], 0, 0), p=p)))
        args.append(pool_ki)
    in_specs.append(pl.BlockSpec((None, page, DI), lambda b, j, pt: (b, 0, 0)))
    args.append(ki_new)
    grid_spec = pltpu.PrefetchScalarGridSpec(
        num_scalar_prefetch=1,
        grid=(Bn, nb),
        in_specs=in_specs,
        out_specs=[pl.BlockSpec((None, nb, Q, P * page), lambda b, j, pt: (b, 0, 0, 0)),
                   pl.BlockSpec((None, Q, page), lambda b, j, pt: (b, 0, 0))],
        scratch_shapes=[pltpu.VMEM((P * page, DI), BF16), pltpu.VMEM((nb, Q, P * page), F32),
                        pltpu.VMEM((Q, page), F32)],
    )
    return pl.pallas_call(
        functools.partial(_dsa_score_sample_kernel, P=P, HI=HI, Q=Q, page=page, k_top=k_top, scale=scale),
        grid_spec=grid_spec,
        out_shape=[jax.ShapeDtypeStruct((Bn, nb, Q, P * page), F32), jax.ShapeDtypeStruct((Bn, Q, page), F32)],
        compiler_params=_cparams("arbitrary", "arbitrary"),
        name="dsa_select_sample",
    )(page_table, *args)


def _t5_bucket_np(dist):
    max_exact = N_BUCKETS // 2
    d = np.maximum(dist, 0)
    ratio = np.log(np.maximum(d, 1).astype(np.float32) / np.float32(max_exact)) / np.float32(
        math.log(MAX_DISTANCE / max_exact))
    large = np.minimum(max_exact + (ratio * (N_BUCKETS - max_exact)).astype(np.int32), N_BUCKETS - 1)
    return np.where(d < max_exact, d, large).astype(np.int32)


def _pad_cols(w, n):
    return jnp.pad(w, ((0, 0), (0, n - w.shape[1])))


def _to_rows(o, Bn, Q, KVH, G, hd):
    o = o.reshape(Bn, KVH, G, Q, hd)
    return jnp.transpose(o, (0, 3, 1, 2, 4)).reshape(Bn * Q, KVH * G * hd)


def _to_heads(q, Bn, Q, KVH, G, hd):
    q = q.reshape(Bn, Q, KVH, G, hd)
    return jnp.transpose(q, (0, 2, 3, 1, 4)).reshape(Bn, KVH, G * Q, hd)


def _pad_page(x, page):
    return jnp.pad(x, ((0, 0), (0, page - x.shape[1]), (0, 0)))


def _fox_layer(hp, hs, up, us, dims, layer, pool_k, pool_v, pool_lft, page_table, w_q, w_k, w_v, w_f, b_f, w_o):
    Bp, T, Bs, Q, H, KVH, hd = dims
    G = H // KVH
    scale = hd ** -0.5
    page = pool_k.shape[2] // KVH
    wf_pad = _pad_cols(w_f[layer], LANE)[None]
    bf_pad = jnp.pad(b_f[layer], (0, LANE - H))
    wft = w_f[layer].T.astype(BF16)
    tq, tk = min(256, T), min(512, T)

    q_hm, qs = _mmw(up, w_q, layer, xs=us, emit_f32=False, emit_bf16=True, head_major=True, name="fox_q")
    k32, k16, ks = _mmw(up, w_k, layer, xs=us, emit_bf16=True, name="fox_k")
    v32, v16, vs = _mmw(up, w_v, layer, xs=us, emit_bf16=True, name="fox_v")
    lf_pad, lfs_pad = _mmw(up, wf_pad, 0, xs=us, ls_bias=bf_pad, tn=LANE, name="fox_logf")
    lfs = lfs_pad[:, :H]

    cum = _fox_cum(up, wft, b_f[layer], Bp, T, tk)
    cum = jnp.transpose(cum.reshape(Bp, T // tk, KVH, G, 1, tk), (0, 2, 1, 3, 4, 5))
    o = _fox_attn_prompt(q_hm, k16, v16, cum, Bp, T, G, scale, tq, tk)

    q_t = _to_heads(qs, Bs, Q, KVH, G, hd).astype(BF16)
    lf_newt = jnp.transpose(_pad_page(lfs.reshape(Bs, Q, H), page), (0, 2, 1))
    o_s = _paged_attn("fox", q_t, pool_k, pool_v, layer,
                      _pad_page(ks.reshape(Bs, Q, KVH * hd), page), _pad_page(vs.reshape(Bs, Q, KVH * hd), page),
                      page_table, (pool_lft, lf_newt), scale)
    o_s = _to_rows(o_s, Bs, Q, KVH, G, hd).astype(BF16)

    hp, hs = _mmw(o, w_o, layer, xs=o_s, res=hp, res_s=hs, name="fox_o")
    rows = (k32.reshape(Bp, T, KVH, hd), v32.reshape(Bp, T, KVH, hd), lf_pad[:, :H].reshape(Bp, T, H),
            ks.reshape(Bs, Q, KVH, hd), vs.reshape(Bs, Q, KVH, hd), lfs.reshape(Bs, Q, H))
    return hp, hs, rows


def _sconv_layer(hp, hs, up, us, dims, layer, state, w_in, w_conv, w_out):
    Bp, T, Bs, Q = dims[:4]
    D = w_out.shape[1]
    zp, stp, zs, sts = _gconv(up, us, w_in, layer, (0, D, 2 * D), D, w_conv, _conv_prefix(state[layer], Q),
                              mode="sconv", seq_len=T, tail_len=Q, tm=512, name="sconv_in")
    hp, hs = _mmw(zp, w_out, layer, xs=zs, res=hp, res_s=hs, name="sconv_out")
    nblk = stp.shape[0] // Bp
    sp = stp.reshape(Bp, nblk, SUBLANE, D)[:, -1, SUBLANE - 2:, :]
    ss = sts.reshape(Bs, Q, D)[:, Q - 2:, :]
    return hp, hs, sp, ss


def _dsa_layer(hp, hs, up, us, dims, layer, pool_k, pool_v, pool_ki, page_table, rel_bias,
               w_q, w_k, w_v, w_o, w_qi, w_ki, w_wi):
    Bp, T, Bs, Q, H, KVH, hd = dims
    G = H // KVH
    scale = hd ** -0.5
    page = pool_k.shape[2] // KVH
    DI = w_ki.shape[2]
    HI = w_wi.shape[2]
    idx_scale = (DI * HI) ** -0.5
    past = page_table.shape[1] * page
    wkw = jnp.concatenate([w_ki[layer], _pad_cols(w_wi[layer], LANE)], axis=1)[None]
    c_far = rel_bias[N_BUCKETS - 1]

    q_hm, qs = _mmw(up, w_q, layer, xs=us, emit_f32=False, emit_bf16=True, head_major=True, name="dsa_q")
    k32, k16, ks = _mmw(up, w_k, layer, xs=us, emit_bf16=True, name="dsa_k")
    v32, v16, vs = _mmw(up, w_v, layer, xs=us, emit_bf16=True, name="dsa_v")
    qi_hm, qis = _mmw(up, w_qi, layer, xs=us, emit_f32=False, emit_bf16=True, head_major=True, name="dsa_qi")
    kw, kws = _mmw(up, wkw, 0, xs=us, tn=DI + LANE, name="dsa_kiwi")

    tqs = min(512, T // 2)
    tq = min(256, tqs)
    assert tq >= MAX_DISTANCE and tqs % tq == 0 and T % tqs == 0
    ki32 = kw[:, :DI]
    wi = kw[:, DI:]
    ki16 = ki32.astype(BF16).reshape(Bp, T, DI)
    k16 = k16.reshape(Bp, T, KVH * hd)
    v16 = v16.reshape(Bp, T, KVH * hd)
    ii = np.arange(tq)[:, None]
    jj = np.arange(tq)[None, :]
    t_diag = jnp.transpose(rel_bias[_t5_bucket_np(ii - jj)] - c_far, (2, 0, 1))
    t_left = jnp.transpose(rel_bias[_t5_bucket_np(tq + ii - jj)] - c_far, (2, 0, 1))
    k_top = min(TOPK_MAX, T // 4)
    o_blocks = []
    for c in range(T // tqs):
        madd = _dsa_select_prompt(qi_hm, ki16, wi, Bp, T, c, tqs, k_top, idx_scale)
        o_blocks.append(_dsa_attn_prompt(q_hm, k16, v16, madd, t_diag, t_left, Bp, T, c, tqs, tq, G, scale))
    o = jnp.concatenate(o_blocks, axis=1).reshape(Bp * T, H * hd)

    kis = kws[:, :DI]
    wis = kws[:, DI:DI + HI]
    qi_t = jnp.transpose(qis.reshape(Bs, Q, HI, DI), (0, 2, 1, 3)).reshape(Bs, HI * Q, DI).astype(BF16)
    wi_t = jnp.transpose(wis.reshape(Bs, Q, HI), (0, 2, 1)).reshape(Bs, HI * Q, 1)
    mp, mn = _dsa_select_sample(qi_t, wi_t, pool_ki, layer, _pad_page(kis.reshape(Bs, Q, DI), page), page_table,
                                Q, min(TOPK_MAX, (past + Q) // 4), idx_scale)
    nb, n = mp.shape[1], mp.shape[3]
    madd_past = jnp.transpose(mp, (0, 2, 1, 3)).reshape(Bs, Q, nb * n)
    qq = np.arange(Q)[:, None]
    cc = np.arange(page)[None, :]

    def sample_bias(dist):
        t = rel_bias[_t5_bucket_np(dist)] - c_far
        t = jnp.transpose(t.reshape(Q, page, KVH, G), (2, 3, 0, 1))
        return t.reshape(KVH, G * Q, page)

    bias_last = sample_bias(past + qq - (past - page + cc))
    bias_new = sample_bias(qq - cc)
    q_t = _to_heads(qs, Bs, Q, KVH, G, hd).astype(BF16)
    o_s = _paged_attn("dsa", q_t, pool_k, pool_v, layer,
                      _pad_page(ks.reshape(Bs, Q, KVH * hd), page), _pad_page(vs.reshape(Bs, Q, KVH * hd), page),
                      page_table, (madd_past, mn, bias_last, bias_new), scale)
    o_s = _to_rows(o_s, Bs, Q, KVH, G, hd).astype(BF16)

    hp, hs = _mmw(o, w_o, layer, xs=o_s, res=hp, res_s=hs, name="dsa_o")
    rows = (k32.reshape(Bp, T, KVH, hd), v32.reshape(Bp, T, KVH, hd), ki32.reshape(Bp, T, DI),
            ks.reshape(Bs, Q, KVH, hd), vs.reshape(Bs, Q, KVH, hd), kis.reshape(Bs, Q, DI))
    return hp, hs, rows


def _ffn_layer(hp, hs, vp, vs, dims, layer, state, w_up, w_conv, w_down16):
    Bp, T, Bs, Q = dims[:4]
    DFF = w_down16.shape[1]
    ap, stp, a_s, sts = _gconv(vp, vs, w_up, layer, (0, DFF), DFF, w_conv, _conv_prefix(state[layer], Q),
                               mode="ffn", seq_len=T, tail_len=Q, tm=1024, name="ffn_up")
    hp = _mm(ap, w_down16, layer, hp, name="ffn_down")
    hs = _mm(a_s, w_down16, layer, hs, name="ffn_down_s")
    nblk = stp.shape[0] // Bp
    cp = stp.reshape(Bp, nblk, SUBLANE, DFF)[:, -1, SUBLANE - 2:, :]
    cs = sts.reshape(Bs, Q, DFF)[:, Q - 2:, :]
    return hp, hs, cp, cs


def kernel(x_prompt, x_sample, cache_fox_k, cache_fox_v, cache_fox_logf, state_sconv, cache_dsa_k, cache_dsa_v,
           cache_dsa_kidx, state_ffn_conv, page_table, rel_bias, norm_mix, norm_ffn, norm_final,
           fox_w_q, fox_w_k, fox_w_v, fox_w_f, fox_b_f, fox_w_o, sc_w_in, sc_w_conv, sc_w_out,
           dsa_w_q, dsa_w_k, dsa_w_v, dsa_w_o, dsa_w_qi, dsa_w_ki, dsa_w_wi, ffn_w_up, ffn_w_conv, ffn_w_down):
    Bp, T, D = x_prompt.shape
    Bs, Q, _ = x_sample.shape
    depth = norm_mix.shape[0]
    KVH, hd = cache_fox_k.shape[3], cache_fox_k.shape[4]
    H = fox_w_f.shape[2]
    dims = (Bp, T, Bs, Q, H, KVH, hd)
    hp = x_prompt.reshape(Bp * T, D)
    hs = x_sample.reshape(Bs * Q, D)

    def key_head_rows(c):
        return c.reshape(c.shape[0], c.shape[1], c.shape[2] * KVH, hd)

    fox_k, fox_v, dsa_k, dsa_v = (key_head_rows(c) for c in (cache_fox_k, cache_fox_v, cache_dsa_k, cache_dsa_v))
    fox_lft = jnp.transpose(cache_fox_logf, (0, 1, 3, 2))
    w_down16 = ffn_w_down.astype(BF16)
    fox_new = ([], [], [], [], [], [])
    sc_new = ([], [])
    dsa_new = ([], [], [], [], [], [])
    ffn_new = ([], [])
    for i in range(depth):
        j, kind = i // 3, i % 3
        up = _rmsnorm(hp, norm_mix[i], BF16)
        us = _rmsnorm(hs, norm_mix[i], BF16)
        if kind == 0:
            hp, hs, rows = _fox_layer(hp, hs, up, us, dims, j, fox_k, fox_v, fox_lft, page_table,
                                      fox_w_q, fox_w_k, fox_w_v, fox_w_f, fox_b_f, fox_w_o)
            for lst, a in zip(fox_new, rows):
                lst.append(a)
        elif kind == 1:
            hp, hs, sp, ss = _sconv_layer(hp, hs, up, us, dims, j, state_sconv, sc_w_in, sc_w_conv, sc_w_out)
            sc_new[0].append(sp)
            sc_new[1].append(ss)
        else:
            hp, hs, rows = _dsa_layer(hp, hs, up, us, dims, j, dsa_k, dsa_v, cache_dsa_kidx, page_table, rel_bias,
                                      dsa_w_q, dsa_w_k, dsa_w_v, dsa_w_o, dsa_w_qi, dsa_w_ki, dsa_w_wi)
            for lst, a in zip(dsa_new, rows):
                lst.append(a)
        vp = _rmsnorm(hp, norm_ffn[i], BF16)
        vs = _rmsnorm(hs, norm_ffn[i], BF16)
        hp, hs, cp, cs = _ffn_layer(hp, hs, vp, vs, dims, i, state_ffn_conv, ffn_w_up, ffn_w_conv, w_down16)
        ffn_new[0].append(cp)
        ffn_new[1].append(cs)
    y_prompt = _rmsnorm(hp, norm_final, F32).reshape(Bp, T, D)
    y_sample = _rmsnorm(hs, norm_final, F32).reshape(Bs, Q, D)
    return (y_prompt, y_sample,
            jnp.stack(fox_new[0]), jnp.stack(fox_new[1]), jnp.stack(fox_new[2]),
            jnp.stack(fox_new[3]), jnp.stack(fox_new[4]), jnp.stack(fox_new[5]),
            jnp.stack(sc_new[0]), jnp.stack(sc_new[1]),
            jnp.stack(dsa_new[0]), jnp.stack(dsa_new[1]), jnp.stack(dsa_new[2]),
            jnp.stack(dsa_new[3]), jnp.stack(dsa_new[4]), jnp.stack(dsa_new[5]),
            jnp.stack(ffn_new[0]), jnp.stack(ffn_new[1]))
```

```python
import functools
import math

import numpy as np
import jax
import jax.numpy as jnp
from jax import lax
from jax.experimental import pallas as pl
from jax.experimental.pallas import tpu as pltpu

F32 = jnp.float32
BF16 = jnp.bfloat16
I32 = jnp.int32

RMS_EPS = 1e-6
NEG_INF = -1e30
TOPK_MAX = 256
N_BUCKETS = 32
MAX_DISTANCE = 128
CONV_WIDTH = 3
LOG2E = 1.4426950408889634

LANE = 128
SUBLANE = 8
VMEM_LIMIT_BYTES = 56 * 1024 * 1024
PAGES_PER_STEP = 8

_NT = (((1,), (1,)), ((), ()))


def _cparams(*sem):
    return pltpu.CompilerParams(dimension_semantics=sem, vmem_limit_bytes=VMEM_LIMIT_BYTES)


def _log_sigmoid(x):
    return jnp.minimum(x, 0.0) - jnp.log1p(jnp.exp(-jnp.abs(x)))


def _split3(x):
    hi = x.astype(BF16)
    r1 = x - hi.astype(F32)
    mid = r1.astype(BF16)
    lo = (r1 - mid.astype(F32)).astype(BF16)
    return hi, mid, lo


def _tri_cumsum(x):
    n = x.shape[-1]
    r = lax.broadcasted_iota(I32, (n, n), 0)
    c = lax.broadcasted_iota(I32, (n, n), 1)
    tri = jnp.where(r <= c, 1.0, 0.0).astype(BF16)
    hi, mid, lo = _split3(x)
    out = jnp.dot(hi, tri, preferred_element_type=F32)
    out = out + jnp.dot(mid, tri, preferred_element_type=F32)
    out = out + jnp.dot(lo, tri, preferred_element_type=F32)
    return out


def _rms_kernel(x_ref, g_ref, o_ref):
    x = x_ref[...]
    ms = jnp.mean(x * x, axis=-1, keepdims=True)
    o_ref[...] = ((x * lax.rsqrt(ms + RMS_EPS)) * g_ref[...]).astype(o_ref.dtype)


def _rmsnorm(h, g, out_dtype):
    M, D = h.shape
    tm = min(512, M)
    return pl.pallas_call(
        _rms_kernel,
        grid=(M // tm,),
        in_specs=[pl.BlockSpec((tm, D), lambda i: (i, 0)), pl.BlockSpec((1, D), lambda i: (0, 0))],
        out_specs=pl.BlockSpec((tm, D), lambda i: (i, 0)),
        out_shape=jax.ShapeDtypeStruct((M, D), out_dtype),
        compiler_params=_cparams("parallel"),
        name="rmsnorm",
    )(h, g.reshape(1, D).astype(F32))


def _mm_kernel(x_ref, w_ref, r_ref, o_ref):
    o_ref[...] = r_ref[...] + jnp.dot(x_ref[...], w_ref[...], preferred_element_type=F32)


def _mm(x, w, layer, res, *, tm=512, tn=512, name="mm"):
    M, K = x.shape
    N = w.shape[2]
    tm = min(tm, M)
    tn = min(tn, N)
    assert M % tm == 0 and N % tn == 0, (M, N, tm, tn)
    return pl.pallas_call(
        _mm_kernel,
        grid=(M // tm, N // tn),
        in_specs=[pl.BlockSpec((tm, K), lambda i, j: (i, 0)),
                  pl.BlockSpec((None, K, tn), lambda i, j: (layer, 0, j)),
                  pl.BlockSpec((tm, tn), lambda i, j: (i, j))],
        out_specs=pl.BlockSpec((tm, tn), lambda i, j: (i, j)),
        out_shape=jax.ShapeDtypeStruct((M, N), F32),
        compiler_params=_cparams("parallel", "parallel"),
        name=name,
    )(x, w, res)


def _mmw_kernel(*refs, has_res, has_ls, emit_f32, emit_bf16, head_major, has_tail):
    it = iter(refs)
    x_ref = next(it)
    w_ref = next(it)
    b_ref = next(it) if has_ls else None
    r_ref = next(it) if has_res else None
    xs_ref = next(it) if has_tail else None
    rs_ref = next(it) if has_tail and has_res else None
    o32_ref = next(it) if emit_f32 else None
    o16_ref = next(it) if emit_bf16 else None
    os_ref = next(it) if has_tail else None
    wbf_ref = next(it)

    def project(x, r):
        acc = jnp.dot(x, wbf_ref[...], preferred_element_type=F32)
        if has_ls:
            acc = _log_sigmoid(acc + b_ref[...])
        if r is not None:
            acc = r[...] + acc
        return acc

    @pl.when(pl.program_id(1) == 0)
    def _():
        wbf_ref[...] = w_ref[...].astype(BF16)
        if has_tail:
            os_ref[...] = project(xs_ref[...], rs_ref)

    acc = project(x_ref[...], r_ref)
    if emit_f32:
        o32_ref[...] = acc
    if emit_bf16:
        if head_major:
            for hh in range(o16_ref.shape[0]):
                o16_ref[hh] = acc[:, hh * LANE:(hh + 1) * LANE].astype(BF16)
        else:
            o16_ref[...] = acc.astype(BF16)


def _mmw(x, w, layer, *, xs=None, res=None, res_s=None, ls_bias=None, emit_f32=True, emit_bf16=False,
         head_major=False, tm=1024, tn=512, name="mmw"):
    M, K = x.shape
    N = w.shape[2]
    tm = min(tm, M)
    tn = min(tn, N)
    assert M % tm == 0 and N % tn == 0, (M, N, tm, tn)
    has_tail = xs is not None
    in_specs = [pl.BlockSpec((tm, K), lambda j, i: (i, 0)), pl.BlockSpec((None, K, tn), lambda j, i: (layer, 0, j))]
    args = [x, w]
    if ls_bias is not None:
        in_specs.append(pl.BlockSpec((1, tn), lambda j, i: (0, j)))
        args.append(ls_bias.reshape(1, N).astype(F32))
    if res is not None:
        in_specs.append(pl.BlockSpec((tm, tn), lambda j, i: (i, j)))
        args.append(res)
    if has_tail:
        Ms = xs.shape[0]
        in_specs.append(pl.BlockSpec((Ms, K), lambda j, i: (0, 0)))
        args.append(xs)
        if res is not None:
            in_specs.append(pl.BlockSpec((Ms, tn), lambda j, i: (0, j)))
            args.append(res_s)
    out_specs, out_shape = [], []
    if emit_f32:
        out_specs.append(pl.BlockSpec((tm, tn), lambda j, i: (i, j)))
        out_shape.append(jax.ShapeDtypeStruct((M, N), F32))
    if emit_bf16:
        if head_major:
            out_specs.append(pl.BlockSpec((tn // LANE, tm, LANE), lambda j, i: (j, i, 0)))
            out_shape.append(jax.ShapeDtypeStruct((N // LANE, M, LANE), BF16))
        else:
            out_specs.append(pl.BlockSpec((tm, tn), lambda j, i: (i, j)))
            out_shape.append(jax.ShapeDtypeStruct((M, N), BF16))
    if has_tail:
        out_specs.append(pl.BlockSpec((Ms, tn), lambda j, i: (0, j)))
        out_shape.append(jax.ShapeDtypeStruct((Ms, N), F32))
    outs = pl.pallas_call(
        functools.partial(_mmw_kernel, has_res=res is not None, has_ls=ls_bias is not None, emit_f32=emit_f32,
                          emit_bf16=emit_bf16, head_major=head_major, has_tail=has_tail),
        grid=(N // tn, M // tm),
        in_specs=in_specs,
        out_specs=out_specs,
        out_shape=out_shape,
        scratch_shapes=[pltpu.VMEM((K, tn), BF16)],
        compiler_params=_cparams("arbitrary", "arbitrary"),
        name=name,
    )(*args)
    return outs[0] if len(outs) == 1 else tuple(outs)


def _gconv_kernel(x_ref, *refs, mode, nw, tm, seq_blocks, tail_len, row_chunks):
    w_refs = refs[:nw]
    wc_ref, xs_ref, p1_ref, p2_ref, o_ref, st_ref, os_ref, sts_ref, wbf_ref, buf_ref, carry_ref, bufs_ref = refs[nw:]
    i = pl.program_id(1)
    ms = xs_ref.shape[0]

    def gated(x):
        tn = wc_ref.shape[1]
        y = jnp.dot(x, wbf_ref[...], preferred_element_type=F32)
        ys = [y[:, k * tn:(k + 1) * tn] for k in range(nw)]
        if mode == "ffn":
            return ys[0], ys[1]
        return ys[1] * ys[2], ys[0]

    def finish(cin, other, x1, x2):
        wc = wc_ref[...]
        y = wc[0:1, :] * x2 + wc[1:2, :] * x1 + wc[2:3, :] * cin
        if mode == "ffn":
            return ((y * (1.0 / (1.0 + jnp.exp(-y)))) * other).astype(BF16)
        return (other * y).astype(BF16)

    @pl.when(i == 0)
    def _():
        for k in range(nw):
            wbf_ref[:, k * wc_ref.shape[1]:(k + 1) * wc_ref.shape[1]] = w_refs[k][...].astype(BF16)
        cin, other = gated(xs_ref[...])
        bufs_ref[0:SUBLANE, :] = jnp.zeros((SUBLANE, cin.shape[1]), F32)
        bufs_ref[SUBLANE:SUBLANE + ms, :] = cin
        rmod = lax.broadcasted_iota(I32, (ms, 1), 0) % tail_len
        x1 = jnp.where(rmod >= 1, bufs_ref[SUBLANE - 1:SUBLANE - 1 + ms, :], p1_ref[...])
        x2 = jnp.where(rmod >= 2, bufs_ref[SUBLANE - 2:SUBLANE - 2 + ms, :], p2_ref[...])
        sts_ref[...] = cin
        os_ref[...] = finish(cin, other, x1, x2)

    first = (i % seq_blocks) == 0

    @pl.when(first)
    def _():
        buf_ref[0:SUBLANE, :] = jnp.zeros((SUBLANE, buf_ref.shape[1]), F32)

    @pl.when(jnp.logical_not(first))
    def _():
        buf_ref[0:SUBLANE, :] = carry_ref[...]

    rc = tm // row_chunks
    for c in range(row_chunks):
        lo = c * rc
        cin, other = gated(x_ref[lo:lo + rc, :])
        buf_ref[SUBLANE + lo:SUBLANE + lo + rc, :] = cin
        o_ref[lo:lo + rc, :] = finish(cin, other, buf_ref[SUBLANE - 1 + lo:SUBLANE - 1 + lo + rc, :],
                                      buf_ref[SUBLANE - 2 + lo:SUBLANE - 2 + lo + rc, :])
    tail = buf_ref[tm:tm + SUBLANE, :]
    carry_ref[...] = tail
    st_ref[...] = tail


def _gconv(x, xs, w, layer, col_offsets, nc, wconv, prefix, *, mode, seq_len, tail_len, tm, tn=256, name="gconv"):
    M, K = x.shape
    Ms = xs.shape[0]
    tm = min(tm, seq_len)
    tn = min(tn, nc)
    assert M % tm == 0 and nc % tn == 0 and seq_len % tm == 0
    nw = len(col_offsets)
    in_specs = [pl.BlockSpec((tm, K), lambda j, i: (i, 0))]
    args = [x]
    for off in col_offsets:
        assert off % tn == 0
        in_specs.append(pl.BlockSpec((None, K, tn), functools.partial(lambda j, i, o: (layer, 0, o + j), o=off // tn)))
        args.append(w)
    in_specs.append(pl.BlockSpec((None, CONV_WIDTH, tn), lambda j, i: (layer, 0, j)))
    args.append(wconv)
    in_specs.append(pl.BlockSpec((Ms, K), lambda j, i: (0, 0)))
    args.append(xs)
    for p in prefix:
        in_specs.append(pl.BlockSpec((Ms, tn), lambda j, i: (0, j)))
        args.append(p)
    return pl.pallas_call(
        functools.partial(_gconv_kernel, mode=mode, nw=nw, tm=tm, seq_blocks=seq_len // tm, tail_len=tail_len,
                          row_chunks=max(tm // 512, 1)),
        grid=(nc // tn, M // tm),
        in_specs=in_specs,
        out_specs=[pl.BlockSpec((tm, tn), lambda j, i: (i, j)),
                   pl.BlockSpec((None, SUBLANE, tn), lambda j, i: (i, 0, j)),
                   pl.BlockSpec((Ms, tn), lambda j, i: (0, j)),
                   pl.BlockSpec((Ms, tn), lambda j, i: (0, j))],
        out_shape=[jax.ShapeDtypeStruct((M, nc), BF16), jax.ShapeDtypeStruct((M // tm, SUBLANE, nc), F32),
                   jax.ShapeDtypeStruct((Ms, nc), BF16), jax.ShapeDtypeStruct((Ms, nc), F32)],
        scratch_shapes=[pltpu.VMEM((K, nw * tn), BF16), pltpu.VMEM((SUBLANE + tm, tn), F32),
                        pltpu.VMEM((SUBLANE, tn), F32), pltpu.VMEM((SUBLANE + Ms, tn), F32)],
        compiler_params=_cparams("arbitrary", "arbitrary"),
        name=name,
    )(*args)


def _conv_prefix(state, seq_len):
    Bn, _, C = state.shape
    z = jnp.zeros((Bn, seq_len, C), F32)
    p1 = z.at[:, 0].set(state[:, 1])
    p2 = z.at[:, 0].set(state[:, 0]).at[:, 1].set(state[:, 1])
    return p1.reshape(Bn * seq_len, C), p2.reshape(Bn * seq_len, C)


def _cum_kernel(u_ref, wft_ref, bf_ref, o_ref, carry_ref):
    t = pl.program_id(1)

    @pl.when(t == 0)
    def _():
        carry_ref[...] = jnp.zeros_like(carry_ref)

    z = lax.dot_general(wft_ref[...], u_ref[...], _NT, preferred_element_type=F32) + bf_ref[...]
    c = _tri_cumsum(_log_sigmoid(z)) + carry_ref[...]
    o_ref[...] = c
    carry_ref[...] = c[:, c.shape[1] - 1:c.shape[1]]


def _fox_cum(u, wft, bf, Bn, T, tk):
    M, D = u.shape
    H = wft.shape[0]
    nt = T // tk
    return pl.pallas_call(
        _cum_kernel,
        grid=(Bn, nt),
        in_specs=[pl.BlockSpec((tk, D), lambda b, t: (b * nt + t, 0)),
                  pl.BlockSpec((H, D), lambda b, t: (0, 0)),
                  pl.BlockSpec((H, 1), lambda b, t: (0, 0))],
        out_specs=pl.BlockSpec((None, None, H, tk), lambda b, t: (b, t, 0, 0)),
        out_shape=jax.ShapeDtypeStruct((Bn, nt, H, tk), F32),
        scratch_shapes=[pltpu.VMEM((H, 1), F32)],
        compiler_params=_cparams("arbitrary", "arbitrary"),
        name="fox_cum",
    )(u, wft, bf.reshape(H, 1).astype(F32))


def _fox_attn_kernel(q_ref, k_ref, v_ref, cum_ref, o_ref, m_ref, l_ref, acc_ref, *, G, tq, tk, scale):
    qi = pl.program_id(2)
    hd = q_ref.shape[-1]
    reps = tk // LANE
    m_ref[...] = jnp.full(m_ref.shape, NEG_INF, F32)
    l_ref[...] = jnp.zeros(l_ref.shape, F32)
    acc_ref[...] = jnp.zeros(acc_ref.shape, F32)
    n_full = (qi * tq) // tk

    q = q_ref[...].reshape(G * tq, hd)

    def tile(j, masked):
        start = pl.multiple_of(j * tk, tk)
        kt = k_ref[pl.ds(start, tk), :]
        vt = v_ref[pl.ds(start, tk), :]
        s = lax.dot_general(q, kt, _NT, preferred_element_type=F32) * (scale * LOG2E)
        s = s.reshape(G, tq, tk) - cum_ref[j] * LOG2E
        if masked:
            row = lax.broadcasted_iota(I32, (tq, tk), 0) + qi * tq
            col = lax.broadcasted_iota(I32, (tq, tk), 1) + j * tk
            s = jnp.where((col <= row)[None], s, NEG_INF)
        s = s.reshape(G * tq, tk)
        m_prev = m_ref[...]
        m_new = jnp.maximum(m_prev, jnp.max(s, axis=1, keepdims=True))
        p = jnp.exp2(s - jnp.concatenate([m_new] * reps, axis=1))
        alpha = jnp.exp2(m_prev - m_new)
        l_ref[...] = alpha * l_ref[...] + jnp.sum(p, axis=1, keepdims=True)
        acc_ref[...] = alpha * acc_ref[...] + jnp.dot(p.astype(BF16), vt, preferred_element_type=F32)
        m_ref[...] = m_new

    def body(j, carry):
        tile(j, False)
        return carry

    lax.fori_loop(0, n_full, body, 0)
    tile(n_full, True)
    out = acc_ref[...] / l_ref[...]
    for g in range(G):
        o_ref[:, g * hd:(g + 1) * hd] = out[g * tq:(g + 1) * tq, :].astype(o_ref.dtype)


def _fox_attn_prompt(q_hm, k16, v16, cum, Bn, T, G, scale, tq, tk):
    H, M, hd = q_hm.shape
    assert hd == LANE and tk % tq == 0
    KVH = H // G
    nq = T // tq
    return pl.pallas_call(
        functools.partial(_fox_attn_kernel, G=G, tq=tq, tk=tk, scale=scale),
        grid=(Bn, KVH, nq),
        in_specs=[pl.BlockSpec((G, tq, hd), lambda b, h, i: (h, b * nq + i, 0)),
                  pl.BlockSpec((T, hd), lambda b, h, i: (b, h)),
                  pl.BlockSpec((T, hd), lambda b, h, i: (b, h)),
                  pl.BlockSpec((None, None, T // tk, G, 1, tk), lambda b, h, i: (b, h, 0, 0, 0, 0))],
        out_specs=pl.BlockSpec((tq, G * hd), lambda b, h, i: (b * nq + i, h)),
        out_shape=jax.ShapeDtypeStruct((M, H * hd), BF16),
        scratch_shapes=[pltpu.VMEM((G * tq, LANE), F32), pltpu.VMEM((G * tq, LANE), F32),
                        pltpu.VMEM((G * tq, hd), F32)],
        compiler_params=_cparams("parallel", "parallel", "arbitrary"),
        name="fox_attn_prompt",
    )(q_hm, k16, v16, cum)


def _paged_attn_kernel(*refs, kind, P, KVH, G, Q, page, hd, scale):
    pt_ref = refs[0]
    del pt_ref
    pos = 1
    q_ref = refs[pos]; pos += 1
    kp_refs = refs[pos:pos + P]; pos += P
    vp_refs = refs[pos:pos + P]; pos += P
    kn_ref, vn_ref = refs[pos], refs[pos + 1]; pos += 2
    if kind == "fox":
        lf_refs = refs[pos:pos + P]; pos += P
        lfn_ref = refs[pos]; pos += 1
    else:
        mp_ref, mn_ref, tl_ref, tn_ref = refs[pos:pos + 4]; pos += 4
    o_ref = refs[pos]; pos += 1
    kb_ref, vb_ref, m_ref, l_ref, acc_ref = refs[pos:pos + 5]; pos += 5
    if kind == "fox":
        cum_ref, carry_ref = refs[pos:pos + 2]

    jb = pl.program_id(1)
    nb = pl.num_programs(1)
    R = G * Q

    @pl.when(jb == 0)
    def _():
        m_ref[...] = jnp.full(m_ref.shape, NEG_INF, F32)
        l_ref[...] = jnp.zeros(l_ref.shape, F32)
        acc_ref[...] = jnp.zeros(acc_ref.shape, F32)
        if kind == "fox":
            carry_ref[...] = jnp.zeros(carry_ref.shape, F32)

    for p in range(P):
        for kvh in range(KVH):
            kb_ref[kvh, p * page:(p + 1) * page, :] = kp_refs[p][pl.ds(kvh, page, stride=KVH), :].astype(BF16)
            vb_ref[kvh, p * page:(p + 1) * page, :] = vp_refs[p][pl.ds(kvh, page, stride=KVH), :].astype(BF16)
        if kind == "fox":
            c = _tri_cumsum(lf_refs[p][...]) + carry_ref[...]
            cum_ref[:, p * page:(p + 1) * page] = c
            carry_ref[...] = c[:, page - 1:page]

    def logits(keys_of):
        return jnp.concatenate(
            [lax.dot_general(q_ref[kvh], keys_of(kvh), _NT, preferred_element_type=F32) for kvh in range(KVH)],
            axis=0) * (scale * LOG2E)

    def update(s, vals_of):
        m_prev = m_ref[...]
        m_new = jnp.maximum(m_prev, jnp.max(s, axis=1, keepdims=True))
        pr = jnp.exp2(s - jnp.concatenate([m_new] * (s.shape[1] // LANE), axis=1))
        alpha = jnp.exp2(m_prev - m_new)
        l_ref[...] = alpha * l_ref[...] + jnp.sum(pr, axis=1, keepdims=True)
        p16 = pr.astype(BF16)
        pv = jnp.concatenate(
            [jnp.dot(p16[kvh * R:(kvh + 1) * R, :], vals_of(kvh), preferred_element_type=F32) for kvh in range(KVH)],
            axis=0)
        acc_ref[...] = alpha * acc_ref[...] + pv
        m_ref[...] = m_new

    def head_rows(tile):
        return jnp.concatenate(
            [jnp.broadcast_to(tile[h:h + 1, :], (Q, tile.shape[1])) for h in range(KVH * G)], axis=0)

    def query_rows(tile):
        return jnp.concatenate([tile] * (KVH * G), axis=0)

    n = P * page
    s = logits(lambda kvh: kb_ref[kvh])
    if kind == "fox":
        s = s - head_rows(cum_ref[...]) * LOG2E
    else:
        s = s + query_rows(mp_ref[...])
        is_last = jnp.where(jb == nb - 1, LOG2E, 0.0)
        tail = s[:, n - page:] + is_last * tl_ref[...]
        s = tail if n == page else jnp.concatenate([s[:, :n - page], tail], axis=1)
    update(s, lambda kvh: vb_ref[kvh])

    @pl.when(jb == nb - 1)
    def _():
        kn = kn_ref[...].astype(BF16)
        vn = vn_ref[...].astype(BF16)
        qpos = lax.broadcasted_iota(I32, (KVH * R, page), 0) % Q
        kpos = lax.broadcasted_iota(I32, (KVH * R, page), 1)
        s = logits(lambda kvh: kn[:, kvh * hd:(kvh + 1) * hd])
        if kind == "fox":
            s = s - head_rows(_tri_cumsum(lfn_ref[...]) + carry_ref[...]) * LOG2E
        else:
            s = s + query_rows(mn_ref[...]) + tn_ref[...] * LOG2E
        s = jnp.where(kpos <= qpos, s, NEG_INF)
        update(s, lambda kvh: vn[:, kvh * hd:(kvh + 1) * hd])
        out = acc_ref[...] / l_ref[...]
        for kvh in range(KVH):
            o_ref[kvh] = out[kvh * R:(kvh + 1) * R, :]


def _paged_attn(kind, q, pool_k, pool_v, layer, k_new, v_new, page_table, extras, scale):
    Bn, KVH, R, hd = q.shape
    page = pool_k.shape[2] // KVH
    n_pages = page_table.shape[1]
    P = min(PAGES_PER_STEP, n_pages)
    assert n_pages % P == 0
    nb = n_pages // P
    H = None
    if kind == "fox":
        H = extras[0].shape[2]
        G = H // KVH
    else:
        G = extras[2].shape[0] // (KVH * extras[0].shape[1])
    Q = R // G

    def page_map(p, nd):
        return lambda b, j, pt: (layer, pt[b, j * P + p]) + (0,) * nd

    in_specs = [pl.BlockSpec((None, KVH, R, hd), lambda b, j, pt: (b, 0, 0, 0))]
    args = [q]
    for pool in (pool_k, pool_v):
        for p in range(P):
            in_specs.append(pl.BlockSpec((None, None, page * KVH, hd), page_map(p, 2)))
            args.append(pool)
    for new in (k_new, v_new):
        in_specs.append(pl.BlockSpec((None, page, KVH * hd), lambda b, j, pt: (b, 0, 0)))
        args.append(new)
    scratch = [pltpu.VMEM((KVH, P * page, hd), BF16), pltpu.VMEM((KVH, P * page, hd), BF16),
               pltpu.VMEM((KVH * R, LANE), F32), pltpu.VMEM((KVH * R, LANE), F32),
               pltpu.VMEM((KVH * R, hd), F32)]
    if kind == "fox":
        pool_lft, lf_newt = extras
        for p in range(P):
            in_specs.append(pl.BlockSpec((None, None, H, page), page_map(p, 2)))
            args.append(pool_lft)
        in_specs.append(pl.BlockSpec((None, H, page), lambda b, j, pt: (b, 0, 0)))
        args.append(lf_newt)
        scratch += [pltpu.VMEM((H, P * page), F32), pltpu.VMEM((H, 1), F32)]
    else:
        madd_past, madd_new, bias_last, bias_new = extras
        in_specs.append(pl.BlockSpec((None, Q, P * page), lambda b, j, pt: (b, 0, j)))
        args.append(madd_past)
        in_specs.append(pl.BlockSpec((None, Q, page), lambda b, j, pt: (b, 0, 0)))
        args.append(madd_new)
        for t in (bias_last, bias_new):
            in_specs.append(pl.BlockSpec((KVH * R, page), lambda b, j, pt: (0, 0)))
            args.append(t)
    grid_spec = pltpu.PrefetchScalarGridSpec(
        num_scalar_prefetch=1,
        grid=(Bn, nb),
        in_specs=in_specs,
        out_specs=pl.BlockSpec((None, KVH, R, hd), lambda b, j, pt: (b, 0, 0, 0)),
        scratch_shapes=scratch,
    )
    return pl.pallas_call(
        functools.partial(_paged_attn_kernel, kind=kind, P=P, KVH=KVH, G=G, Q=Q, page=page, hd=hd, scale=scale),
        grid_spec=grid_spec,
        out_shape=jax.ShapeDtypeStruct((Bn, KVH, R, hd), F32),
        compiler_params=_cparams("arbitrary", "arbitrary"),
        name=kind + "_attn_sample",
    )(page_table, *args)


def _sortable(s):
    bits = pltpu.bitcast(s + 0.0, I32)
    return jnp.where(bits >= 0, bits, bits ^ jnp.int32(0x7FFFFFFF))


def _kth_largest_key(count_ge, rows, k):
    sign = jnp.int32(-2 ** 31)

    def body(i, t):
        cand = t | jnp.left_shift(jnp.int32(1), 31 - i)
        cnt = count_ge(cand ^ sign)
        return jnp.where(cnt >= k, cand, t)

    t = lax.fori_loop(0, 32, body, jnp.zeros((rows, 1), I32))
    return t ^ sign


def _last_tie_index(ties_before, need, rows, n_keys):
    nbits = max(int(n_keys - 1).bit_length(), 1)

    def body(i, m):
        cand = m | jnp.left_shift(jnp.int32(1), nbits - 1 - i)
        return jnp.where(ties_before(cand) < need, cand, m)

    return lax.fori_loop(0, nbits, body, jnp.zeros((rows, 1), I32))


def _dsa_score_kernel(qi_ref, ki_ref, wi_ref, o_ref, acc_ref, wb_ref, *, HI, tq, T, q0, k_top, scale):
    wi = wi_ref[...]
    for h in range(HI):
        wb_ref[h] = jnp.broadcast_to(wi[:, h:h + 1], (tq, LANE))
    acc_ref[...] = jnp.zeros(acc_ref.shape, F32)
    ki = ki_ref[...]
    reps = T // LANE

    def body(h, carry):
        d = lax.dot_general(qi_ref[h], ki, _NT, preferred_element_type=F32)
        w = jnp.concatenate([wb_ref[h]] * reps, axis=1)
        acc_ref[...] += w * jnp.maximum(d, 0.0)
        return carry

    lax.fori_loop(0, HI, body, 0)
    qpos = lax.broadcasted_iota(I32, (tq, T), 0) + q0
    kpos = lax.broadcasted_iota(I32, (tq, T), 1)
    adm = kpos <= qpos
    key = _sortable(jnp.where(adm, acc_ref[...] * scale, NEG_INF))

    def count_ge(thr):
        return jnp.sum(jnp.where(key >= thr, 1, 0), axis=-1, keepdims=True)

    thr = _kth_largest_key(count_ge, tq, k_top)
    sel = jnp.logical_and(key >= thr, adm)
    o_ref[...] = jnp.where(sel, 0.0, NEG_INF)

    @pl.when(jnp.max(jnp.sum(jnp.where(sel, 1, 0), axis=-1, keepdims=True)) > k_top)
    def _():
        gt = key > thr
        tie = key == thr
        need = k_top - jnp.sum(jnp.where(gt, 1, 0), axis=-1, keepdims=True)

        def ties_before(m):
            return jnp.sum(jnp.where(jnp.logical_and(tie, kpos < m), 1, 0), axis=-1, keepdims=True)

        last = _last_tie_index(ties_before, need, tq, T)
        keep = jnp.logical_or(gt, jnp.logical_and(tie, kpos <= last))
        o_ref[...] = jnp.where(jnp.logical_and(keep, adm), 0.0, NEG_INF)


def _dsa_select_prompt(qi_hm, ki16, wi, Bn, T, c, tq, k_top, scale):
    HI, M, DI = qi_hm.shape
    nq = T // tq
    tk = (c + 1) * tq
    wl = wi.shape[1]
    return pl.pallas_call(
        functools.partial(_dsa_score_kernel, HI=HI, tq=tq, T=tk, q0=c * tq, k_top=k_top, scale=scale),
        grid=(Bn,),
        in_specs=[pl.BlockSpec((HI, tq, DI), lambda b: (0, b * nq + c, 0)),
                  pl.BlockSpec((None, tk, DI), lambda b: (b, 0, 0)),
                  pl.BlockSpec((tq, wl), lambda b: (b * nq + c, 0))],
        out_specs=pl.BlockSpec((None, tq, tk), lambda b: (b, 0, 0)),
        out_shape=jax.ShapeDtypeStruct((Bn, tq, tk), F32),
        scratch_shapes=[pltpu.VMEM((tq, tk), F32), pltpu.VMEM((HI, tq, LANE), F32)],
        compiler_params=_cparams("parallel"),
        name="dsa_select_prompt",
    )(qi_hm, ki16, wi)


def _dsa_attn_kernel(q_ref, k_ref, v_ref, madd_ref, td_ref, tl_ref, o_ref, s_ref, *, G, tq, iq0, scale):
    iq = iq0 + pl.program_id(1)
    hd = q_ref.shape[-1]
    k = k_ref[...]
    v = v_ref[...]
    madd = madd_ref[...]
    bt = td_ref.shape[-1]
    for g in range(G):
        s_ref[g] = lax.dot_general(q_ref[g], k, _NT, preferred_element_type=F32) * (scale * LOG2E) + madd
        for r in range(tq // bt):
            rows = slice(r * bt, (r + 1) * bt)
            a = iq * (tq // bt) + r
            s_ref[g, rows, pl.ds(pl.multiple_of(a * bt, bt), bt)] += td_ref[g] * LOG2E

            @pl.when(a > 0)
            def _():
                s_ref[g, rows, pl.ds(pl.multiple_of(jnp.maximum(a - 1, 0) * bt, bt), bt)] += tl_ref[g] * LOG2E

        s = s_ref[g]
        m = jnp.max(s, axis=-1, keepdims=True)
        p = jnp.exp2(s - m)
        l = jnp.sum(p, axis=-1, keepdims=True)
        o = jnp.dot(p.astype(BF16), v, preferred_element_type=F32) / l
        o_ref[:, g * hd:(g + 1) * hd] = o.astype(o_ref.dtype)


def _dsa_attn_prompt(q_hm, k16, v16, madd, t_diag, t_left, Bn, T, c, tqs, tq, G, scale):
    H, M, hd = q_hm.shape
    KVH = H // G
    tk = (c + 1) * tqs
    nsub = tqs // tq
    nq = T // tq
    return pl.pallas_call(
        functools.partial(_dsa_attn_kernel, G=G, tq=tq, iq0=c * nsub, scale=scale),
        grid=(Bn, nsub, KVH),
        in_specs=[pl.BlockSpec((G, tq, hd), lambda b, i, h: (h, b * nq + c * nsub + i, 0)),
                  pl.BlockSpec((None, tk, hd), lambda b, i, h: (b, 0, h)),
                  pl.BlockSpec((None, tk, hd), lambda b, i, h: (b, 0, h)),
                  pl.BlockSpec((None, tq, tk), lambda b, i, h: (b, i, 0)),
                  pl.BlockSpec((G,) + t_diag.shape[1:], lambda b, i, h: (h, 0, 0)),
                  pl.BlockSpec((G,) + t_left.shape[1:], lambda b, i, h: (h, 0, 0))],
        out_specs=pl.BlockSpec((None, tq, G * hd), lambda b, i, h: (b, i, h)),
        out_shape=jax.ShapeDtypeStruct((Bn, tqs, H * hd), BF16),
        scratch_shapes=[pltpu.VMEM((G, tq, tk), F32)],
        compiler_params=_cparams("parallel", "parallel", "arbitrary"),
        name="dsa_attn_prompt",
    )(q_hm, k16, v16, madd, t_diag, t_left)


def _dsa_score_sample_kernel(*refs, P, HI, Q, page, k_top, scale):
    pt_ref = refs[0]
    del pt_ref
    qi_ref, wi_ref = refs[1], refs[2]
    kp_refs = refs[3:3 + P]
    kn_ref = refs[3 + P]
    mp_ref, mn_ref = refs[4 + P], refs[5 + P]
    kb_ref, sp_ref, sn_ref = refs[6 + P:9 + P]

    jb = pl.program_id(1)
    nb = pl.num_programs(1)
    n = P * page
    qi = qi_ref[...]
    wcol = wi_ref[...]

    def score(keys16):
        d = lax.dot_general(qi, keys16, _NT, preferred_element_type=F32)
        d = jnp.broadcast_to(wcol, d.shape) * jnp.maximum(d, 0.0)
        tot = d[0:Q]
        for h in range(1, HI):
            tot = tot + d[h * Q:(h + 1) * Q]
        return tot * scale

    for p in range(P):
        kb_ref[p * page:(p + 1) * page, :] = kp_refs[p][...].astype(BF16)
    sp_ref[jb] = score(kb_ref[...])

    @pl.when(jb == nb - 1)
    def _():
        qpos = lax.broadcasted_iota(I32, (Q, page), 0)
        kpos = lax.broadcasted_iota(I32, (Q, page), 1)
        adm = kpos <= qpos
        sn_ref[...] = jnp.where(adm, score(kn_ref[...].astype(BF16)), NEG_INF)
        key_p = _sortable(sp_ref[...])
        key_n = _sortable(sn_ref[...])

        def count_ge(thr):
            cp = jnp.sum(jnp.where(key_p >= thr[None], 1, 0), axis=-1, keepdims=True)
            cn = jnp.sum(jnp.where(key_n >= thr, 1, 0), axis=-1, keepdims=True)
            return jnp.sum(cp, axis=0) + cn

        thr = _kth_largest_key(count_ge, Q, k_top)
        sel_n = jnp.logical_and(key_n >= thr, adm)
        mp_ref[...] = jnp.where(key_p >= thr[None], 0.0, NEG_INF)
        mn_ref[...] = jnp.where(sel_n, 0.0, NEG_INF)

        def count(mask_p, mask_n):
            cp = jnp.sum(jnp.where(mask_p, 1, 0), axis=-1, keepdims=True)
            return jnp.sum(cp, axis=0) + jnp.sum(jnp.where(mask_n, 1, 0), axis=-1, keepdims=True)

        @pl.when(jnp.max(count(key_p >= thr[None], sel_n)) > k_top)
        def _():
            idx_p = (lax.broadcasted_iota(I32, key_p.shape, 0) * n + lax.broadcasted_iota(I32, key_p.shape, 2))
            n_past = key_p.shape[0] * n
            idx_n = kpos + n_past
            gt_p, gt_n = key_p > thr[None], key_n > thr
            tie_p, tie_n = key_p == thr[None], key_n == thr
            need = k_top - count(gt_p, gt_n)

            def ties_before(m):
                return count(jnp.logical_and(tie_p, idx_p < m[None]), jnp.logical_and(tie_n, idx_n < m))

            last = _last_tie_index(ties_before, need, Q, n_past + page)
            keep_p = jnp.logical_or(gt_p, jnp.logical_and(tie_p, idx_p <= last[None]))
            keep_n = jnp.logical_or(gt_n, jnp.logical_and(tie_n, idx_n <= last))
            mp_ref[...] = jnp.where(keep_p, 0.0, NEG_INF)
            mn_ref[...] = jnp.where(jnp.logical_and(keep_n, adm), 0.0, NEG_INF)


def _dsa_select_sample(qi, wi, pool_ki, layer, ki_new, page_table, Q, k_top, scale):
    Bn, RQ, DI = qi.shape
    page = pool_ki.shape[2]
    n_pages = page_table.shape[1]
    P = min(PAGES_PER_STEP, n_pages)
    nb = n_pages // P
    HI = RQ // Q
    in_specs = [pl.BlockSpec((None, RQ, DI), lambda b, j, pt: (b, 0, 0)),
                pl.BlockSpec((None, RQ, 1), lambda b, j, pt: (b, 0, 0))]
    args = [qi, wi]
    for p in range(P):
        in_specs.append(pl.BlockSpec((None, None, page, DI), functools.partial(
            lambda b, j, pt, p: (layer, pt[b, j * P + p], 0, 0), p=p)))
        args.append(pool_ki)
    in_specs.append(pl.BlockSpec((None, page, DI), lambda b, j, pt: (b, 0, 0)))
    args.append(ki_new)
    grid_spec = pltpu.PrefetchScalarGridSpec(
        num_scalar_prefetch=1,
        grid=(Bn, nb),
        in_specs=in_specs,
        out_specs=[pl.BlockSpec((None, nb, Q, P * page), lambda b, j, pt: (b, 0, 0, 0)),
                   pl.BlockSpec((None, Q, page), lambda b, j, pt: (b, 0, 0))],
        scratch_shapes=[pltpu.VMEM((P * page, DI), BF16), pltpu.VMEM((nb, Q, P * page), F32),
                        pltpu.VMEM((Q, page), F32)],
    )
    return pl.pallas_call(
        functools.partial(_dsa_score_sample_kernel, P=P, HI=HI, Q=Q, page=page, k_top=k_top, scale=scale),
        grid_spec=grid_spec,
        out_shape=[jax.ShapeDtypeStruct((Bn, nb, Q, P * page), F32), jax.ShapeDtypeStruct((Bn, Q, page), F32)],
        compiler_params=_cparams("arbitrary", "arbitrary"),
        name="dsa_select_sample",
    )(page_table, *args)


def _t5_bucket_np(dist):
    max_exact = N_BUCKETS // 2
    d = np.maximum(dist, 0)
    ratio = np.log(np.maximum(d, 1).astype(np.float32) / np.float32(max_exact)) / np.float32(
        math.log(MAX_DISTANCE / max_exact))
    large = np.minimum(max_exact + (ratio * (N_BUCKETS - max_exact)).astype(np.int32), N_BUCKETS - 1)
    return np.where(d < max_exact, d, large).astype(np.int32)


def _pad_cols(w, n):
    return jnp.pad(w, ((0, 0), (0, n - w.shape[1])))


def _to_rows(o, Bn, Q, KVH, G, hd):
    o = o.reshape(Bn, KVH, G, Q, hd)
    return jnp.transpose(o, (0, 3, 1, 2, 4)).reshape(Bn * Q, KVH * G * hd)


def _to_heads(q, Bn, Q, KVH, G, hd):
    q = q.reshape(Bn, Q, KVH, G, hd)
    return jnp.transpose(q, (0, 2, 3, 1, 4)).reshape(Bn, KVH, G * Q, hd)


def _pad_page(x, page):
    return jnp.pad(x, ((0, 0), (0, page - x.shape[1]), (0, 0)))


def _fox_layer(hp, hs, up, us, dims, layer, pool_k, pool_v, pool_lft, page_table, w_q, w_k, w_v, w_f, b_f, w_o):
    Bp, T, Bs, Q, H, KVH, hd = dims
    G = H // KVH
    scale = hd ** -0.5
    page = pool_k.shape[2] // KVH
    wf_pad = _pad_cols(w_f[layer], LANE)[None]
    bf_pad = jnp.pad(b_f[layer], (0, LANE - H))
    wft = w_f[layer].T.astype(BF16)
    tq, tk = min(256, T), min(512, T)

    q_hm, qs = _mmw(up, w_q, layer, xs=us, emit_f32=False, emit_bf16=True, head_major=True, name="fox_q")
    k32, k16, ks = _mmw(up, w_k, layer, xs=us, emit_bf16=True, name="fox_k")
    v32, v16, vs = _mmw(up, w_v, layer, xs=us, emit_bf16=True, name="fox_v")
    lf_pad, lfs_pad = _mmw(up, wf_pad, 0, xs=us, ls_bias=bf_pad, tn=LANE, name="fox_logf")
    lfs = lfs_pad[:, :H]

    cum = _fox_cum(up, wft, b_f[layer], Bp, T, tk)
    cum = jnp.transpose(cum.reshape(Bp, T // tk, KVH, G, 1, tk), (0, 2, 1, 3, 4, 5))
    o = _fox_attn_prompt(q_hm, k16, v16, cum, Bp, T, G, scale, tq, tk)

    q_t = _to_heads(qs, Bs, Q, KVH, G, hd).astype(BF16)
    lf_newt = jnp.transpose(_pad_page(lfs.reshape(Bs, Q, H), page), (0, 2, 1))
    o_s = _paged_attn("fox", q_t, pool_k, pool_v, layer,
                      _pad_page(ks.reshape(Bs, Q, KVH * hd), page), _pad_page(vs.reshape(Bs, Q, KVH * hd), page),
                      page_table, (pool_lft, lf_newt), scale)
    o_s = _to_rows(o_s, Bs, Q, KVH, G, hd).astype(BF16)

    hp, hs = _mmw(o, w_o, layer, xs=o_s, res=hp, res_s=hs, name="fox_o")
    rows = (k32.reshape(Bp, T, KVH, hd), v32.reshape(Bp, T, KVH, hd), lf_pad[:, :H].reshape(Bp, T, H),
            ks.reshape(Bs, Q, KVH, hd), vs.reshape(Bs, Q, KVH, hd), lfs.reshape(Bs, Q, H))
    return hp, hs, rows


def _sconv_layer(hp, hs, up, us, dims, layer, state, w_in, w_conv, w_out):
    Bp, T, Bs, Q = dims[:4]
    D = w_out.shape[1]
    zp, stp, zs, sts = _gconv(up, us, w_in, layer, (0, D, 2 * D), D, w_conv, _conv_prefix(state[layer], Q),
                              mode="sconv", seq_len=T, tail_len=Q, tm=512, name="sconv_in")
    hp, hs = _mmw(zp, w_out, layer, xs=zs, res=hp, res_s=hs, name="sconv_out")
    nblk = stp.shape[0] // Bp
    sp = stp.reshape(Bp, nblk, SUBLANE, D)[:, -1, SUBLANE - 2:, :]
    ss = sts.reshape(Bs, Q, D)[:, Q - 2:, :]
    return hp, hs, sp, ss


def _dsa_layer(hp, hs, up, us, dims, layer, pool_k, pool_v, pool_ki, page_table, rel_bias,
               w_q, w_k, w_v, w_o, w_qi, w_ki, w_wi):
    Bp, T, Bs, Q, H, KVH, hd = dims
    G = H // KVH
    scale = hd ** -0.5
    page = pool_k.shape[2] // KVH
    DI = w_ki.shape[2]
    HI = w_wi.shape[2]
    idx_scale = (DI * HI) ** -0.5
    past = page_table.shape[1] * page
    wkw = jnp.concatenate([w_ki[layer], _pad_cols(w_wi[layer], LANE)], axis=1)[None]
    c_far = rel_bias[N_BUCKETS - 1]

    q_hm, qs = _mmw(up, w_q, layer, xs=us, emit_f32=False, emit_bf16=True, head_major=True, name="dsa_q")
    k32, k16, ks = _mmw(up, w_k, layer, xs=us, emit_bf16=True, name="dsa_k")
    v32, v16, vs = _mmw(up, w_v, layer, xs=us, emit_bf16=True, name="dsa_v")
    qi_hm, qis = _mmw(up, w_qi, layer, xs=us, emit_f32=False, emit_bf16=True, head_major=True, name="dsa_qi")
    kw, kws = _mmw(up, wkw, 0, xs=us, tn=DI + LANE, name="dsa_kiwi")

    tqs = min(512, T // 2)
    tq = min(256, tqs)
    assert tq >= MAX_DISTANCE and tqs % tq == 0 and T % tqs == 0
    ki32 = kw[:, :DI]
    wi = kw[:, DI:]
    ki16 = ki32.astype(BF16).reshape(Bp, T, DI)
    k16 = k16.reshape(Bp, T, KVH * hd)
    v16 = v16.reshape(Bp, T, KVH * hd)
    bt = MAX_DISTANCE
    ii = np.arange(bt)[:, None]
    jj = np.arange(bt)[None, :]
    t_diag = jnp.transpose(rel_bias[_t5_bucket_np(ii - jj)] - c_far, (2, 0, 1))
    t_left = jnp.transpose(rel_bias[_t5_bucket_np(bt + ii - jj)] - c_far, (2, 0, 1))
    k_top = min(TOPK_MAX, T // 4)
    o_blocks = []
    for c in range(T // tqs):
        madd = _dsa_select_prompt(qi_hm, ki16, wi, Bp, T, c, tqs, k_top, idx_scale)
        o_blocks.append(_dsa_attn_prompt(q_hm, k16, v16, madd, t_diag, t_left, Bp, T, c, tqs, tq, G, scale))
    o = jnp.concatenate(o_blocks, axis=1).reshape(Bp * T, H * hd)

    kis = kws[:, :DI]
    wis = kws[:, DI:DI + HI]
    qi_t = jnp.transpose(qis.reshape(Bs, Q, HI, DI), (0, 2, 1, 3)).reshape(Bs, HI * Q, DI).astype(BF16)
    wi_t = jnp.transpose(wis.reshape(Bs, Q, HI), (0, 2, 1)).reshape(Bs, HI * Q, 1)
    mp, mn = _dsa_select_sample(qi_t, wi_t, pool_ki, layer, _pad_page(kis.reshape(Bs, Q, DI), page), page_table,
                                Q, min(TOPK_MAX, (past + Q) // 4), idx_scale)
    nb, n = mp.shape[1], mp.shape[3]
    madd_past = jnp.transpose(mp, (0, 2, 1, 3)).reshape(Bs, Q, nb * n)
    qq = np.arange(Q)[:, None]
    cc = np.arange(page)[None, :]

    def sample_bias(dist):
        t = rel_bias[_t5_bucket_np(dist)] - c_far
        t = jnp.transpose(t.reshape(Q, page, KVH, G), (2, 3, 0, 1))
        return t.reshape(KVH * G * Q, page)

    bias_last = sample_bias(past + qq - (past - page + cc))
    bias_new = sample_bias(qq - cc)
    q_t = _to_heads(qs, Bs, Q, KVH, G, hd).astype(BF16)
    o_s = _paged_attn("dsa", q_t, pool_k, pool_v, layer,
                      _pad_page(ks.reshape(Bs, Q, KVH * hd), page), _pad_page(vs.reshape(Bs, Q, KVH * hd), page),
                      page_table, (madd_past, mn, bias_last, bias_new), scale)
    o_s = _to_rows(o_s, Bs, Q, KVH, G, hd).astype(BF16)

    hp, hs = _mmw(o, w_o, layer, xs=o_s, res=hp, res_s=hs, name="dsa_o")
    rows = (k32.reshape(Bp, T, KVH, hd), v32.reshape(Bp, T, KVH, hd), ki32.reshape(Bp, T, DI),
            ks.reshape(Bs, Q, KVH, hd), vs.reshape(Bs, Q, KVH, hd), kis.reshape(Bs, Q, DI))
    return hp, hs, rows


def _ffn_layer(hp, hs, vp, vs, dims, layer, state, w_up, w_conv, w_down16):
    Bp, T, Bs, Q = dims[:4]
    DFF = w_down16.shape[1]
    ap, stp, a_s, sts = _gconv(vp, vs, w_up, layer, (0, DFF), DFF, w_conv, _conv_prefix(state[layer], Q),
                               mode="ffn", seq_len=T, tail_len=Q, tm=1024, name="ffn_up")
    hp = _mm(ap, w_down16, layer, hp, name="ffn_down")
    hs = _mm(a_s, w_down16, layer, hs, name="ffn_down_s")
    nblk = stp.shape[0] // Bp
    cp = stp.reshape(Bp, nblk, SUBLANE, DFF)[:, -1, SUBLANE - 2:, :]
    cs = sts.reshape(Bs, Q, DFF)[:, Q - 2:, :]
    return hp, hs, cp, cs


def kernel(x_prompt, x_sample, cache_fox_k, cache_fox_v, cache_fox_logf, state_sconv, cache_dsa_k, cache_dsa_v,
           cache_dsa_kidx, state_ffn_conv, page_table, rel_bias, norm_mix, norm_ffn, norm_final,
           fox_w_q, fox_w_k, fox_w_v, fox_w_f, fox_b_f, fox_w_o, sc_w_in, sc_w_conv, sc_w_out,
           dsa_w_q, dsa_w_k, dsa_w_v, dsa_w_o, dsa_w_qi, dsa_w_ki, dsa_w_wi, ffn_w_up, ffn_w_conv, ffn_w_down):
    Bp, T, D = x_prompt.shape
    Bs, Q, _ = x_sample.shape
    depth = norm_mix.shape[0]
    KVH, hd = cache_fox_k.shape[3], cache_fox_k.shape[4]
    H = fox_w_f.shape[2]
    dims = (Bp, T, Bs, Q, H, KVH, hd)
    hp = x_prompt.reshape(Bp * T, D)
    hs = x_sample.reshape(Bs * Q, D)

    def key_head_rows(c):
        return c.reshape(c.shape[0], c.shape[1], c.shape[2] * KVH, hd)

    fox_k, fox_v, dsa_k, dsa_v = (key_head_rows(c) for c in (cache_fox_k, cache_fox_v, cache_dsa_k, cache_dsa_v))
    fox_lft = jnp.transpose(cache_fox_logf, (0, 1, 3, 2))
    w_down16 = ffn_w_down.astype(BF16)
    fox_new = ([], [], [], [], [], [])
    sc_new = ([], [])
    dsa_new = ([], [], [], [], [], [])
    ffn_new = ([], [])
    for i in range(depth):
        j, kind = i // 3, i % 3
        up = _rmsnorm(hp, norm_mix[i], BF16)
        us = _rmsnorm(hs, norm_mix[i], BF16)
        if kind == 0:
            hp, hs, rows = _fox_layer(hp, hs, up, us, dims, j, fox_k, fox_v, fox_lft, page_table,
                                      fox_w_q, fox_w_k, fox_w_v, fox_w_f, fox_b_f, fox_w_o)
            for lst, a in zip(fox_new, rows):
                lst.append(a)
        elif kind == 1:
            hp, hs, sp, ss = _sconv_layer(hp, hs, up, us, dims, j, state_sconv, sc_w_in, sc_w_conv, sc_w_out)
            sc_new[0].append(sp)
            sc_new[1].append(ss)
        else:
            hp, hs, rows = _dsa_layer(hp, hs, up, us, dims, j, dsa_k, dsa_v, cache_dsa_kidx, page_table, rel_bias,
                                      dsa_w_q, dsa_w_k, dsa_w_v, dsa_w_o, dsa_w_qi, dsa_w_ki, dsa_w_wi)
            for lst, a in zip(dsa_new, rows):
                lst.append(a)
        vp = _rmsnorm(hp, norm_ffn[i], BF16)
        vs = _rmsnorm(hs, norm_ffn[i], BF16)
        hp, hs, cp, cs = _ffn_layer(hp, hs, vp, vs, dims, i, state_ffn_conv, ffn_w_up, ffn_w_conv, w_down16)
        ffn_new[0].append(cp)
        ffn_new[1].append(cs)
    y_prompt = _rmsnorm(hp, norm_final, F32).reshape(Bp, T, D)
    y_sample = _rmsnorm(hs, norm_final, F32).reshape(Bs, Q, D)
    return (y_prompt, y_sample,
            jnp.stack(fox_new[0]), jnp.stack(fox_new[1]), jnp.stack(fox_new[2]),
            jnp.stack(fox_new[3]), jnp.stack(fox_new[4]), jnp.stack(fox_new[5]),
            jnp.stack(sc_new[0]), jnp.stack(sc_new[1]),
            jnp.stack(dsa_new[0]), jnp.stack(dsa_new[1]), jnp.stack(dsa_new[2]),
            jnp.stack(dsa_new[3]), jnp.stack(dsa_new[4]), jnp.stack(dsa_new[5]),
            jnp.stack(ffn_new[0]), jnp.stack(ffn_new[1]))
```

```python
import functools
import math

import numpy as np
import jax
import jax.numpy as jnp
from jax import lax
from jax.experimental import pallas as pl
from jax.experimental.pallas import tpu as pltpu

F32 = jnp.float32
BF16 = jnp.bfloat16
I32 = jnp.int32

RMS_EPS = 1e-6
NEG_INF = -1e30
TOPK_MAX = 256
N_BUCKETS = 32
MAX_DISTANCE = 128
CONV_WIDTH = 3
LOG2E = 1.4426950408889634

LANE = 128
SUBLANE = 8
VMEM_LIMIT_BYTES = 56 * 1024 * 1024
PAGES_PER_STEP = 8

_NT = (((1,), (1,)), ((), ()))


def _cparams(*sem):
    return pltpu.CompilerParams(dimension_semantics=sem, vmem_limit_bytes=VMEM_LIMIT_BYTES)


def _log_sigmoid(x):
    return jnp.minimum(x, 0.0) - jnp.log1p(jnp.exp(-jnp.abs(x)))


def _split3(x):
    hi = x.astype(BF16)
    r1 = x - hi.astype(F32)
    mid = r1.astype(BF16)
    lo = (r1 - mid.astype(F32)).astype(BF16)
    return hi, mid, lo


def _tri_cumsum(x):
    n = x.shape[-1]
    r = lax.broadcasted_iota(I32, (n, n), 0)
    c = lax.broadcasted_iota(I32, (n, n), 1)
    tri = jnp.where(r <= c, 1.0, 0.0).astype(BF16)
    hi, mid, lo = _split3(x)
    out = jnp.dot(hi, tri, preferred_element_type=F32)
    out = out + jnp.dot(mid, tri, preferred_element_type=F32)
    out = out + jnp.dot(lo, tri, preferred_element_type=F32)
    return out


def _rms_kernel(x_ref, g_ref, o_ref):
    x = x_ref[...]
    ms = jnp.mean(x * x, axis=-1, keepdims=True)
    o_ref[...] = ((x * lax.rsqrt(ms + RMS_EPS)) * g_ref[...]).astype(o_ref.dtype)


def _rmsnorm(h, g, out_dtype):
    M, D = h.shape
    tm = min(512, M)
    return pl.pallas_call(
        _rms_kernel,
        grid=(M // tm,),
        in_specs=[pl.BlockSpec((tm, D), lambda i: (i, 0)), pl.BlockSpec((1, D), lambda i: (0, 0))],
        out_specs=pl.BlockSpec((tm, D), lambda i: (i, 0)),
        out_shape=jax.ShapeDtypeStruct((M, D), out_dtype),
        compiler_params=_cparams("parallel"),
        name="rmsnorm",
    )(h, g.reshape(1, D).astype(F32))


def _mm_kernel(x_ref, w_ref, r_ref, o_ref):
    o_ref[...] = r_ref[...] + jnp.dot(x_ref[...], w_ref[...], preferred_element_type=F32)


def _mm(x, w, layer, res, *, tm=512, tn=512, name="mm"):
    M, K = x.shape
    N = w.shape[2]
    tm = min(tm, M)
    tn = min(tn, N)
    assert M % tm == 0 and N % tn == 0, (M, N, tm, tn)
    return pl.pallas_call(
        _mm_kernel,
        grid=(M // tm, N // tn),
        in_specs=[pl.BlockSpec((tm, K), lambda i, j: (i, 0)),
                  pl.BlockSpec((None, K, tn), lambda i, j: (layer, 0, j)),
                  pl.BlockSpec((tm, tn), lambda i, j: (i, j))],
        out_specs=pl.BlockSpec((tm, tn), lambda i, j: (i, j)),
        out_shape=jax.ShapeDtypeStruct((M, N), F32),
        compiler_params=_cparams("parallel", "parallel"),
        name=name,
    )(x, w, res)


def _mmw_kernel(*refs, has_res, has_ls, emit_f32, emit_bf16, head_major, has_tail):
    it = iter(refs)
    x_ref = next(it)
    w_ref = next(it)
    b_ref = next(it) if has_ls else None
    r_ref = next(it) if has_res else None
    xs_ref = next(it) if has_tail else None
    rs_ref = next(it) if has_tail and has_res else None
    o32_ref = next(it) if emit_f32 else None
    o16_ref = next(it) if emit_bf16 else None
    os_ref = next(it) if has_tail else None
    wbf_ref = next(it)

    def project(x, r):
        acc = jnp.dot(x, wbf_ref[...], preferred_element_type=F32)
        if has_ls:
            acc = _log_sigmoid(acc + b_ref[...])
        if r is not None:
            acc = r[...] + acc
        return acc

    @pl.when(pl.program_id(1) == 0)
    def _():
        wbf_ref[...] = w_ref[...].astype(BF16)
        if has_tail:
            os_ref[...] = project(xs_ref[...], rs_ref)

    acc = project(x_ref[...], r_ref)
    if emit_f32:
        o32_ref[...] = acc
    if emit_bf16:
        if head_major:
            for hh in range(o16_ref.shape[0]):
                o16_ref[hh] = acc[:, hh * LANE:(hh + 1) * LANE].astype(BF16)
        else:
            o16_ref[...] = acc.astype(BF16)


def _mmw(x, w, layer, *, xs=None, res=None, res_s=None, ls_bias=None, emit_f32=True, emit_bf16=False,
         head_major=False, tm=1024, tn=512, name="mmw"):
    M, K = x.shape
    N = w.shape[2]
    tm = min(tm, M)
    tn = min(tn, N)
    assert M % tm == 0 and N % tn == 0, (M, N, tm, tn)
    has_tail = xs is not None
    in_specs = [pl.BlockSpec((tm, K), lambda j, i: (i, 0)), pl.BlockSpec((None, K, tn), lambda j, i: (layer, 0, j))]
    args = [x, w]
    if ls_bias is not None:
        in_specs.append(pl.BlockSpec((1, tn), lambda j, i: (0, j)))
        args.append(ls_bias.reshape(1, N).astype(F32))
    if res is not None:
        in_specs.append(pl.BlockSpec((tm, tn), lambda j, i: (i, j)))
        args.append(res)
    if has_tail:
        Ms = xs.shape[0]
        in_specs.append(pl.BlockSpec((Ms, K), lambda j, i: (0, 0)))
        args.append(xs)
        if res is not None:
            in_specs.append(pl.BlockSpec((Ms, tn), lambda j, i: (0, j)))
            args.append(res_s)
    out_specs, out_shape = [], []
    if emit_f32:
        out_specs.append(pl.BlockSpec((tm, tn), lambda j, i: (i, j)))
        out_shape.append(jax.ShapeDtypeStruct((M, N), F32))
    if emit_bf16:
        if head_major:
            out_specs.append(pl.BlockSpec((tn // LANE, tm, LANE), lambda j, i: (j, i, 0)))
            out_shape.append(jax.ShapeDtypeStruct((N // LANE, M, LANE), BF16))
        else:
            out_specs.append(pl.BlockSpec((tm, tn), lambda j, i: (i, j)))
            out_shape.append(jax.ShapeDtypeStruct((M, N), BF16))
    if has_tail:
        out_specs.append(pl.BlockSpec((Ms, tn), lambda j, i: (0, j)))
        out_shape.append(jax.ShapeDtypeStruct((Ms, N), F32))
    outs = pl.pallas_call(
        functools.partial(_mmw_kernel, has_res=res is not None, has_ls=ls_bias is not None, emit_f32=emit_f32,
                          emit_bf16=emit_bf16, head_major=head_major, has_tail=has_tail),
        grid=(N // tn, M // tm),
        in_specs=in_specs,
        out_specs=out_specs,
        out_shape=out_shape,
        scratch_shapes=[pltpu.VMEM((K, tn), BF16)],
        compiler_params=_cparams("arbitrary", "arbitrary"),
        name=name,
    )(*args)
    return outs[0] if len(outs) == 1 else tuple(outs)


def _gconv_kernel(x_ref, *refs, mode, nw, tm, seq_blocks, tail_len, row_chunks, side_cast):
    w_refs = refs[:nw]
    rest = list(refs[nw:])
    cast_in_ref = rest.pop(4) if side_cast else None
    cast_out_ref = rest.pop(8) if side_cast else None
    wc_ref, xs_ref, p1_ref, p2_ref, o_ref, st_ref, os_ref, sts_ref, wbf_ref, buf_ref, carry_ref, bufs_ref = rest
    i = pl.program_id(1)
    ms = xs_ref.shape[0]

    def gated(x):
        tn = wc_ref.shape[1]
        y = jnp.dot(x, wbf_ref[...], preferred_element_type=F32)
        ys = [y[:, k * tn:(k + 1) * tn] for k in range(nw)]
        if mode == "ffn":
            return ys[0], ys[1]
        return ys[1] * ys[2], ys[0]

    def finish(cin, other, x1, x2):
        wc = wc_ref[...]
        y = wc[0:1, :] * x2 + wc[1:2, :] * x1 + wc[2:3, :] * cin
        if mode == "ffn":
            return ((y * (1.0 / (1.0 + jnp.exp(-y)))) * other).astype(BF16)
        return (other * y).astype(BF16)

    @pl.when(i == 0)
    def _():
        for k in range(nw):
            wbf_ref[:, k * wc_ref.shape[1]:(k + 1) * wc_ref.shape[1]] = w_refs[k][...].astype(BF16)
        if side_cast:
            cast_out_ref[...] = cast_in_ref[...].astype(BF16)
        cin, other = gated(xs_ref[...])
        bufs_ref[0:SUBLANE, :] = jnp.zeros((SUBLANE, cin.shape[1]), F32)
        bufs_ref[SUBLANE:SUBLANE + ms, :] = cin
        rmod = lax.broadcasted_iota(I32, (ms, 1), 0) % tail_len
        x1 = jnp.where(rmod >= 1, bufs_ref[SUBLANE - 1:SUBLANE - 1 + ms, :], p1_ref[...])
        x2 = jnp.where(rmod >= 2, bufs_ref[SUBLANE - 2:SUBLANE - 2 + ms, :], p2_ref[...])
        sts_ref[...] = cin
        os_ref[...] = finish(cin, other, x1, x2)

    first = (i % seq_blocks) == 0

    @pl.when(first)
    def _():
        buf_ref[0:SUBLANE, :] = jnp.zeros((SUBLANE, buf_ref.shape[1]), F32)

    @pl.when(jnp.logical_not(first))
    def _():
        buf_ref[0:SUBLANE, :] = carry_ref[...]

    rc = tm // row_chunks
    for c in range(row_chunks):
        lo = c * rc
        cin, other = gated(x_ref[lo:lo + rc, :])
        buf_ref[SUBLANE + lo:SUBLANE + lo + rc, :] = cin
        o_ref[lo:lo + rc, :] = finish(cin, other, buf_ref[SUBLANE - 1 + lo:SUBLANE - 1 + lo + rc, :],
                                      buf_ref[SUBLANE - 2 + lo:SUBLANE - 2 + lo + rc, :])
    tail = buf_ref[tm:tm + SUBLANE, :]
    carry_ref[...] = tail
    st_ref[...] = tail


def _gconv(x, xs, w, layer, col_offsets, nc, wconv, prefix, *, mode, seq_len, tail_len, tm, tn=256, cast_src=None,
           name="gconv"):
    M, K = x.shape
    Ms = xs.shape[0]
    tm = min(tm, seq_len)
    tn = min(tn, nc)
    assert M % tm == 0 and nc % tn == 0 and seq_len % tm == 0
    nw = len(col_offsets)
    in_specs = [pl.BlockSpec((tm, K), lambda j, i: (i, 0))]
    args = [x]
    for off in col_offsets:
        assert off % tn == 0
        in_specs.append(pl.BlockSpec((None, K, tn), functools.partial(lambda j, i, o: (layer, 0, o + j), o=off // tn)))
        args.append(w)
    in_specs.append(pl.BlockSpec((None, CONV_WIDTH, tn), lambda j, i: (layer, 0, j)))
    args.append(wconv)
    in_specs.append(pl.BlockSpec((Ms, K), lambda j, i: (0, 0)))
    args.append(xs)
    for p in prefix:
        in_specs.append(pl.BlockSpec((Ms, tn), lambda j, i: (0, j)))
        args.append(p)
    out_specs = [pl.BlockSpec((tm, tn), lambda j, i: (i, j)),
                 pl.BlockSpec((None, SUBLANE, tn), lambda j, i: (i, 0, j)),
                 pl.BlockSpec((Ms, tn), lambda j, i: (0, j)),
                 pl.BlockSpec((Ms, tn), lambda j, i: (0, j))]
    out_shape = [jax.ShapeDtypeStruct((M, nc), BF16), jax.ShapeDtypeStruct((M // tm, SUBLANE, nc), F32),
                 jax.ShapeDtypeStruct((Ms, nc), BF16), jax.ShapeDtypeStruct((Ms, nc), F32)]
    if cast_src is not None:
        _, R, C = cast_src.shape
        nj = nc // tn
        assert R % nj == 0 and (R // nj) % (2 * SUBLANE) == 0
        in_specs.append(pl.BlockSpec((None, R // nj, C), lambda j, i: (layer, j, 0)))
        args.append(cast_src)
        out_specs.append(pl.BlockSpec((None, R // nj, C), lambda j, i: (0, j, 0)))
        out_shape.append(jax.ShapeDtypeStruct((1, R, C), BF16))
    return pl.pallas_call(
        functools.partial(_gconv_kernel, mode=mode, nw=nw, tm=tm, seq_blocks=seq_len // tm, tail_len=tail_len,
                          row_chunks=max(tm // 512, 1), side_cast=cast_src is not None),
        grid=(nc // tn, M // tm),
        in_specs=in_specs,
        out_specs=out_specs,
        out_shape=out_shape,
        scratch_shapes=[pltpu.VMEM((K, nw * tn), BF16), pltpu.VMEM((SUBLANE + tm, tn), F32),
                        pltpu.VMEM((SUBLANE, tn), F32), pltpu.VMEM((SUBLANE + Ms, tn), F32)],
        compiler_params=_cparams("arbitrary", "arbitrary"),
        name=name,
    )(*args)


def _conv_prefix(state, seq_len):
    Bn, _, C = state.shape
    z = jnp.zeros((Bn, seq_len, C), F32)
    p1 = z.at[:, 0].set(state[:, 1])
    p2 = z.at[:, 0].set(state[:, 0]).at[:, 1].set(state[:, 1])
    return p1.reshape(Bn * seq_len, C), p2.reshape(Bn * seq_len, C)


def _cum_kernel(u_ref, wft_ref, bf_ref, o_ref, carry_ref):
    t = pl.program_id(1)

    @pl.when(t == 0)
    def _():
        carry_ref[...] = jnp.zeros_like(carry_ref)

    z = lax.dot_general(wft_ref[...], u_ref[...], _NT, preferred_element_type=F32) + bf_ref[...]
    c = _tri_cumsum(_log_sigmoid(z)) + carry_ref[...]
    o_ref[...] = c
    carry_ref[...] = c[:, c.shape[1] - 1:c.shape[1]]


def _fox_cum(u, wft, bf, Bn, T, tk):
    M, D = u.shape
    H = wft.shape[0]
    nt = T // tk
    return pl.pallas_call(
        _cum_kernel,
        grid=(Bn, nt),
        in_specs=[pl.BlockSpec((tk, D), lambda b, t: (b * nt + t, 0)),
                  pl.BlockSpec((H, D), lambda b, t: (0, 0)),
                  pl.BlockSpec((H, 1), lambda b, t: (0, 0))],
        out_specs=pl.BlockSpec((None, None, H, tk), lambda b, t: (b, t, 0, 0)),
        out_shape=jax.ShapeDtypeStruct((Bn, nt, H, tk), F32),
        scratch_shapes=[pltpu.VMEM((H, 1), F32)],
        compiler_params=_cparams("arbitrary", "arbitrary"),
        name="fox_cum",
    )(u, wft, bf.reshape(H, 1).astype(F32))


def _fox_attn_kernel(q_ref, k_ref, v_ref, cum_ref, o_ref, m_ref, l_ref, acc_ref, *, G, tq, tk, scale):
    qi = pl.program_id(2)
    hd = q_ref.shape[-1]
    reps = tk // LANE
    m_ref[...] = jnp.full(m_ref.shape, NEG_INF, F32)
    l_ref[...] = jnp.zeros(l_ref.shape, F32)
    acc_ref[...] = jnp.zeros(acc_ref.shape, F32)
    n_full = (qi * tq) // tk

    q = q_ref[...].reshape(G * tq, hd)

    def tile(j, masked):
        start = pl.multiple_of(j * tk, tk)
        kt = k_ref[pl.ds(start, tk), :]
        vt = v_ref[pl.ds(start, tk), :]
        s = lax.dot_general(q, kt, _NT, preferred_element_type=F32) * (scale * LOG2E)
        s = s.reshape(G, tq, tk) - cum_ref[j] * LOG2E
        if masked:
            row = lax.broadcasted_iota(I32, (tq, tk), 0) + qi * tq
            col = lax.broadcasted_iota(I32, (tq, tk), 1) + j * tk
            s = jnp.where((col <= row)[None], s, NEG_INF)
        s = s.reshape(G * tq, tk)
        m_prev = m_ref[...]
        m_new = jnp.maximum(m_prev, jnp.max(s, axis=1, keepdims=True))
        p = jnp.exp2(s - jnp.concatenate([m_new] * reps, axis=1))
        alpha = jnp.exp2(m_prev - m_new)
        l_ref[...] = alpha * l_ref[...] + jnp.sum(p, axis=1, keepdims=True)
        acc_ref[...] = alpha * acc_ref[...] + jnp.dot(p.astype(BF16), vt, preferred_element_type=F32)
        m_ref[...] = m_new

    def body(j, carry):
        tile(j, False)
        return carry

    lax.fori_loop(0, n_full, body, 0)
    tile(n_full, True)
    out = acc_ref[...] / l_ref[...]
    for g in range(G):
        o_ref[:, g * hd:(g + 1) * hd] = out[g * tq:(g + 1) * tq, :].astype(o_ref.dtype)


def _fox_attn_prompt(q_hm, k16, v16, cum, Bn, T, G, scale, tq, tk):
    H, M, hd = q_hm.shape
    assert hd == LANE and tk % tq == 0
    KVH = H // G
    nq = T // tq
    return pl.pallas_call(
        functools.partial(_fox_attn_kernel, G=G, tq=tq, tk=tk, scale=scale),
        grid=(Bn, KVH, nq),
        in_specs=[pl.BlockSpec((G, tq, hd), lambda b, h, i: (h, b * nq + i, 0)),
                  pl.BlockSpec((T, hd), lambda b, h, i: (b, h)),
                  pl.BlockSpec((T, hd), lambda b, h, i: (b, h)),
                  pl.BlockSpec((None, None, T // tk, G, 1, tk), lambda b, h, i: (b, h, 0, 0, 0, 0))],
        out_specs=pl.BlockSpec((tq, G * hd), lambda b, h, i: (b * nq + i, h)),
        out_shape=jax.ShapeDtypeStruct((M, H * hd), BF16),
        scratch_shapes=[pltpu.VMEM((G * tq, LANE), F32), pltpu.VMEM((G * tq, LANE), F32),
                        pltpu.VMEM((G * tq, hd), F32)],
        compiler_params=_cparams("parallel", "parallel", "arbitrary"),
        name="fox_attn_prompt",
    )(q_hm, k16, v16, cum)


def _paged_attn_kernel(*refs, kind, P, KVH, G, Q, page, hd, scale):
    pt_ref = refs[0]
    del pt_ref
    pos = 1
    q_ref = refs[pos]; pos += 1
    kp_refs = refs[pos:pos + P]; pos += P
    vp_refs = refs[pos:pos + P]; pos += P
    kn_ref, vn_ref = refs[pos], refs[pos + 1]; pos += 2
    if kind == "fox":
        lf_refs = refs[pos:pos + P]; pos += P
        lfn_ref = refs[pos]; pos += 1
    else:
        mp_ref, mn_ref, tl_ref, tn_ref = refs[pos:pos + 4]; pos += 4
    o_ref = refs[pos]; pos += 1
    kb_ref, vb_ref, m_ref, l_ref, acc_ref = refs[pos:pos + 5]; pos += 5
    if kind == "fox":
        cum_ref, carry_ref = refs[pos:pos + 2]

    jb = pl.program_id(1)
    nb = pl.num_programs(1)
    R = G * Q

    @pl.when(jb == 0)
    def _():
        m_ref[...] = jnp.full(m_ref.shape, NEG_INF, F32)
        l_ref[...] = jnp.zeros(l_ref.shape, F32)
        acc_ref[...] = jnp.zeros(acc_ref.shape, F32)
        if kind == "fox":
            carry_ref[...] = jnp.zeros(carry_ref.shape, F32)

    for p in range(P):
        for kvh in range(KVH):
            kb_ref[kvh, p * page:(p + 1) * page, :] = kp_refs[p][pl.ds(kvh, page, stride=KVH), :].astype(BF16)
            vb_ref[kvh, p * page:(p + 1) * page, :] = vp_refs[p][pl.ds(kvh, page, stride=KVH), :].astype(BF16)
        if kind == "fox":
            c = _tri_cumsum(lf_refs[p][...]) + carry_ref[...]
            cum_ref[:, p * page:(p + 1) * page] = c
            carry_ref[...] = c[:, page - 1:page]

    def logits(keys_of):
        return jnp.concatenate(
            [lax.dot_general(q_ref[kvh], keys_of(kvh), _NT, preferred_element_type=F32) for kvh in range(KVH)],
            axis=0) * (scale * LOG2E)

    def update(s, vals_of):
        m_prev = m_ref[...]
        m_new = jnp.maximum(m_prev, jnp.max(s, axis=1, keepdims=True))
        pr = jnp.exp2(s - jnp.concatenate([m_new] * (s.shape[1] // LANE), axis=1))
        alpha = jnp.exp2(m_prev - m_new)
        l_ref[...] = alpha * l_ref[...] + jnp.sum(pr, axis=1, keepdims=True)
        p16 = pr.astype(BF16)
        pv = jnp.concatenate(
            [jnp.dot(p16[kvh * R:(kvh + 1) * R, :], vals_of(kvh), preferred_element_type=F32) for kvh in range(KVH)],
            axis=0)
        acc_ref[...] = alpha * acc_ref[...] + pv
        m_ref[...] = m_new

    def head_rows(tile):
        return jnp.concatenate(
            [jnp.broadcast_to(tile[h:h + 1, :], (Q, tile.shape[1])) for h in range(KVH * G)], axis=0)

    def query_rows(tile):
        return jnp.concatenate([tile] * (KVH * G), axis=0)

    n = P * page
    s = logits(lambda kvh: kb_ref[kvh])
    if kind == "fox":
        s = s - head_rows(cum_ref[...]) * LOG2E
    else:
        s = s + query_rows(mp_ref[...])
        is_last = jnp.where(jb == nb - 1, LOG2E, 0.0)
        tail = s[:, n - page:] + is_last * tl_ref[...]
        s = tail if n == page else jnp.concatenate([s[:, :n - page], tail], axis=1)
    update(s, lambda kvh: vb_ref[kvh])

    @pl.when(jb == nb - 1)
    def _():
        kn = kn_ref[...].astype(BF16)
        vn = vn_ref[...].astype(BF16)
        qpos = lax.broadcasted_iota(I32, (KVH * R, page), 0) % Q
        kpos = lax.broadcasted_iota(I32, (KVH * R, page), 1)
        s = logits(lambda kvh: kn[:, kvh * hd:(kvh + 1) * hd])
        if kind == "fox":
            s = s - head_rows(_tri_cumsum(lfn_ref[...]) + carry_ref[...]) * LOG2E
        else:
            s = s + query_rows(mn_ref[...]) + tn_ref[...] * LOG2E
        s = jnp.where(kpos <= qpos, s, NEG_INF)
        update(s, lambda kvh: vn[:, kvh * hd:(kvh + 1) * hd])
        out = acc_ref[...] / l_ref[...]
        for kvh in range(KVH):
            o_ref[kvh] = out[kvh * R:(kvh + 1) * R, :]


def _paged_attn(kind, q, pool_k, pool_v, layer, k_new, v_new, page_table, extras, scale):
    Bn, KVH, R, hd = q.shape
    page = pool_k.shape[2] // KVH
    n_pages = page_table.shape[1]
    P = min(PAGES_PER_STEP, n_pages)
    assert n_pages % P == 0
    nb = n_pages // P
    H = None
    if kind == "fox":
        H = extras[0].shape[2]
        G = H // KVH
    else:
        G = extras[2].shape[0] // (KVH * extras[0].shape[1])
    Q = R // G

    def page_map(p, nd):
        return lambda b, j, pt: (layer, pt[b, j * P + p]) + (0,) * nd

    in_specs = [pl.BlockSpec((None, KVH, R, hd), lambda b, j, pt: (b, 0, 0, 0))]
    args = [q]
    for pool in (pool_k, pool_v):
        for p in range(P):
            in_specs.append(pl.BlockSpec((None, None, page * KVH, hd), page_map(p, 2)))
            args.append(pool)
    for new in (k_new, v_new):
        in_specs.append(pl.BlockSpec((None, page, KVH * hd), lambda b, j, pt: (b, 0, 0)))
        args.append(new)
    scratch = [pltpu.VMEM((KVH, P * page, hd), BF16), pltpu.VMEM((KVH, P * page, hd), BF16),
               pltpu.VMEM((KVH * R, LANE), F32), pltpu.VMEM((KVH * R, LANE), F32),
               pltpu.VMEM((KVH * R, hd), F32)]
    if kind == "fox":
        pool_lft, lf_newt = extras
        for p in range(P):
            in_specs.append(pl.BlockSpec((None, None, H, page), page_map(p, 2)))
            args.append(pool_lft)
        in_specs.append(pl.BlockSpec((None, H, page), lambda b, j, pt: (b, 0, 0)))
        args.append(lf_newt)
        scratch += [pltpu.VMEM((H, P * page), F32), pltpu.VMEM((H, 1), F32)]
    else:
        madd_past, madd_new, bias_last, bias_new = extras
        in_specs.append(pl.BlockSpec((None, Q, P * page), lambda b, j, pt: (b, 0, j)))
        args.append(madd_past)
        in_specs.append(pl.BlockSpec((None, Q, page), lambda b, j, pt: (b, 0, 0)))
        args.append(madd_new)
        for t in (bias_last, bias_new):
            in_specs.append(pl.BlockSpec((KVH * R, page), lambda b, j, pt: (0, 0)))
            args.append(t)
    grid_spec = pltpu.PrefetchScalarGridSpec(
        num_scalar_prefetch=1,
        grid=(Bn, nb),
        in_specs=in_specs,
        out_specs=pl.BlockSpec((None, KVH, R, hd), lambda b, j, pt: (b, 0, 0, 0)),
        scratch_shapes=scratch,
    )
    return pl.pallas_call(
        functools.partial(_paged_attn_kernel, kind=kind, P=P, KVH=KVH, G=G, Q=Q, page=page, hd=hd, scale=scale),
        grid_spec=grid_spec,
        out_shape=jax.ShapeDtypeStruct((Bn, KVH, R, hd), F32),
        compiler_params=_cparams("arbitrary", "arbitrary"),
        name=kind + "_attn_sample",
    )(page_table, *args)


def _sortable(s):
    bits = pltpu.bitcast(s + 0.0, I32)
    return jnp.where(bits >= 0, bits, bits ^ jnp.int32(0x7FFFFFFF))


def _kth_largest_key(count_ge, rows, k):
    sign = jnp.int32(-2 ** 31)

    def body(i, t):
        cand = t | jnp.left_shift(jnp.int32(1), 31 - i)
        cnt = count_ge(cand ^ sign)
        return jnp.where(cnt >= k, cand, t)

    t = lax.fori_loop(0, 32, body, jnp.zeros((rows, 1), I32))
    return t ^ sign


def _last_tie_index(ties_before, need, rows, n_keys):
    nbits = max(int(n_keys - 1).bit_length(), 1)

    def body(i, m):
        cand = m | jnp.left_shift(jnp.int32(1), nbits - 1 - i)
        return jnp.where(ties_before(cand) < need, cand, m)

    return lax.fori_loop(0, nbits, body, jnp.zeros((rows, 1), I32))


def _dsa_score_kernel(qi_ref, ki_ref, wi_ref, o_ref, acc_ref, wb_ref, *, HI, tq, T, q0, k_top, scale):
    wi = wi_ref[...]
    for h in range(HI):
        wb_ref[h] = jnp.broadcast_to(wi[:, h:h + 1], (tq, LANE))
    acc_ref[...] = jnp.zeros(acc_ref.shape, F32)
    ki = ki_ref[...]
    reps = T // LANE

    def body(h, carry):
        d = lax.dot_general(qi_ref[h], ki, _NT, preferred_element_type=F32)
        w = jnp.concatenate([wb_ref[h]] * reps, axis=1)
        acc_ref[...] += w * jnp.maximum(d, 0.0)
        return carry

    lax.fori_loop(0, HI, body, 0)
    qpos = lax.broadcasted_iota(I32, (tq, T), 0) + q0
    kpos = lax.broadcasted_iota(I32, (tq, T), 1)
    adm = kpos <= qpos
    key = _sortable(jnp.where(adm, acc_ref[...] * scale, NEG_INF))

    def count_ge(thr):
        return jnp.sum(jnp.where(key >= thr, 1, 0), axis=-1, keepdims=True)

    thr = _kth_largest_key(count_ge, tq, k_top)
    sel = jnp.logical_and(key >= thr, adm)
    o_ref[...] = jnp.where(sel, 0.0, NEG_INF)

    @pl.when(jnp.max(jnp.sum(jnp.where(sel, 1, 0), axis=-1, keepdims=True)) > k_top)
    def _():
        gt = key > thr
        tie = key == thr
        need = k_top - jnp.sum(jnp.where(gt, 1, 0), axis=-1, keepdims=True)

        def ties_before(m):
            return jnp.sum(jnp.where(jnp.logical_and(tie, kpos < m), 1, 0), axis=-1, keepdims=True)

        last = _last_tie_index(ties_before, need, tq, T)
        keep = jnp.logical_or(gt, jnp.logical_and(tie, kpos <= last))
        o_ref[...] = jnp.where(jnp.logical_and(keep, adm), 0.0, NEG_INF)


def _dsa_select_prompt(qi_hm, ki16, wi, Bn, T, c, tq, k_top, scale):
    HI, M, DI = qi_hm.shape
    nq = T // tq
    tk = (c + 1) * tq
    wl = wi.shape[1]
    return pl.pallas_call(
        functools.partial(_dsa_score_kernel, HI=HI, tq=tq, T=tk, q0=c * tq, k_top=k_top, scale=scale),
        grid=(Bn,),
        in_specs=[pl.BlockSpec((HI, tq, DI), lambda b: (0, b * nq + c, 0)),
                  pl.BlockSpec((None, tk, DI), lambda b: (b, 0, 0)),
                  pl.BlockSpec((tq, wl), lambda b: (b * nq + c, 0))],
        out_specs=pl.BlockSpec((None, tq, tk), lambda b: (b, 0, 0)),
        out_shape=jax.ShapeDtypeStruct((Bn, tq, tk), F32),
        scratch_shapes=[pltpu.VMEM((tq, tk), F32), pltpu.VMEM((HI, tq, LANE), F32)],
        compiler_params=_cparams("parallel"),
        name="dsa_select_prompt",
    )(qi_hm, ki16, wi)


def _dsa_attn_kernel(q_ref, k_ref, v_ref, madd_ref, td_ref, tl_ref, o_ref, s_ref, *, G, tq, iq0, scale):
    iq = iq0 + pl.program_id(1)
    hd = q_ref.shape[-1]
    k = k_ref[...]
    v = v_ref[...]
    madd = madd_ref[...]
    bt = td_ref.shape[-1]
    for g in range(G):
        s_ref[g] = lax.dot_general(q_ref[g], k, _NT, preferred_element_type=F32) * (scale * LOG2E) + madd
        for r in range(tq // bt):
            rows = slice(r * bt, (r + 1) * bt)
            a = iq * (tq // bt) + r
            s_ref[g, rows, pl.ds(pl.multiple_of(a * bt, bt), bt)] += td_ref[g] * LOG2E

            @pl.when(a > 0)
            def _():
                s_ref[g, rows, pl.ds(pl.multiple_of(jnp.maximum(a - 1, 0) * bt, bt), bt)] += tl_ref[g] * LOG2E

        s = s_ref[g]
        m = jnp.max(s, axis=-1, keepdims=True)
        p = jnp.exp2(s - m)
        l = jnp.sum(p, axis=-1, keepdims=True)
        o = jnp.dot(p.astype(BF16), v, preferred_element_type=F32) / l
        o_ref[:, g * hd:(g + 1) * hd] = o.astype(o_ref.dtype)


def _dsa_attn_prompt(q_hm, k16, v16, madd, t_diag, t_left, Bn, T, c, tqs, tq, G, scale):
    H, M, hd = q_hm.shape
    KVH = H // G
    tk = (c + 1) * tqs
    nsub = tqs // tq
    nq = T // tq
    return pl.pallas_call(
        functools.partial(_dsa_attn_kernel, G=G, tq=tq, iq0=c * nsub, scale=scale),
        grid=(Bn, nsub, KVH),
        in_specs=[pl.BlockSpec((G, tq, hd), lambda b, i, h: (h, b * nq + c * nsub + i, 0)),
                  pl.BlockSpec((None, tk, hd), lambda b, i, h: (b, 0, h)),
                  pl.BlockSpec((None, tk, hd), lambda b, i, h: (b, 0, h)),
                  pl.BlockSpec((None, tq, tk), lambda b, i, h: (b, i, 0)),
                  pl.BlockSpec((G,) + t_diag.shape[1:], lambda b, i, h: (h, 0, 0)),
                  pl.BlockSpec((G,) + t_left.shape[1:], lambda b, i, h: (h, 0, 0))],
        out_specs=pl.BlockSpec((None, tq, G * hd), lambda b, i, h: (b, i, h)),
        out_shape=jax.ShapeDtypeStruct((Bn, tqs, H * hd), BF16),
        scratch_shapes=[pltpu.VMEM((G, tq, tk), F32)],
        compiler_params=_cparams("parallel", "parallel", "arbitrary"),
        name="dsa_attn_prompt",
    )(q_hm, k16, v16, madd, t_diag, t_left)


def _dsa_score_sample_kernel(*refs, P, HI, Q, page, k_top, scale):
    pt_ref = refs[0]
    del pt_ref
    qi_ref, wi_ref = refs[1], refs[2]
    kp_refs = refs[3:3 + P]
    kn_ref = refs[3 + P]
    mp_ref, mn_ref = refs[4 + P], refs[5 + P]
    kb_ref, sp_ref, sn_ref = refs[6 + P:9 + P]

    jb = pl.program_id(1)
    nb = pl.num_programs(1)
    n = P * page
    qi = qi_ref[...]
    wcol = wi_ref[...]

    def score(keys16):
        d = lax.dot_general(qi, keys16, _NT, preferred_element_type=F32)
        d = jnp.broadcast_to(wcol, d.shape) * jnp.maximum(d, 0.0)
        tot = d[0:Q]
        for h in range(1, HI):
            tot = tot + d[h * Q:(h + 1) * Q]
        return tot * scale

    for p in range(P):
        kb_ref[p * page:(p + 1) * page, :] = kp_refs[p][...].astype(BF16)
    sp_ref[jb] = score(kb_ref[...])

    @pl.when(jb == nb - 1)
    def _():
        qpos = lax.broadcasted_iota(I32, (Q, page), 0)
        kpos = lax.broadcasted_iota(I32, (Q, page), 1)
        adm = kpos <= qpos
        sn_ref[...] = jnp.where(adm, score(kn_ref[...].astype(BF16)), NEG_INF)
        key_p = _sortable(sp_ref[...])
        key_n = _sortable(sn_ref[...])

        def count_ge(thr):
            cp = jnp.sum(jnp.where(key_p >= thr[None], 1, 0), axis=-1, keepdims=True)
            cn = jnp.sum(jnp.where(key_n >= thr, 1, 0), axis=-1, keepdims=True)
            return jnp.sum(cp, axis=0) + cn

        thr = _kth_largest_key(count_ge, Q, k_top)
        sel_n = jnp.logical_and(key_n >= thr, adm)
        mp_ref[...] = jnp.where(key_p >= thr[None], 0.0, NEG_INF)
        mn_ref[...] = jnp.where(sel_n, 0.0, NEG_INF)

        def count(mask_p, mask_n):
            cp = jnp.sum(jnp.where(mask_p, 1, 0), axis=-1, keepdims=True)
            return jnp.sum(cp, axis=0) + jnp.sum(jnp.where(mask_n, 1, 0), axis=-1, keepdims=True)

        @pl.when(jnp.max(count(key_p >= thr[None], sel_n)) > k_top)
        def _():
            idx_p = (lax.broadcasted_iota(I32, key_p.shape, 0) * n + lax.broadcasted_iota(I32, key_p.shape, 2))
            n_past = key_p.shape[0] * n
            idx_n = kpos + n_past
            gt_p, gt_n = key_p > thr[None], key_n > thr
            tie_p, tie_n = key_p == thr[None], key_n == thr
            need = k_top - count(gt_p, gt_n)

            def ties_before(m):
                return count(jnp.logical_and(tie_p, idx_p < m[None]), jnp.logical_and(tie_n, idx_n < m))

            last = _last_tie_index(ties_before, need, Q, n_past + page)
            keep_p = jnp.logical_or(gt_p, jnp.logical_and(tie_p, idx_p <= last[None]))
            keep_n = jnp.logical_or(gt_n, jnp.logical_and(tie_n, idx_n <= last))
            mp_ref[...] = jnp.where(keep_p, 0.0, NEG_INF)
            mn_ref[...] = jnp.where(jnp.logical_and(keep_n, adm), 0.0, NEG_INF)


def _dsa_select_sample(qi, wi, pool_ki, layer, ki_new, page_table, Q, k_top, scale):
    Bn, RQ, DI = qi.shape
    page = pool_ki.shape[2]
    n_pages = page_table.shape[1]
    P = min(PAGES_PER_STEP, n_pages)
    nb = n_pages // P
    HI = RQ // Q
    in_specs = [pl.BlockSpec((None, RQ, DI), lambda b, j, pt: (b, 0, 0)),
                pl.BlockSpec((None, RQ, 1), lambda b, j, pt: (b, 0, 0))]
    args = [qi, wi]
    for p in range(P):
        in_specs.append(pl.BlockSpec((None, None, page, DI), functools.partial(
            lambda b, j, pt, p: (layer, pt[b, j * P + p], 0, 0), p=p)))
        args.append(pool_ki)
    in_specs.append(pl.BlockSpec((None, page, DI), lambda b, j, pt: (b, 0, 0)))
    args.append(ki_new)
    grid_spec = pltpu.PrefetchScalarGridSpec(
        num_scalar_prefetch=1,
        grid=(Bn, nb),
        in_specs=in_specs,
        out_specs=[pl.BlockSpec((None, nb, Q, P * page), lambda b, j, pt: (b, 0, 0, 0)),
                   pl.BlockSpec((None, Q, page), lambda b, j, pt: (b, 0, 0))],
        scratch_shapes=[pltpu.VMEM((P * page, DI), BF16), pltpu.VMEM((nb, Q, P * page), F32),
                        pltpu.VMEM((Q, page), F32)],
    )
    return pl.pallas_call(
        functools.partial(_dsa_score_sample_kernel, P=P, HI=HI, Q=Q, page=page, k_top=k_top, scale=scale),
        grid_spec=grid_spec,
        out_shape=[jax.ShapeDtypeStruct((Bn, nb, Q, P * page), F32), jax.ShapeDtypeStruct((Bn, Q, page), F32)],
        compiler_params=_cparams("arbitrary", "arbitrary"),
        name="dsa_select_sample",
    )(page_table, *args)


def _t5_bucket_np(dist):
    max_exact = N_BUCKETS // 2
    d = np.maximum(dist, 0)
    ratio = np.log(np.maximum(d, 1).astype(np.float32) / np.float32(max_exact)) / np.float32(
        math.log(MAX_DISTANCE / max_exact))
    large = np.minimum(max_exact + (ratio * (N_BUCKETS - max_exact)).astype(np.int32), N_BUCKETS - 1)
    return np.where(d < max_exact, d, large).astype(np.int32)


def _pad_cols(w, n):
    return jnp.pad(w, ((0, 0), (0, n - w.shape[1])))


def _to_rows(o, Bn, Q, KVH, G, hd):
    o = o.reshape(Bn, KVH, G, Q, hd)
    return jnp.transpose(o, (0, 3, 1, 2, 4)).reshape(Bn * Q, KVH * G * hd)


def _to_heads(q, Bn, Q, KVH, G, hd):
    q = q.reshape(Bn, Q, KVH, G, hd)
    return jnp.transpose(q, (0, 2, 3, 1, 4)).reshape(Bn, KVH, G * Q, hd)


def _pad_page(x, page):
    return jnp.pad(x, ((0, 0), (0, page - x.shape[1]), (0, 0)))


def _fox_layer(hp, hs, up, us, dims, layer, pool_k, pool_v, pool_lft, page_table, w_q, w_k, w_v, w_f, b_f, w_o):
    Bp, T, Bs, Q, H, KVH, hd = dims
    G = H // KVH
    scale = hd ** -0.5
    page = pool_k.shape[2] // KVH
    wf_pad = _pad_cols(w_f[layer], LANE)[None]
    bf_pad = jnp.pad(b_f[layer], (0, LANE - H))
    wft = w_f[layer].T.astype(BF16)
    tq, tk = min(512, T // 2), min(512, T)

    q_hm, qs = _mmw(up, w_q, layer, xs=us, emit_f32=False, emit_bf16=True, head_major=True, name="fox_q")
    k32, k16, ks = _mmw(up, w_k, layer, xs=us, emit_bf16=True, name="fox_k")
    v32, v16, vs = _mmw(up, w_v, layer, xs=us, emit_bf16=True, name="fox_v")
    lf_pad, lfs_pad = _mmw(up, wf_pad, 0, xs=us, ls_bias=bf_pad, tn=LANE, name="fox_logf")
    lfs = lfs_pad[:, :H]

    cum = _fox_cum(up, wft, b_f[layer], Bp, T, tk)
    cum = jnp.transpose(cum.reshape(Bp, T // tk, KVH, G, 1, tk), (0, 2, 1, 3, 4, 5))
    o = _fox_attn_prompt(q_hm, k16, v16, cum, Bp, T, G, scale, tq, tk)

    q_t = _to_heads(qs, Bs, Q, KVH, G, hd).astype(BF16)
    lf_newt = jnp.transpose(_pad_page(lfs.reshape(Bs, Q, H), page), (0, 2, 1))
    o_s = _paged_attn("fox", q_t, pool_k, pool_v, layer,
                      _pad_page(ks.reshape(Bs, Q, KVH * hd), page), _pad_page(vs.reshape(Bs, Q, KVH * hd), page),
                      page_table, (pool_lft, lf_newt), scale)
    o_s = _to_rows(o_s, Bs, Q, KVH, G, hd).astype(BF16)

    hp, hs = _mmw(o, w_o, layer, xs=o_s, res=hp, res_s=hs, name="fox_o")
    rows = (k32.reshape(Bp, T, KVH, hd), v32.reshape(Bp, T, KVH, hd), lf_pad[:, :H].reshape(Bp, T, H),
            ks.reshape(Bs, Q, KVH, hd), vs.reshape(Bs, Q, KVH, hd), lfs.reshape(Bs, Q, H))
    return hp, hs, rows


def _sconv_layer(hp, hs, up, us, dims, layer, state, w_in, w_conv, w_out):
    Bp, T, Bs, Q = dims[:4]
    D = w_out.shape[1]
    zp, stp, zs, sts = _gconv(up, us, w_in, layer, (0, D, 2 * D), D, w_conv, _conv_prefix(state[layer], Q),
                              mode="sconv", seq_len=T, tail_len=Q, tm=512, name="sconv_in")
    hp, hs = _mmw(zp, w_out, layer, xs=zs, res=hp, res_s=hs, name="sconv_out")
    nblk = stp.shape[0] // Bp
    sp = stp.reshape(Bp, nblk, SUBLANE, D)[:, -1, SUBLANE - 2:, :]
    ss = sts.reshape(Bs, Q, D)[:, Q - 2:, :]
    return hp, hs, sp, ss


def _dsa_layer(hp, hs, up, us, dims, layer, pool_k, pool_v, pool_ki, page_table, rel_bias,
               w_q, w_k, w_v, w_o, w_qi, w_ki, w_wi):
    Bp, T, Bs, Q, H, KVH, hd = dims
    G = H // KVH
    scale = hd ** -0.5
    page = pool_k.shape[2] // KVH
    DI = w_ki.shape[2]
    HI = w_wi.shape[2]
    idx_scale = (DI * HI) ** -0.5
    past = page_table.shape[1] * page
    wkw = jnp.concatenate([w_ki[layer], _pad_cols(w_wi[layer], LANE)], axis=1)[None]
    c_far = rel_bias[N_BUCKETS - 1]

    q_hm, qs = _mmw(up, w_q, layer, xs=us, emit_f32=False, emit_bf16=True, head_major=True, name="dsa_q")
    k32, k16, ks = _mmw(up, w_k, layer, xs=us, emit_bf16=True, name="dsa_k")
    v32, v16, vs = _mmw(up, w_v, layer, xs=us, emit_bf16=True, name="dsa_v")
    qi_hm, qis = _mmw(up, w_qi, layer, xs=us, emit_f32=False, emit_bf16=True, head_major=True, name="dsa_qi")
    kw, kws = _mmw(up, wkw, 0, xs=us, tn=DI + LANE, name="dsa_kiwi")

    tqs = min(512, T // 2)
    tq = min(512, tqs)
    assert tq >= MAX_DISTANCE and tqs % tq == 0 and T % tqs == 0
    ki32 = kw[:, :DI]
    wi = kw[:, DI:]
    ki16 = ki32.astype(BF16).reshape(Bp, T, DI)
    k16 = k16.reshape(Bp, T, KVH * hd)
    v16 = v16.reshape(Bp, T, KVH * hd)
    bt = MAX_DISTANCE
    ii = np.arange(bt)[:, None]
    jj = np.arange(bt)[None, :]
    t_diag = jnp.transpose(rel_bias[_t5_bucket_np(ii - jj)] - c_far, (2, 0, 1))
    t_left = jnp.transpose(rel_bias[_t5_bucket_np(bt + ii - jj)] - c_far, (2, 0, 1))
    k_top = min(TOPK_MAX, T // 4)
    o_blocks = []
    for c in range(T // tqs):
        madd = _dsa_select_prompt(qi_hm, ki16, wi, Bp, T, c, tqs, k_top, idx_scale)
        o_blocks.append(_dsa_attn_prompt(q_hm, k16, v16, madd, t_diag, t_left, Bp, T, c, tqs, tq, G, scale))
    o = jnp.concatenate(o_blocks, axis=1).reshape(Bp * T, H * hd)

    kis = kws[:, :DI]
    wis = kws[:, DI:DI + HI]
    qi_t = jnp.transpose(qis.reshape(Bs, Q, HI, DI), (0, 2, 1, 3)).reshape(Bs, HI * Q, DI).astype(BF16)
    wi_t = jnp.transpose(wis.reshape(Bs, Q, HI), (0, 2, 1)).reshape(Bs, HI * Q, 1)
    mp, mn = _dsa_select_sample(qi_t, wi_t, pool_ki, layer, _pad_page(kis.reshape(Bs, Q, DI), page), page_table,
                                Q, min(TOPK_MAX, (past + Q) // 4), idx_scale)
    nb, n = mp.shape[1], mp.shape[3]
    madd_past = jnp.transpose(mp, (0, 2, 1, 3)).reshape(Bs, Q, nb * n)
    qq = np.arange(Q)[:, None]
    cc = np.arange(page)[None, :]

    def sample_bias(dist):
        t = rel_bias[_t5_bucket_np(dist)] - c_far
        t = jnp.transpose(t.reshape(Q, page, KVH, G), (2, 3, 0, 1))
        return t.reshape(KVH * G * Q, page)

    bias_last = sample_bias(past + qq - (past - page + cc))
    bias_new = sample_bias(qq - cc)
    q_t = _to_heads(qs, Bs, Q, KVH, G, hd).astype(BF16)
    o_s = _paged_attn("dsa", q_t, pool_k, pool_v, layer,
                      _pad_page(ks.reshape(Bs, Q, KVH * hd), page), _pad_page(vs.reshape(Bs, Q, KVH * hd), page),
                      page_table, (madd_past, mn, bias_last, bias_new), scale)
    o_s = _to_rows(o_s, Bs, Q, KVH, G, hd).astype(BF16)

    hp, hs = _mmw(o, w_o, layer, xs=o_s, res=hp, res_s=hs, name="dsa_o")
    rows = (k32.reshape(Bp, T, KVH, hd), v32.reshape(Bp, T, KVH, hd), ki32.reshape(Bp, T, DI),
            ks.reshape(Bs, Q, KVH, hd), vs.reshape(Bs, Q, KVH, hd), kis.reshape(Bs, Q, DI))
    return hp, hs, rows


def _ffn_layer(hp, hs, vp, vs, dims, layer, state, w_up, w_conv, w_down):
    Bp, T, Bs, Q = dims[:4]
    DFF = w_down.shape[1]
    ap, stp, a_s, sts, w_down16 = _gconv(vp, vs, w_up, layer, (0, DFF), DFF, w_conv, _conv_prefix(state[layer], Q),
                                         mode="ffn", seq_len=T, tail_len=Q, tm=1024, cast_src=w_down, name="ffn_up")
    hp = _mm(ap, w_down16, 0, hp, name="ffn_down")
    hs = _mm(a_s, w_down16, 0, hs, name="ffn_down_s")
    nblk = stp.shape[0] // Bp
    cp = stp.reshape(Bp, nblk, SUBLANE, DFF)[:, -1, SUBLANE - 2:, :]
    cs = sts.reshape(Bs, Q, DFF)[:, Q - 2:, :]
    return hp, hs, cp, cs


def kernel(x_prompt, x_sample, cache_fox_k, cache_fox_v, cache_fox_logf, state_sconv, cache_dsa_k, cache_dsa_v,
           cache_dsa_kidx, state_ffn_conv, page_table, rel_bias, norm_mix, norm_ffn, norm_final,
           fox_w_q, fox_w_k, fox_w_v, fox_w_f, fox_b_f, fox_w_o, sc_w_in, sc_w_conv, sc_w_out,
           dsa_w_q, dsa_w_k, dsa_w_v, dsa_w_o, dsa_w_qi, dsa_w_ki, dsa_w_wi, ffn_w_up, ffn_w_conv, ffn_w_down):
    Bp, T, D = x_prompt.shape
    Bs, Q, _ = x_sample.shape
    depth = norm_mix.shape[0]
    KVH, hd = cache_fox_k.shape[3], cache_fox_k.shape[4]
    H = fox_w_f.shape[2]
    dims = (Bp, T, Bs, Q, H, KVH, hd)
    hp = x_prompt.reshape(Bp * T, D)
    hs = x_sample.reshape(Bs * Q, D)

    def key_head_rows(c):
        return c.reshape(c.shape[0], c.shape[1], c.shape[2] * KVH, hd)

    fox_k, fox_v, dsa_k, dsa_v = (key_head_rows(c) for c in (cache_fox_k, cache_fox_v, cache_dsa_k, cache_dsa_v))
    fox_lft = jnp.transpose(cache_fox_logf, (0, 1, 3, 2))
    fox_new = ([], [], [], [], [], [])
    sc_new = ([], [])
    dsa_new = ([], [], [], [], [], [])
    ffn_new = ([], [])
    for i in range(depth):
        j, kind = i // 3, i % 3
        up = _rmsnorm(hp, norm_mix[i], BF16)
        us = _rmsnorm(hs, norm_mix[i], BF16)
        if kind == 0:
            hp, hs, rows = _fox_layer(hp, hs, up, us, dims, j, fox_k, fox_v, fox_lft, page_table,
                                      fox_w_q, fox_w_k, fox_w_v, fox_w_f, fox_b_f, fox_w_o)
            for lst, a in zip(fox_new, rows):
                lst.append(a)
        elif kind == 1:
            hp, hs, sp, ss = _sconv_layer(hp, hs, up, us, dims, j, state_sconv, sc_w_in, sc_w_conv, sc_w_out)
            sc_new[0].append(sp)
            sc_new[1].append(ss)
        else:
            hp, hs, rows = _dsa_layer(hp, hs, up, us, dims, j, dsa_k, dsa_v, cache_dsa_kidx, page_table, rel_bias,
                                      dsa_w_q, dsa_w_k, dsa_w_v, dsa_w_o, dsa_w_qi, dsa_w_ki, dsa_w_wi)
            for lst, a in zip(dsa_new, rows):
                lst.append(a)
        vp = _rmsnorm(hp, norm_ffn[i], BF16)
        vs = _rmsnorm(hs, norm_ffn[i], BF16)
        hp, hs, cp, cs = _ffn_layer(hp, hs, vp, vs, dims, i, state_ffn_conv, ffn_w_up, ffn_w_conv, ffn_w_down)
        ffn_new[0].append(cp)
        ffn_new[1].append(cs)
    y_prompt = _rmsnorm(hp, norm_final, F32).reshape(Bp, T, D)
    y_sample = _rmsnorm(hs, norm_final, F32).reshape(Bs, Q, D)
    return (y_prompt, y_sample,
            jnp.stack(fox_new[0]), jnp.stack(fox_new[1]), jnp.stack(fox_new[2]),
            jnp.stack(fox_new[3]), jnp.stack(fox_new[4]), jnp.stack(fox_new[5]),
            jnp.stack(sc_new[0]), jnp.stack(sc_new[1]),
            jnp.stack(dsa_new[0]), jnp.stack(dsa_new[1]), jnp.stack(dsa_new[2]),
            jnp.stack(dsa_new[3]), jnp.stack(dsa_new[4]), jnp.stack(dsa_new[5]),
            jnp.stack(ffn_new[0]), jnp.stack(ffn_new[1]))
```

```python
import functools
import math

import numpy as np
import jax
import jax.numpy as jnp
from jax import lax
from jax.experimental import pallas as pl
from jax.experimental.pallas import tpu as pltpu

F32 = jnp.float32
BF16 = jnp.bfloat16
I32 = jnp.int32

RMS_EPS = 1e-6
NEG_INF = -1e30
TOPK_MAX = 256
N_BUCKETS = 32
MAX_DISTANCE = 128
CONV_WIDTH = 3
LOG2E = 1.4426950408889634

LANE = 128
SUBLANE = 8
VMEM_LIMIT_BYTES = 56 * 1024 * 1024
PAGES_PER_STEP = 8

_NT = (((1,), (1,)), ((), ()))


def _cparams(*sem):
    return pltpu.CompilerParams(dimension_semantics=sem, vmem_limit_bytes=VMEM_LIMIT_BYTES)


def _log_sigmoid(x):
    return jnp.minimum(x, 0.0) - jnp.log1p(jnp.exp(-jnp.abs(x)))


def _split3(x):
    hi = x.astype(BF16)
    r1 = x - hi.astype(F32)
    mid = r1.astype(BF16)
    lo = (r1 - mid.astype(F32)).astype(BF16)
    return hi, mid, lo


def _tri_cumsum(x):
    n = x.shape[-1]
    r = lax.broadcasted_iota(I32, (n, n), 0)
    c = lax.broadcasted_iota(I32, (n, n), 1)
    tri = jnp.where(r <= c, 1.0, 0.0).astype(BF16)
    hi, mid, lo = _split3(x)
    out = jnp.dot(hi, tri, preferred_element_type=F32)
    out = out + jnp.dot(mid, tri, preferred_element_type=F32)
    out = out + jnp.dot(lo, tri, preferred_element_type=F32)
    return out


def _rms_kernel(x_ref, g_ref, o_ref):
    x = x_ref[...]
    ms = jnp.mean(x * x, axis=-1, keepdims=True)
    o_ref[...] = ((x * lax.rsqrt(ms + RMS_EPS)) * g_ref[...]).astype(o_ref.dtype)


def _rmsnorm(h, g, out_dtype):
    M, D = h.shape
    tm = min(512, M)
    return pl.pallas_call(
        _rms_kernel,
        grid=(M // tm,),
        in_specs=[pl.BlockSpec((tm, D), lambda i: (i, 0)), pl.BlockSpec((1, D), lambda i: (0, 0))],
        out_specs=pl.BlockSpec((tm, D), lambda i: (i, 0)),
        out_shape=jax.ShapeDtypeStruct((M, D), out_dtype),
        compiler_params=_cparams("parallel"),
        name="rmsnorm",
    )(h, g.reshape(1, D).astype(F32))


def _mm_kernel(x_ref, w_ref, r_ref, *refs, has_gain):
    out = r_ref[...] + jnp.dot(x_ref[...], w_ref[...], preferred_element_type=F32)
    if not has_gain:
        refs[0][...] = out
        return
    g_ref, o_ref, og_ref, oq_ref = refs
    o_ref[...] = out
    og_ref[...] = (out * g_ref[...]).astype(BF16)
    sq = jnp.sum(out * out, axis=1, keepdims=True)

    @pl.when(pl.program_id(1) == 0)
    def _():
        oq_ref[...] = sq

    @pl.when(pl.program_id(1) > 0)
    def _():
        oq_ref[...] += sq


def _mm(x, w, layer, res, *, next_gain=None, tm=512, tn=512, name="mm"):
    M, K = x.shape
    N = w.shape[2]
    tm = min(tm, M)
    tn = min(tn, N)
    assert M % tm == 0 and N % tn == 0, (M, N, tm, tn)
    in_specs = [pl.BlockSpec((tm, K), lambda i, j: (i, 0)),
                pl.BlockSpec((None, K, tn), lambda i, j: (layer, 0, j)),
                pl.BlockSpec((tm, tn), lambda i, j: (i, j))]
    args = [x, w, res]
    out_specs = [pl.BlockSpec((tm, tn), lambda i, j: (i, j))]
    out_shape = [jax.ShapeDtypeStruct((M, N), F32)]
    if next_gain is not None:
        in_specs.append(pl.BlockSpec((1, tn), lambda i, j: (0, j)))
        args.append(next_gain.reshape(1, N).astype(F32))
        out_specs += [pl.BlockSpec((tm, tn), lambda i, j: (i, j)), pl.BlockSpec((tm, 1), lambda i, j: (i, 0))]
        out_shape += [jax.ShapeDtypeStruct((M, N), BF16), jax.ShapeDtypeStruct((M, 1), F32)]
    outs = pl.pallas_call(
        functools.partial(_mm_kernel, has_gain=next_gain is not None),
        grid=(M // tm, N // tn),
        in_specs=in_specs,
        out_specs=out_specs,
        out_shape=out_shape,
        compiler_params=_cparams("parallel", "arbitrary"),
        name=name,
    )(*args)
    return outs[0] if len(outs) == 1 else tuple(outs)


def _mmw_kernel(*refs, has_res, has_ls, has_gain, has_scale, emit_f32, emit_bf16, head_major, has_tail):
    it = iter(refs)
    x_ref = next(it)
    w_ref = next(it)
    b_ref = next(it) if has_ls else None
    r_ref = next(it) if has_res else None
    g_ref = next(it) if has_gain else None
    sc_ref = next(it) if has_scale else None
    xs_ref = next(it) if has_tail else None
    rs_ref = next(it) if has_tail and has_res else None
    scs_ref = next(it) if has_tail and has_scale else None
    o32_ref = next(it) if emit_f32 else None
    o16_ref = next(it) if emit_bf16 else None
    og_ref, oq_ref = (next(it), next(it)) if has_gain else (None, None)
    os_ref = next(it) if has_tail else None
    osg_ref, osq_ref = (next(it), next(it)) if has_tail and has_gain else (None, None)
    wbf_ref = next(it)

    def project(x, r, sc):
        acc = jnp.dot(x, wbf_ref[...], preferred_element_type=F32)
        if sc is not None:
            acc = acc * sc[...]
        if has_ls:
            acc = _log_sigmoid(acc + b_ref[...])
        if r is not None:
            acc = r[...] + acc
        return acc

    def next_norm(acc, og, oq):
        og[...] = (acc * g_ref[...]).astype(BF16)
        oq[...] = jnp.sum(acc * acc, axis=1, keepdims=True)

    @pl.when(pl.program_id(1) == 0)
    def _():
        wbf_ref[...] = w_ref[...].astype(BF16)
        if has_tail:
            tail = project(xs_ref[...], rs_ref, scs_ref)
            os_ref[...] = tail
            if has_gain:
                next_norm(tail, osg_ref, osq_ref)

    acc = project(x_ref[...], r_ref, sc_ref)
    if has_gain:
        next_norm(acc, og_ref, oq_ref)
    if emit_f32:
        o32_ref[...] = acc
    if emit_bf16:
        if head_major:
            for hh in range(o16_ref.shape[0]):
                o16_ref[hh] = acc[:, hh * LANE:(hh + 1) * LANE].astype(BF16)
        else:
            o16_ref[...] = acc.astype(BF16)


def _mmw(x, w, layer, *, xs=None, res=None, res_s=None, ls_bias=None, scale=None, next_gain=None, emit_f32=True,
         emit_bf16=False, head_major=False, tm=1024, tn=512, name="mmw"):
    M, K = x.shape
    N = w.shape[2]
    tm = min(tm, M)
    tn = min(tn, N)
    assert M % tm == 0 and N % tn == 0, (M, N, tm, tn)
    nj = N // tn
    has_tail = xs is not None
    in_specs = [pl.BlockSpec((tm, K), lambda j, i: (i, 0)), pl.BlockSpec((None, K, tn), lambda j, i: (layer, 0, j))]
    args = [x, w]
    if ls_bias is not None:
        in_specs.append(pl.BlockSpec((1, tn), lambda j, i: (0, j)))
        args.append(ls_bias.reshape(1, N).astype(F32))
    if res is not None:
        in_specs.append(pl.BlockSpec((tm, tn), lambda j, i: (i, j)))
        args.append(res)
    if next_gain is not None:
        in_specs.append(pl.BlockSpec((1, tn), lambda j, i: (0, j)))
        args.append(next_gain.reshape(1, N).astype(F32))
    if scale is not None:
        in_specs.append(pl.BlockSpec((tm, 1), lambda j, i: (i, 0)))
        args.append(scale[0])
    if has_tail:
        Ms = xs.shape[0]
        in_specs.append(pl.BlockSpec((Ms, K), lambda j, i: (0, 0)))
        args.append(xs)
        if res is not None:
            in_specs.append(pl.BlockSpec((Ms, tn), lambda j, i: (0, j)))
            args.append(res_s)
        if scale is not None:
            in_specs.append(pl.BlockSpec((Ms, 1), lambda j, i: (0, 0)))
            args.append(scale[1])
    out_specs, out_shape = [], []
    if emit_f32:
        out_specs.append(pl.BlockSpec((tm, tn), lambda j, i: (i, j)))
        out_shape.append(jax.ShapeDtypeStruct((M, N), F32))
    if emit_bf16:
        if head_major:
            out_specs.append(pl.BlockSpec((tn // LANE, tm, LANE), lambda j, i: (j, i, 0)))
            out_shape.append(jax.ShapeDtypeStruct((N // LANE, M, LANE), BF16))
        else:
            out_specs.append(pl.BlockSpec((tm, tn), lambda j, i: (i, j)))
            out_shape.append(jax.ShapeDtypeStruct((M, N), BF16))
    if next_gain is not None:
        out_specs += [pl.BlockSpec((tm, tn), lambda j, i: (i, j)), pl.BlockSpec((None, tm, 1), lambda j, i: (j, i, 0))]
        out_shape += [jax.ShapeDtypeStruct((M, N), BF16), jax.ShapeDtypeStruct((nj, M, 1), F32)]
    if has_tail:
        out_specs.append(pl.BlockSpec((Ms, tn), lambda j, i: (0, j)))
        out_shape.append(jax.ShapeDtypeStruct((Ms, N), F32))
        if next_gain is not None:
            out_specs += [pl.BlockSpec((Ms, tn), lambda j, i: (0, j)),
                          pl.BlockSpec((None, Ms, 1), lambda j, i: (j, 0, 0))]
            out_shape += [jax.ShapeDtypeStruct((Ms, N), BF16), jax.ShapeDtypeStruct((nj, Ms, 1), F32)]
    outs = pl.pallas_call(
        functools.partial(_mmw_kernel, has_res=res is not None, has_ls=ls_bias is not None,
                          has_gain=next_gain is not None, has_scale=scale is not None, emit_f32=emit_f32,
                          emit_bf16=emit_bf16, head_major=head_major, has_tail=has_tail),
        grid=(N // tn, M // tm),
        in_specs=in_specs,
        out_specs=out_specs,
        out_shape=out_shape,
        scratch_shapes=[pltpu.VMEM((K, tn), BF16)],
        compiler_params=_cparams("arbitrary", "arbitrary"),
        name=name,
    )(*args)
    return outs[0] if len(outs) == 1 else tuple(outs)


def _gconv_kernel(x_ref, *refs, mode, nw, tm, seq_blocks, tail_len, row_chunks):
    w_refs = refs[:nw]
    (wc_ref, xs_ref, p1_ref, p2_ref, sc_ref, scs_ref, o_ref, st_ref, os_ref, sts_ref,
     wbf_ref, buf_ref, carry_ref, bufs_ref) = refs[nw:]
    i = pl.program_id(1)
    ms = xs_ref.shape[0]

    def gated(x, sc):
        tn = wc_ref.shape[1]
        y = jnp.dot(x, wbf_ref[...], preferred_element_type=F32) * sc
        ys = [y[:, k * tn:(k + 1) * tn] for k in range(nw)]
        if mode == "ffn":
            return ys[0], ys[1]
        return ys[1] * ys[2], ys[0]

    def finish(cin, other, x1, x2):
        wc = wc_ref[...]
        y = wc[0:1, :] * x2 + wc[1:2, :] * x1 + wc[2:3, :] * cin
        if mode == "ffn":
            return ((y * (1.0 / (1.0 + jnp.exp(-y)))) * other).astype(BF16)
        return (other * y).astype(BF16)

    @pl.when(i == 0)
    def _():
        for k in range(nw):
            wbf_ref[:, k * wc_ref.shape[1]:(k + 1) * wc_ref.shape[1]] = w_refs[k][...].astype(BF16)
        cin, other = gated(xs_ref[...], scs_ref[...])
        bufs_ref[0:SUBLANE, :] = jnp.zeros((SUBLANE, cin.shape[1]), F32)
        bufs_ref[SUBLANE:SUBLANE + ms, :] = cin
        rmod = lax.broadcasted_iota(I32, (ms, 1), 0) % tail_len
        x1 = jnp.where(rmod >= 1, bufs_ref[SUBLANE - 1:SUBLANE - 1 + ms, :], p1_ref[...])
        x2 = jnp.where(rmod >= 2, bufs_ref[SUBLANE - 2:SUBLANE - 2 + ms, :], p2_ref[...])
        sts_ref[...] = cin
        os_ref[...] = finish(cin, other, x1, x2)

    first = (i % seq_blocks) == 0

    @pl.when(first)
    def _():
        buf_ref[0:SUBLANE, :] = jnp.zeros((SUBLANE, buf_ref.shape[1]), F32)

    @pl.when(jnp.logical_not(first))
    def _():
        buf_ref[0:SUBLANE, :] = carry_ref[...]

    rc = tm // row_chunks
    for c in range(row_chunks):
        lo = c * rc
        cin, other = gated(x_ref[lo:lo + rc, :], sc_ref[lo:lo + rc, :])
        buf_ref[SUBLANE + lo:SUBLANE + lo + rc, :] = cin
        o_ref[lo:lo + rc, :] = finish(cin, other, buf_ref[SUBLANE - 1 + lo:SUBLANE - 1 + lo + rc, :],
                                      buf_ref[SUBLANE - 2 + lo:SUBLANE - 2 + lo + rc, :])
    tail = buf_ref[tm:tm + SUBLANE, :]
    carry_ref[...] = tail
    st_ref[...] = tail


def _gconv(x, xs, scale, w, layer, col_offsets, nc, wconv, prefix, *, mode, seq_len, tail_len, tm, tn=256,
           name="gconv"):
    M, K = x.shape
    Ms = xs.shape[0]
    tm = min(tm, seq_len)
    tn = min(tn, nc)
    assert M % tm == 0 and nc % tn == 0 and seq_len % tm == 0
    nw = len(col_offsets)
    in_specs = [pl.BlockSpec((tm, K), lambda j, i: (i, 0))]
    args = [x]
    for off in col_offsets:
        assert off % tn == 0
        in_specs.append(pl.BlockSpec((None, K, tn), functools.partial(lambda j, i, o: (layer, 0, o + j), o=off // tn)))
        args.append(w)
    in_specs.append(pl.BlockSpec((None, CONV_WIDTH, tn), lambda j, i: (layer, 0, j)))
    args.append(wconv)
    in_specs.append(pl.BlockSpec((Ms, K), lambda j, i: (0, 0)))
    args.append(xs)
    for p in prefix:
        in_specs.append(pl.BlockSpec((Ms, tn), lambda j, i: (0, j)))
        args.append(p)
    in_specs += [pl.BlockSpec((tm, 1), lambda j, i: (i, 0)), pl.BlockSpec((Ms, 1), lambda j, i: (0, 0))]
    args += [scale[0], scale[1]]
    out_specs = [pl.BlockSpec((tm, tn), lambda j, i: (i, j)),
                 pl.BlockSpec((None, SUBLANE, tn), lambda j, i: (i, 0, j)),
                 pl.BlockSpec((Ms, tn), lambda j, i: (0, j)),
                 pl.BlockSpec((Ms, tn), lambda j, i: (0, j))]
    out_shape = [jax.ShapeDtypeStruct((M, nc), BF16), jax.ShapeDtypeStruct((M // tm, SUBLANE, nc), F32),
                 jax.ShapeDtypeStruct((Ms, nc), BF16), jax.ShapeDtypeStruct((Ms, nc), F32)]
    return pl.pallas_call(
        functools.partial(_gconv_kernel, mode=mode, nw=nw, tm=tm, seq_blocks=seq_len // tm, tail_len=tail_len,
                          row_chunks=max(tm // 512, 1)),
        grid=(nc // tn, M // tm),
        in_specs=in_specs,
        out_specs=out_specs,
        out_shape=out_shape,
        scratch_shapes=[pltpu.VMEM((K, nw * tn), BF16), pltpu.VMEM((SUBLANE + tm, tn), F32),
                        pltpu.VMEM((SUBLANE, tn), F32), pltpu.VMEM((SUBLANE + Ms, tn), F32)],
        compiler_params=_cparams("arbitrary", "arbitrary"),
        name=name,
    )(*args)


def _conv_prefix(state, seq_len):
    Bn, _, C = state.shape
    z = jnp.zeros((Bn, seq_len, C), F32)
    p1 = z.at[:, 0].set(state[:, 1])
    p2 = z.at[:, 0].set(state[:, 0]).at[:, 1].set(state[:, 1])
    return p1.reshape(Bn * seq_len, C), p2.reshape(Bn * seq_len, C)


def _cum_kernel(u_ref, sc_ref, wft_ref, bf_ref, o_ref, carry_ref):
    t = pl.program_id(1)

    @pl.when(t == 0)
    def _():
        carry_ref[...] = jnp.zeros_like(carry_ref)

    z = lax.dot_general(wft_ref[...], u_ref[...], _NT, preferred_element_type=F32) * sc_ref[...] + bf_ref[...]
    c = _tri_cumsum(_log_sigmoid(z)) + carry_ref[...]
    o_ref[...] = c
    carry_ref[...] = c[:, c.shape[1] - 1:c.shape[1]]


def _fox_cum(u, scale, wft, bf, Bn, T, tk):
    M, D = u.shape
    H = wft.shape[0]
    nt = T // tk
    return pl.pallas_call(
        _cum_kernel,
        grid=(Bn, nt),
        in_specs=[pl.BlockSpec((tk, D), lambda b, t: (b * nt + t, 0)),
                  pl.BlockSpec((None, 1, tk), lambda b, t: (b * nt + t, 0, 0)),
                  pl.BlockSpec((H, D), lambda b, t: (0, 0)),
                  pl.BlockSpec((H, 1), lambda b, t: (0, 0))],
        out_specs=pl.BlockSpec((None, None, H, tk), lambda b, t: (b, t, 0, 0)),
        out_shape=jax.ShapeDtypeStruct((Bn, nt, H, tk), F32),
        scratch_shapes=[pltpu.VMEM((H, 1), F32)],
        compiler_params=_cparams("arbitrary", "arbitrary"),
        name="fox_cum",
    )(u, scale.reshape(Bn * nt, 1, tk), wft, bf.reshape(H, 1).astype(F32))


def _fox_attn_kernel(q_ref, k_ref, v_ref, cum_ref, o_ref, m_ref, l_ref, acc_ref, *, G, tq, tk, scale):
    qi = pl.program_id(2)
    hd = q_ref.shape[-1]
    reps = tk // LANE
    m_ref[...] = jnp.full(m_ref.shape, NEG_INF, F32)
    l_ref[...] = jnp.zeros(l_ref.shape, F32)
    acc_ref[...] = jnp.zeros(acc_ref.shape, F32)
    n_full = (qi * tq) // tk

    q = q_ref[...].reshape(G * tq, hd)

    def tile(j, masked):
        start = pl.multiple_of(j * tk, tk)
        kt = k_ref[pl.ds(start, tk), :]
        vt = v_ref[pl.ds(start, tk), :]
        s = lax.dot_general(q, kt, _NT, preferred_element_type=F32) * (scale * LOG2E)
        s = s.reshape(G, tq, tk) - cum_ref[j] * LOG2E
        if masked:
            row = lax.broadcasted_iota(I32, (tq, tk), 0) + qi * tq
            col = lax.broadcasted_iota(I32, (tq, tk), 1) + j * tk
            s = jnp.where((col <= row)[None], s, NEG_INF)
        s = s.reshape(G * tq, tk)
        m_prev = m_ref[...]
        m_new = jnp.maximum(m_prev, jnp.max(s, axis=1, keepdims=True))
        p = jnp.exp2(s - jnp.concatenate([m_new] * reps, axis=1))
        alpha = jnp.exp2(m_prev - m_new)
        l_ref[...] = alpha * l_ref[...] + jnp.sum(p, axis=1, keepdims=True)
        acc_ref[...] = alpha * acc_ref[...] + jnp.dot(p.astype(BF16), vt, preferred_element_type=F32)
        m_ref[...] = m_new

    def body(j, carry):
        tile(j, False)
        return carry

    lax.fori_loop(0, n_full, body, 0)
    tile(n_full, True)
    out = acc_ref[...] / l_ref[...]
    for g in range(G):
        o_ref[:, g * hd:(g + 1) * hd] = out[g * tq:(g + 1) * tq, :].astype(o_ref.dtype)


def _fox_attn_prompt(q_hm, k16, v16, cum, Bn, T, G, scale, tq, tk):
    H, M, hd = q_hm.shape
    assert hd == LANE and tk % tq == 0
    KVH = H // G
    nq = T // tq
    return pl.pallas_call(
        functools.partial(_fox_attn_kernel, G=G, tq=tq, tk=tk, scale=scale),
        grid=(Bn, KVH, nq),
        in_specs=[pl.BlockSpec((G, tq, hd), lambda b, h, i: (h, b * nq + i, 0)),
                  pl.BlockSpec((T, hd), lambda b, h, i: (b, h)),
                  pl.BlockSpec((T, hd), lambda b, h, i: (b, h)),
                  pl.BlockSpec((None, None, T // tk, G, 1, tk), lambda b, h, i: (b, h, 0, 0, 0, 0))],
        out_specs=pl.BlockSpec((tq, G * hd), lambda b, h, i: (b * nq + i, h)),
        out_shape=jax.ShapeDtypeStruct((M, H * hd), BF16),
        scratch_shapes=[pltpu.VMEM((G * tq, LANE), F32), pltpu.VMEM((G * tq, LANE), F32),
                        pltpu.VMEM((G * tq, hd), F32)],
        compiler_params=_cparams("parallel", "parallel", "arbitrary"),
        name="fox_attn_prompt",
    )(q_hm, k16, v16, cum)


def _paged_attn_kernel(*refs, kind, P, KVH, G, Q, page, hd, scale):
    pt_ref = refs[0]
    del pt_ref
    pos = 1
    q_ref = refs[pos]; pos += 1
    kp_refs = refs[pos:pos + P]; pos += P
    vp_refs = refs[pos:pos + P]; pos += P
    kn_ref, vn_ref = refs[pos], refs[pos + 1]; pos += 2
    if kind == "fox":
        lf_refs = refs[pos:pos + P]; pos += P
        lfn_ref = refs[pos]; pos += 1
    else:
        mp_ref, mn_ref, tl_ref, tn_ref = refs[pos:pos + 4]; pos += 4
    o_ref = refs[pos]; pos += 1
    kb_ref, vb_ref, m_ref, l_ref, acc_ref = refs[pos:pos + 5]; pos += 5
    if kind == "fox":
        cum_ref, carry_ref = refs[pos:pos + 2]

    jb = pl.program_id(1)
    nb = pl.num_programs(1)
    R = G * Q

    @pl.when(jb == 0)
    def _():
        m_ref[...] = jnp.full(m_ref.shape, NEG_INF, F32)
        l_ref[...] = jnp.zeros(l_ref.shape, F32)
        acc_ref[...] = jnp.zeros(acc_ref.shape, F32)
        if kind == "fox":
            carry_ref[...] = jnp.zeros(carry_ref.shape, F32)

    for p in range(P):
        for kvh in range(KVH):
            kb_ref[kvh, p * page:(p + 1) * page, :] = kp_refs[p][pl.ds(kvh, page, stride=KVH), :].astype(BF16)
            vb_ref[kvh, p * page:(p + 1) * page, :] = vp_refs[p][pl.ds(kvh, page, stride=KVH), :].astype(BF16)
        if kind == "fox":
            c = _tri_cumsum(lf_refs[p][...]) + carry_ref[...]
            cum_ref[:, p * page:(p + 1) * page] = c
            carry_ref[...] = c[:, page - 1:page]

    def logits(keys_of):
        return jnp.concatenate(
            [lax.dot_general(q_ref[kvh], keys_of(kvh), _NT, preferred_element_type=F32) for kvh in range(KVH)],
            axis=0) * (scale * LOG2E)

    def update(s, vals_of):
        m_prev = m_ref[...]
        m_new = jnp.maximum(m_prev, jnp.max(s, axis=1, keepdims=True))
        pr = jnp.exp2(s - jnp.concatenate([m_new] * (s.shape[1] // LANE), axis=1))
        alpha = jnp.exp2(m_prev - m_new)
        l_ref[...] = alpha * l_ref[...] + jnp.sum(pr, axis=1, keepdims=True)
        p16 = pr.astype(BF16)
        pv = jnp.concatenate(
            [jnp.dot(p16[kvh * R:(kvh + 1) * R, :], vals_of(kvh), preferred_element_type=F32) for kvh in range(KVH)],
            axis=0)
        acc_ref[...] = alpha * acc_ref[...] + pv
        m_ref[...] = m_new

    def head_rows(tile):
        return jnp.concatenate(
            [jnp.broadcast_to(tile[h:h + 1, :], (Q, tile.shape[1])) for h in range(KVH * G)], axis=0)

    def query_rows(tile):
        return jnp.concatenate([tile] * (KVH * G), axis=0)

    n = P * page
    s = logits(lambda kvh: kb_ref[kvh])
    if kind == "fox":
        s = s - head_rows(cum_ref[...]) * LOG2E
    else:
        s = s + query_rows(mp_ref[...])
        is_last = jnp.where(jb == nb - 1, LOG2E, 0.0)
        tail = s[:, n - page:] + is_last * tl_ref[...]
        s = tail if n == page else jnp.concatenate([s[:, :n - page], tail], axis=1)
    update(s, lambda kvh: vb_ref[kvh])

    @pl.when(jb == nb - 1)
    def _():
        kn = kn_ref[...].astype(BF16)
        vn = vn_ref[...].astype(BF16)
        qpos = lax.broadcasted_iota(I32, (KVH * R, page), 0) % Q
        kpos = lax.broadcasted_iota(I32, (KVH * R, page), 1)
        s = logits(lambda kvh: kn[:, kvh * hd:(kvh + 1) * hd])
        if kind == "fox":
            s = s - head_rows(_tri_cumsum(lfn_ref[...]) + carry_ref[...]) * LOG2E
        else:
            s = s + query_rows(mn_ref[...]) + tn_ref[...] * LOG2E
        s = jnp.where(kpos <= qpos, s, NEG_INF)
        update(s, lambda kvh: vn[:, kvh * hd:(kvh + 1) * hd])
        out = acc_ref[...] / l_ref[...]
        for kvh in range(KVH):
            o_ref[kvh] = out[kvh * R:(kvh + 1) * R, :]


def _paged_attn(kind, q, pool_k, pool_v, layer, k_new, v_new, page_table, extras, scale):
    Bn, KVH, R, hd = q.shape
    page = pool_k.shape[2] // KVH
    n_pages = page_table.shape[1]
    P = min(PAGES_PER_STEP, n_pages)
    assert n_pages % P == 0
    nb = n_pages // P
    H = None
    if kind == "fox":
        H = extras[0].shape[2]
        G = H // KVH
    else:
        G = extras[2].shape[0] // (KVH * extras[0].shape[1])
    Q = R // G

    def page_map(p, nd):
        return lambda b, j, pt: (layer, pt[b, j * P + p]) + (0,) * nd

    in_specs = [pl.BlockSpec((None, KVH, R, hd), lambda b, j, pt: (b, 0, 0, 0))]
    args = [q]
    for pool in (pool_k, pool_v):
        for p in range(P):
            in_specs.append(pl.BlockSpec((None, None, page * KVH, hd), page_map(p, 2)))
            args.append(pool)
    for new in (k_new, v_new):
        in_specs.append(pl.BlockSpec((None, page, KVH * hd), lambda b, j, pt: (b, 0, 0)))
        args.append(new)
    scratch = [pltpu.VMEM((KVH, P * page, hd), BF16), pltpu.VMEM((KVH, P * page, hd), BF16),
               pltpu.VMEM((KVH * R, LANE), F32), pltpu.VMEM((KVH * R, LANE), F32),
               pltpu.VMEM((KVH * R, hd), F32)]
    if kind == "fox":
        pool_lft, lf_newt = extras
        for p in range(P):
            in_specs.append(pl.BlockSpec((None, None, H, page), page_map(p, 2)))
            args.append(pool_lft)
        in_specs.append(pl.BlockSpec((None, H, page), lambda b, j, pt: (b, 0, 0)))
        args.append(lf_newt)
        scratch += [pltpu.VMEM((H, P * page), F32), pltpu.VMEM((H, 1), F32)]
    else:
        madd_past, madd_new, bias_last, bias_new = extras
        in_specs.append(pl.BlockSpec((None, Q, P * page), lambda b, j, pt: (b, 0, j)))
        args.append(madd_past)
        in_specs.append(pl.BlockSpec((None, Q, page), lambda b, j, pt: (b, 0, 0)))
        args.append(madd_new)
        for t in (bias_last, bias_new):
            in_specs.append(pl.BlockSpec((KVH * R, page), lambda b, j, pt: (0, 0)))
            args.append(t)
    grid_spec = pltpu.PrefetchScalarGridSpec(
        num_scalar_prefetch=1,
        grid=(Bn, nb),
        in_specs=in_specs,
        out_specs=pl.BlockSpec((None, KVH, R, hd), lambda b, j, pt: (b, 0, 0, 0)),
        scratch_shapes=scratch,
    )
    return pl.pallas_call(
        functools.partial(_paged_attn_kernel, kind=kind, P=P, KVH=KVH, G=G, Q=Q, page=page, hd=hd, scale=scale),
        grid_spec=grid_spec,
        out_shape=jax.ShapeDtypeStruct((Bn, KVH, R, hd), F32),
        compiler_params=_cparams("arbitrary", "arbitrary"),
        name=kind + "_attn_sample",
    )(page_table, *args)


def _sortable(s):
    bits = pltpu.bitcast(s + 0.0, I32)
    return jnp.where(bits >= 0, bits, bits ^ jnp.int32(0x7FFFFFFF))


def _kth_largest_key(count_ge, rows, k):
    sign = jnp.int32(-2 ** 31)

    def body(i, t):
        cand = t | jnp.left_shift(jnp.int32(1), 31 - i)
        cnt = count_ge(cand ^ sign)
        return jnp.where(cnt >= k, cand, t)

    t = lax.fori_loop(0, 32, body, jnp.zeros((rows, 1), I32))
    return t ^ sign


def _last_tie_index(ties_before, need, rows, n_keys):
    nbits = max(int(n_keys - 1).bit_length(), 1)

    def body(i, m):
        cand = m | jnp.left_shift(jnp.int32(1), nbits - 1 - i)
        return jnp.where(ties_before(cand) < need, cand, m)

    return lax.fori_loop(0, nbits, body, jnp.zeros((rows, 1), I32))


def _dsa_score_kernel(qi_ref, ki_ref, wi_ref, o_ref, acc_ref, wb_ref, *, HI, tq, T, q0, k_top, scale):
    wi = wi_ref[...]
    for h in range(HI):
        wb_ref[h] = jnp.broadcast_to(wi[:, h:h + 1], (tq, LANE))
    acc_ref[...] = jnp.zeros(acc_ref.shape, F32)
    ki = ki_ref[...]
    reps = T // LANE

    def body(h, carry):
        d = lax.dot_general(qi_ref[h], ki, _NT, preferred_element_type=F32)
        w = jnp.concatenate([wb_ref[h]] * reps, axis=1)
        acc_ref[...] += w * jnp.maximum(d, 0.0)
        return carry

    lax.fori_loop(0, HI, body, 0)
    qpos = lax.broadcasted_iota(I32, (tq, T), 0) + q0
    kpos = lax.broadcasted_iota(I32, (tq, T), 1)
    adm = kpos <= qpos
    key = _sortable(jnp.where(adm, acc_ref[...] * scale, NEG_INF))

    def count_ge(thr):
        return jnp.sum(jnp.where(key >= thr, 1, 0), axis=-1, keepdims=True)

    thr = _kth_largest_key(count_ge, tq, k_top)
    sel = jnp.logical_and(key >= thr, adm)
    o_ref[...] = jnp.where(sel, 0.0, NEG_INF)

    @pl.when(jnp.max(jnp.sum(jnp.where(sel, 1, 0), axis=-1, keepdims=True)) > k_top)
    def _():
        gt = key > thr
        tie = key == thr
        need = k_top - jnp.sum(jnp.where(gt, 1, 0), axis=-1, keepdims=True)

        def ties_before(m):
            return jnp.sum(jnp.where(jnp.logical_and(tie, kpos < m), 1, 0), axis=-1, keepdims=True)

        last = _last_tie_index(ties_before, need, tq, T)
        keep = jnp.logical_or(gt, jnp.logical_and(tie, kpos <= last))
        o_ref[...] = jnp.where(jnp.logical_and(keep, adm), 0.0, NEG_INF)


def _dsa_select_prompt(qi_hm, ki16, wi, Bn, T, c, tq, k_top, scale):
    HI, M, DI = qi_hm.shape
    nq = T // tq
    tk = (c + 1) * tq
    wl = wi.shape[1]
    return pl.pallas_call(
        functools.partial(_dsa_score_kernel, HI=HI, tq=tq, T=tk, q0=c * tq, k_top=k_top, scale=scale),
        grid=(Bn,),
        in_specs=[pl.BlockSpec((HI, tq, DI), lambda b: (0, b * nq + c, 0)),
                  pl.BlockSpec((None, tk, DI), lambda b: (b, 0, 0)),
                  pl.BlockSpec((tq, wl), lambda b: (b * nq + c, 0))],
        out_specs=pl.BlockSpec((None, tq, tk), lambda b: (b, 0, 0)),
        out_shape=jax.ShapeDtypeStruct((Bn, tq, tk), F32),
        scratch_shapes=[pltpu.VMEM((tq, tk), F32), pltpu.VMEM((HI, tq, LANE), F32)],
        compiler_params=_cparams("parallel"),
        name="dsa_select_prompt",
    )(qi_hm, ki16, wi)


def _dsa_attn_kernel(q_ref, k_ref, v_ref, madd_ref, td_ref, tl_ref, o_ref, s_ref, *, G, tq, iq0, scale):
    iq = iq0 + pl.program_id(1)
    hd = q_ref.shape[-1]
    k = k_ref[...]
    v = v_ref[...]
    madd = madd_ref[...]
    bt = td_ref.shape[-1]
    for g in range(G):
        s_ref[g] = lax.dot_general(q_ref[g], k, _NT, preferred_element_type=F32) * (scale * LOG2E) + madd
        for r in range(tq // bt):
            rows = slice(r * bt, (r + 1) * bt)
            a = iq * (tq // bt) + r
            s_ref[g, rows, pl.ds(pl.multiple_of(a * bt, bt), bt)] += td_ref[g] * LOG2E

            @pl.when(a > 0)
            def _():
                s_ref[g, rows, pl.ds(pl.multiple_of(jnp.maximum(a - 1, 0) * bt, bt), bt)] += tl_ref[g] * LOG2E

        s = s_ref[g]
        m = jnp.max(s, axis=-1, keepdims=True)
        p = jnp.exp2(s - m)
        l = jnp.sum(p, axis=-1, keepdims=True)
        o = jnp.dot(p.astype(BF16), v, preferred_element_type=F32) / l
        o_ref[:, g * hd:(g + 1) * hd] = o.astype(o_ref.dtype)


def _dsa_attn_prompt(q_hm, k16, v16, madd, t_diag, t_left, Bn, T, c, tqs, tq, G, scale):
    H, M, hd = q_hm.shape
    KVH = H // G
    tk = (c + 1) * tqs
    nsub = tqs // tq
    nq = T // tq
    return pl.pallas_call(
        functools.partial(_dsa_attn_kernel, G=G, tq=tq, iq0=c * nsub, scale=scale),
        grid=(Bn, nsub, KVH),
        in_specs=[pl.BlockSpec((G, tq, hd), lambda b, i, h: (h, b * nq + c * nsub + i, 0)),
                  pl.BlockSpec((None, tk, hd), lambda b, i, h: (b, 0, h)),
                  pl.BlockSpec((None, tk, hd), lambda b, i, h: (b, 0, h)),
                  pl.BlockSpec((None, tq, tk), lambda b, i, h: (b, i, 0)),
                  pl.BlockSpec((G,) + t_diag.shape[1:], lambda b, i, h: (h, 0, 0)),
                  pl.BlockSpec((G,) + t_left.shape[1:], lambda b, i, h: (h, 0, 0))],
        out_specs=pl.BlockSpec((None, tq, G * hd), lambda b, i, h: (b, i, h)),
        out_shape=jax.ShapeDtypeStruct((Bn, tqs, H * hd), BF16),
        scratch_shapes=[pltpu.VMEM((G, tq, tk), F32)],
        compiler_params=_cparams("parallel", "parallel", "arbitrary"),
        name="dsa_attn_prompt",
    )(q_hm, k16, v16, madd, t_diag, t_left)


def _dsa_score_sample_kernel(*refs, P, HI, Q, page, k_top, scale):
    pt_ref = refs[0]
    del pt_ref
    qi_ref, wi_ref = refs[1], refs[2]
    kp_refs = refs[3:3 + P]
    kn_ref = refs[3 + P]
    mp_ref, mn_ref = refs[4 + P], refs[5 + P]
    kb_ref, sp_ref, sn_ref = refs[6 + P:9 + P]

    jb = pl.program_id(1)
    nb = pl.num_programs(1)
    n = P * page
    qi = qi_ref[...]
    wcol = wi_ref[...]

    def score(keys16):
        d = lax.dot_general(qi, keys16, _NT, preferred_element_type=F32)
        d = jnp.broadcast_to(wcol, d.shape) * jnp.maximum(d, 0.0)
        tot = d[0:Q]
        for h in range(1, HI):
            tot = tot + d[h * Q:(h + 1) * Q]
        return tot * scale

    for p in range(P):
        kb_ref[p * page:(p + 1) * page, :] = kp_refs[p][...].astype(BF16)
    sp_ref[jb] = score(kb_ref[...])

    @pl.when(jb == nb - 1)
    def _():
        qpos = lax.broadcasted_iota(I32, (Q, page), 0)
        kpos = lax.broadcasted_iota(I32, (Q, page), 1)
        adm = kpos <= qpos
        sn_ref[...] = jnp.where(adm, score(kn_ref[...].astype(BF16)), NEG_INF)
        key_p = _sortable(sp_ref[...])
        key_n = _sortable(sn_ref[...])

        def count_ge(thr):
            cp = jnp.sum(jnp.where(key_p >= thr[None], 1, 0), axis=-1, keepdims=True)
            cn = jnp.sum(jnp.where(key_n >= thr, 1, 0), axis=-1, keepdims=True)
            return jnp.sum(cp, axis=0) + cn

        thr = _kth_largest_key(count_ge, Q, k_top)
        sel_n = jnp.logical_and(key_n >= thr, adm)
        mp_ref[...] = jnp.where(key_p >= thr[None], 0.0, NEG_INF)
        mn_ref[...] = jnp.where(sel_n, 0.0, NEG_INF)

        def count(mask_p, mask_n):
            cp = jnp.sum(jnp.where(mask_p, 1, 0), axis=-1, keepdims=True)
            return jnp.sum(cp, axis=0) + jnp.sum(jnp.where(mask_n, 1, 0), axis=-1, keepdims=True)

        @pl.when(jnp.max(count(key_p >= thr[None], sel_n)) > k_top)
        def _():
            idx_p = (lax.broadcasted_iota(I32, key_p.shape, 0) * n + lax.broadcasted_iota(I32, key_p.shape, 2))
            n_past = key_p.shape[0] * n
            idx_n = kpos + n_past
            gt_p, gt_n = key_p > thr[None], key_n > thr
            tie_p, tie_n = key_p == thr[None], key_n == thr
            need = k_top - count(gt_p, gt_n)

            def ties_before(m):
                return count(jnp.logical_and(tie_p, idx_p < m[None]), jnp.logical_and(tie_n, idx_n < m))

            last = _last_tie_index(ties_before, need, Q, n_past + page)
            keep_p = jnp.logical_or(gt_p, jnp.logical_and(tie_p, idx_p <= last[None]))
            keep_n = jnp.logical_or(gt_n, jnp.logical_and(tie_n, idx_n <= last))
            mp_ref[...] = jnp.where(keep_p, 0.0, NEG_INF)
            mn_ref[...] = jnp.where(jnp.logical_and(keep_n, adm), 0.0, NEG_INF)


def _dsa_select_sample(qi, wi, pool_ki, layer, ki_new, page_table, Q, k_top, scale):
    Bn, RQ, DI = qi.shape
    page = pool_ki.shape[2]
    n_pages = page_table.shape[1]
    P = min(PAGES_PER_STEP, n_pages)
    nb = n_pages // P
    HI = RQ // Q
    in_specs = [pl.BlockSpec((None, RQ, DI), lambda b, j, pt: (b, 0, 0)),
                pl.BlockSpec((None, RQ, 1), lambda b, j, pt: (b, 0, 0))]
    args = [qi, wi]
    for p in range(P):
        in_specs.append(pl.BlockSpec((None, None, page, DI), functools.partial(
            lambda b, j, pt, p: (layer, pt[b, j * P + p], 0, 0), p=p)))
        args.append(pool_ki)
    in_specs.append(pl.BlockSpec((None, page, DI), lambda b, j, pt: (b, 0, 0)))
    args.append(ki_new)
    grid_spec = pltpu.PrefetchScalarGridSpec(
        num_scalar_prefetch=1,
        grid=(Bn, nb),
        in_specs=in_specs,
        out_specs=[pl.BlockSpec((None, nb, Q, P * page), lambda b, j, pt: (b, 0, 0, 0)),
                   pl.BlockSpec((None, Q, page), lambda b, j, pt: (b, 0, 0))],
        scratch_shapes=[pltpu.VMEM((P * page, DI), BF16), pltpu.VMEM((nb, Q, P * page), F32),
                        pltpu.VMEM((Q, page), F32)],
    )
    return pl.pallas_call(
        functools.partial(_dsa_score_sample_kernel, P=P, HI=HI, Q=Q, page=page, k_top=k_top, scale=scale),
        grid_spec=grid_spec,
        out_shape=[jax.ShapeDtypeStruct((Bn, nb, Q, P * page), F32), jax.ShapeDtypeStruct((Bn, Q, page), F32)],
        compiler_params=_cparams("arbitrary", "arbitrary"),
        name="dsa_select_sample",
    )(page_table, *args)


def _t5_bucket_np(dist):
    max_exact = N_BUCKETS // 2
    d = np.maximum(dist, 0)
    ratio = np.log(np.maximum(d, 1).astype(np.float32) / np.float32(max_exact)) / np.float32(
        math.log(MAX_DISTANCE / max_exact))
    large = np.minimum(max_exact + (ratio * (N_BUCKETS - max_exact)).astype(np.int32), N_BUCKETS - 1)
    return np.where(d < max_exact, d, large).astype(np.int32)


def _pad_cols(w, n):
    return jnp.pad(w, ((0, 0), (0, n - w.shape[1])))


def _to_rows(o, Bn, Q, KVH, G, hd):
    o = o.reshape(Bn, KVH, G, Q, hd)
    return jnp.transpose(o, (0, 3, 1, 2, 4)).reshape(Bn * Q, KVH * G * hd)


def _to_heads(q, Bn, Q, KVH, G, hd):
    q = q.reshape(Bn, Q, KVH, G, hd)
    return jnp.transpose(q, (0, 2, 3, 1, 4)).reshape(Bn, KVH, G * Q, hd)


def _pad_page(x, page):
    return jnp.pad(x, ((0, 0), (0, page - x.shape[1]), (0, 0)))


def _next_norm(hp, pg, psq, hs, sg, ssq):
    n = hp.shape[1]
    if psq.ndim == 3:
        psq, ssq = jnp.sum(psq, axis=0), jnp.sum(ssq, axis=0)
    return hp, hs, (pg, sg, (lax.rsqrt(psq / n + RMS_EPS), lax.rsqrt(ssq / n + RMS_EPS)))


def _fox_layer(hp, hs, xin, next_gain, dims, layer, pool_k, pool_v, pool_lft, page_table,
               w_q, w_k, w_v, w_f, b_f, w_o):
    up, us, rs = xin
    Bp, T, Bs, Q, H, KVH, hd = dims
    G = H // KVH
    scale = hd ** -0.5
    page = pool_k.shape[2] // KVH
    wf_pad = _pad_cols(w_f[layer], LANE)[None]
    bf_pad = jnp.pad(b_f[layer], (0, LANE - H))
    wft = w_f[layer].T.astype(BF16)
    tq, tk = min(512, T // 2), min(512, T)

    q_hm, qs = _mmw(up, w_q, layer, xs=us, scale=rs, emit_f32=False, emit_bf16=True, head_major=True, name="fox_q")
    k32, k16, ks = _mmw(up, w_k, layer, xs=us, scale=rs, emit_bf16=True, name="fox_k")
    v32, v16, vs = _mmw(up, w_v, layer, xs=us, scale=rs, emit_bf16=True, name="fox_v")
    lf_pad, lfs_pad = _mmw(up, wf_pad, 0, xs=us, scale=rs, ls_bias=bf_pad, tn=LANE, name="fox_logf")
    lfs = lfs_pad[:, :H]

    cum = _fox_cum(up, rs[0], wft, b_f[layer], Bp, T, tk)
    cum = jnp.transpose(cum.reshape(Bp, T // tk, KVH, G, 1, tk), (0, 2, 1, 3, 4, 5))
    o = _fox_attn_prompt(q_hm, k16, v16, cum, Bp, T, G, scale, tq, tk)

    q_t = _to_heads(qs, Bs, Q, KVH, G, hd).astype(BF16)
    lf_newt = jnp.transpose(_pad_page(lfs.reshape(Bs, Q, H), page), (0, 2, 1))
    o_s = _paged_attn("fox", q_t, pool_k, pool_v, layer,
                      _pad_page(ks.reshape(Bs, Q, KVH * hd), page), _pad_page(vs.reshape(Bs, Q, KVH * hd), page),
                      page_table, (pool_lft, lf_newt), scale)
    o_s = _to_rows(o_s, Bs, Q, KVH, G, hd).astype(BF16)

    hp, hs, xnext = _next_norm(*_mmw(o, w_o, layer, xs=o_s, res=hp, res_s=hs, next_gain=next_gain, name="fox_o"))
    rows = (k32.reshape(Bp, T, KVH, hd), v32.reshape(Bp, T, KVH, hd), lf_pad[:, :H].reshape(Bp, T, H),
            ks.reshape(Bs, Q, KVH, hd), vs.reshape(Bs, Q, KVH, hd), lfs.reshape(Bs, Q, H))
    return hp, hs, xnext, rows


def _sconv_layer(hp, hs, xin, next_gain, dims, layer, state, w_in, w_conv, w_out):
    up, us, rs = xin
    Bp, T, Bs, Q = dims[:4]
    D = w_out.shape[1]
    zp, stp, zs, sts = _gconv(up, us, rs, w_in, layer, (0, D, 2 * D), D, w_conv, _conv_prefix(state[layer], Q),
                              mode="sconv", seq_len=T, tail_len=Q, tm=512, name="sconv_in")
    hp, hs, xnext = _next_norm(*_mmw(zp, w_out, layer, xs=zs, res=hp, res_s=hs, next_gain=next_gain,
                                     name="sconv_out"))
    nblk = stp.shape[0] // Bp
    sp = stp.reshape(Bp, nblk, SUBLANE, D)[:, -1, SUBLANE - 2:, :]
    ss = sts.reshape(Bs, Q, D)[:, Q - 2:, :]
    return hp, hs, xnext, sp, ss


def _dsa_layer(hp, hs, xin, next_gain, dims, layer, pool_k, pool_v, pool_ki, page_table, rel_bias,
               w_q, w_k, w_v, w_o, w_qi, w_ki, w_wi):
    up, us, rs = xin
    Bp, T, Bs, Q, H, KVH, hd = dims
    G = H // KVH
    scale = hd ** -0.5
    page = pool_k.shape[2] // KVH
    DI = w_ki.shape[2]
    HI = w_wi.shape[2]
    idx_scale = (DI * HI) ** -0.5
    past = page_table.shape[1] * page
    wkw = jnp.concatenate([w_ki[layer], _pad_cols(w_wi[layer], LANE)], axis=1)[None]
    c_far = rel_bias[N_BUCKETS - 1]

    q_hm, qs = _mmw(up, w_q, layer, xs=us, scale=rs, emit_f32=False, emit_bf16=True, head_major=True, name="dsa_q")
    k32, k16, ks = _mmw(up, w_k, layer, xs=us, scale=rs, emit_bf16=True, name="dsa_k")
    v32, v16, vs = _mmw(up, w_v, layer, xs=us, scale=rs, emit_bf16=True, name="dsa_v")
    qi_hm, qis = _mmw(up, w_qi, layer, xs=us, scale=rs, emit_f32=False, emit_bf16=True, head_major=True,
                      name="dsa_qi")
    kw, kws = _mmw(up, wkw, 0, xs=us, scale=rs, tn=DI + LANE, name="dsa_kiwi")

    tqs = min(512, T // 2)
    tq = min(512, tqs)
    assert tq >= MAX_DISTANCE and tqs % tq == 0 and T % tqs == 0
    ki32 = kw[:, :DI]
    wi = kw[:, DI:]
    ki16 = ki32.astype(BF16).reshape(Bp, T, DI)
    k16 = k16.reshape(Bp, T, KVH * hd)
    v16 = v16.reshape(Bp, T, KVH * hd)
    bt = MAX_DISTANCE
    ii = np.arange(bt)[:, None]
    jj = np.arange(bt)[None, :]
    t_diag = jnp.transpose(rel_bias[_t5_bucket_np(ii - jj)] - c_far, (2, 0, 1))
    t_left = jnp.transpose(rel_bias[_t5_bucket_np(bt + ii - jj)] - c_far, (2, 0, 1))
    k_top = min(TOPK_MAX, T // 4)
    o_blocks = []
    for c in range(T // tqs):
        madd = _dsa_select_prompt(qi_hm, ki16, wi, Bp, T, c, tqs, k_top, idx_scale)
        o_blocks.append(_dsa_attn_prompt(q_hm, k16, v16, madd, t_diag, t_left, Bp, T, c, tqs, tq, G, scale))
    o = jnp.concatenate(o_blocks, axis=1).reshape(Bp * T, H * hd)

    kis = kws[:, :DI]
    wis = kws[:, DI:DI + HI]
    qi_t = jnp.transpose(qis.reshape(Bs, Q, HI, DI), (0, 2, 1, 3)).reshape(Bs, HI * Q, DI).astype(BF16)
    wi_t = jnp.transpose(wis.reshape(Bs, Q, HI), (0, 2, 1)).reshape(Bs, HI * Q, 1)
    mp, mn = _dsa_select_sample(qi_t, wi_t, pool_ki, layer, _pad_page(kis.reshape(Bs, Q, DI), page), page_table,
                                Q, min(TOPK_MAX, (past + Q) // 4), idx_scale)
    nb, n = mp.shape[1], mp.shape[3]
    madd_past = jnp.transpose(mp, (0, 2, 1, 3)).reshape(Bs, Q, nb * n)
    qq = np.arange(Q)[:, None]
    cc = np.arange(page)[None, :]

    def sample_bias(dist):
        t = rel_bias[_t5_bucket_np(dist)] - c_far
        t = jnp.transpose(t.reshape(Q, page, KVH, G), (2, 3, 0, 1))
        return t.reshape(KVH * G * Q, page)

    bias_last = sample_bias(past + qq - (past - page + cc))
    bias_new = sample_bias(qq - cc)
    q_t = _to_heads(qs, Bs, Q, KVH, G, hd).astype(BF16)
    o_s = _paged_attn("dsa", q_t, pool_k, pool_v, layer,
                      _pad_page(ks.reshape(Bs, Q, KVH * hd), page), _pad_page(vs.reshape(Bs, Q, KVH * hd), page),
                      page_table, (madd_past, mn, bias_last, bias_new), scale)
    o_s = _to_rows(o_s, Bs, Q, KVH, G, hd).astype(BF16)

    hp, hs, xnext = _next_norm(*_mmw(o, w_o, layer, xs=o_s, res=hp, res_s=hs, next_gain=next_gain, name="dsa_o"))
    rows = (k32.reshape(Bp, T, KVH, hd), v32.reshape(Bp, T, KVH, hd), ki32.reshape(Bp, T, DI),
            ks.reshape(Bs, Q, KVH, hd), vs.reshape(Bs, Q, KVH, hd), kis.reshape(Bs, Q, DI))
    return hp, hs, xnext, rows


def _ffn_layer(hp, hs, xin, next_gain, dims, layer, state, w_up, w_conv, w_down16):
    vp, vs, rs = xin
    Bp, T, Bs, Q = dims[:4]
    DFF = w_down16.shape[1]
    ap, stp, a_s, sts = _gconv(vp, vs, rs, w_up, layer, (0, DFF), DFF, w_conv, _conv_prefix(state[layer], Q),
                               mode="ffn", seq_len=T, tail_len=Q, tm=1024, name="ffn_up")
    if next_gain is None:
        hp = _mm(ap, w_down16, layer, hp, name="ffn_down")
        hs = _mm(a_s, w_down16, layer, hs, name="ffn_down_s")
        xnext = None
    else:
        hp, pg, psq = _mm(ap, w_down16, layer, hp, next_gain=next_gain, name="ffn_down")
        hs, sg, ssq = _mm(a_s, w_down16, layer, hs, next_gain=next_gain, name="ffn_down_s")
        hp, hs, xnext = _next_norm(hp, pg, psq, hs, sg, ssq)
    nblk = stp.shape[0] // Bp
    cp = stp.reshape(Bp, nblk, SUBLANE, DFF)[:, -1, SUBLANE - 2:, :]
    cs = sts.reshape(Bs, Q, DFF)[:, Q - 2:, :]
    return hp, hs, xnext, cp, cs


def kernel(x_prompt, x_sample, cache_fox_k, cache_fox_v, cache_fox_logf, state_sconv, cache_dsa_k, cache_dsa_v,
           cache_dsa_kidx, state_ffn_conv, page_table, rel_bias, norm_mix, norm_ffn, norm_final,
           fox_w_q, fox_w_k, fox_w_v, fox_w_f, fox_b_f, fox_w_o, sc_w_in, sc_w_conv, sc_w_out,
           dsa_w_q, dsa_w_k, dsa_w_v, dsa_w_o, dsa_w_qi, dsa_w_ki, dsa_w_wi, ffn_w_up, ffn_w_conv, ffn_w_down):
    Bp, T, D = x_prompt.shape
    Bs, Q, _ = x_sample.shape
    depth = norm_mix.shape[0]
    KVH, hd = cache_fox_k.shape[3], cache_fox_k.shape[4]
    H = fox_w_f.shape[2]
    dims = (Bp, T, Bs, Q, H, KVH, hd)
    hp = x_prompt.reshape(Bp * T, D)
    hs = x_sample.reshape(Bs * Q, D)

    def key_head_rows(c):
        return c.reshape(c.shape[0], c.shape[1], c.shape[2] * KVH, hd)

    fox_k, fox_v, dsa_k, dsa_v = (key_head_rows(c) for c in (cache_fox_k, cache_fox_v, cache_dsa_k, cache_dsa_v))
    fox_lft = jnp.transpose(cache_fox_logf, (0, 1, 3, 2))
    w_down16 = ffn_w_down.astype(BF16)
    fox_new = ([], [], [], [], [], [])
    sc_new = ([], [])
    dsa_new = ([], [], [], [], [], [])
    ffn_new = ([], [])
    ones = (jnp.ones((Bp * T, 1), F32), jnp.ones((Bs * Q, 1), F32))
    xin = (_rmsnorm(hp, norm_mix[0], BF16), _rmsnorm(hs, norm_mix[0], BF16), ones)
    for i in range(depth):
        j, kind = i // 3, i % 3
        if kind == 0:
            hp, hs, xin, rows = _fox_layer(hp, hs, xin, norm_ffn[i], dims, j, fox_k, fox_v, fox_lft, page_table,
                                           fox_w_q, fox_w_k, fox_w_v, fox_w_f, fox_b_f, fox_w_o)
            for lst, a in zip(fox_new, rows):
                lst.append(a)
        elif kind == 1:
            hp, hs, xin, sp, ss = _sconv_layer(hp, hs, xin, norm_ffn[i], dims, j, state_sconv, sc_w_in, sc_w_conv,
                                               sc_w_out)
            sc_new[0].append(sp)
            sc_new[1].append(ss)
        else:
            hp, hs, xin, rows = _dsa_layer(hp, hs, xin, norm_ffn[i], dims, j, dsa_k, dsa_v, cache_dsa_kidx,
                                           page_table, rel_bias, dsa_w_q, dsa_w_k, dsa_w_v, dsa_w_o, dsa_w_qi,
                                           dsa_w_ki, dsa_w_wi)
            for lst, a in zip(dsa_new, rows):
                lst.append(a)
        hp, hs, xin, cp, cs = _ffn_layer(hp, hs, xin, norm_mix[i + 1] if i + 1 < depth else None, dims, i,
                                         state_ffn_conv, ffn_w_up, ffn_w_conv, w_down16)
        ffn_new[0].append(cp)
        ffn_new[1].append(cs)
    y_prompt = _rmsnorm(hp, norm_final, F32).reshape(Bp, T, D)
    y_sample = _rmsnorm(hs, norm_final, F32).reshape(Bs, Q, D)
    return (y_prompt, y_sample,
            jnp.stack(fox_new[0]), jnp.stack(fox_new[1]), jnp.stack(fox_new[2]),
            jnp.stack(fox_new[3]), jnp.stack(fox_new[4]), jnp.stack(fox_new[5]),
            jnp.stack(sc_new[0]), jnp.stack(sc_new[1]),
            jnp.stack(dsa_new[0]), jnp.stack(dsa_new[1]), jnp.stack(dsa_new[2]),
            jnp.stack(dsa_new[3]), jnp.stack(dsa_new[4]), jnp.stack(dsa_new[5]),
            jnp.stack(ffn_new[0]), jnp.stack(ffn_new[1]))
```

```python
import functools
import math

import numpy as np
import jax
import jax.numpy as jnp
from jax import lax
from jax.experimental import pallas as pl
from jax.experimental.pallas import tpu as pltpu

F32 = jnp.float32
BF16 = jnp.bfloat16
I32 = jnp.int32

RMS_EPS = 1e-6
NEG_INF = -1e30
TOPK_MAX = 256
N_BUCKETS = 32
MAX_DISTANCE = 128
CONV_WIDTH = 3
LOG2E = 1.4426950408889634

LANE = 128
SUBLANE = 8
VMEM_LIMIT_BYTES = 56 * 1024 * 1024
PAGES_PER_STEP = 16

ROW_TILE = 1024
COL_TILE = 512
GATED_COL_TILE = 256
GATED_ROW_CHUNK = 512
SCONV_ROW_TILE = 512
DOWN_TILE = 512
NORM_ROW_TILE = 512
ATTN_TILE = 512

_NT = (((1,), (1,)), ((), ()))


def _cparams(*sem):
    return pltpu.CompilerParams(dimension_semantics=sem, vmem_limit_bytes=VMEM_LIMIT_BYTES)


def _log_sigmoid(x):
    return jnp.minimum(x, 0.0) - jnp.log1p(jnp.exp(-jnp.abs(x)))


def _split3(x):
    hi = x.astype(BF16)
    r1 = x - hi.astype(F32)
    mid = r1.astype(BF16)
    lo = (r1 - mid.astype(F32)).astype(BF16)
    return hi, mid, lo


def _tri_cumsum(x):
    n = x.shape[-1]
    r = lax.broadcasted_iota(I32, (n, n), 0)
    c = lax.broadcasted_iota(I32, (n, n), 1)
    tri = jnp.where(r <= c, 1.0, 0.0).astype(BF16)
    hi, mid, lo = _split3(x)
    out = jnp.dot(hi, tri, preferred_element_type=F32)
    out = out + jnp.dot(mid, tri, preferred_element_type=F32)
    out = out + jnp.dot(lo, tri, preferred_element_type=F32)
    return out


def _rms_kernel(x_ref, g_ref, o_ref):
    x = x_ref[...]
    ms = jnp.mean(x * x, axis=-1, keepdims=True)
    o_ref[...] = ((x * lax.rsqrt(ms + RMS_EPS)) * g_ref[...]).astype(o_ref.dtype)


def _rmsnorm(h, g, out_dtype):
    M, D = h.shape
    tm = min(NORM_ROW_TILE, M)
    return pl.pallas_call(
        _rms_kernel,
        grid=(M // tm,),
        in_specs=[pl.BlockSpec((tm, D), lambda i: (i, 0)), pl.BlockSpec((1, D), lambda i: (0, 0))],
        out_specs=pl.BlockSpec((tm, D), lambda i: (i, 0)),
        out_shape=jax.ShapeDtypeStruct((M, D), out_dtype),
        compiler_params=_cparams("parallel"),
        name="rmsnorm",
    )(h, g.reshape(1, D).astype(F32))


def _mm_kernel(x_ref, w_ref, r_ref, *refs, has_gain):
    out = r_ref[...] + jnp.dot(x_ref[...], w_ref[...], preferred_element_type=F32)
    if not has_gain:
        refs[0][...] = out
        return
    g_ref, o_ref, og_ref, oq_ref = refs
    o_ref[...] = out
    og_ref[...] = (out * g_ref[...]).astype(BF16)
    sq = jnp.sum(out * out, axis=1, keepdims=True)

    @pl.when(pl.program_id(1) == 0)
    def _():
        oq_ref[...] = sq

    @pl.when(pl.program_id(1) > 0)
    def _():
        oq_ref[...] += sq


def _mm(x, w, layer, res, *, next_gain=None, tm=DOWN_TILE, tn=DOWN_TILE, name="mm"):
    M, K = x.shape
    N = w.shape[2]
    tm = min(tm, M)
    tn = min(tn, N)
    assert M % tm == 0 and N % tn == 0, (M, N, tm, tn)
    in_specs = [pl.BlockSpec((tm, K), lambda i, j: (i, 0)),
                pl.BlockSpec((None, K, tn), lambda i, j: (layer, 0, j)),
                pl.BlockSpec((tm, tn), lambda i, j: (i, j))]
    args = [x, w, res]
    out_specs = [pl.BlockSpec((tm, tn), lambda i, j: (i, j))]
    out_shape = [jax.ShapeDtypeStruct((M, N), F32)]
    if next_gain is not None:
        in_specs.append(pl.BlockSpec((1, tn), lambda i, j: (0, j)))
        args.append(next_gain.reshape(1, N).astype(F32))
        out_specs += [pl.BlockSpec((tm, tn), lambda i, j: (i, j)), pl.BlockSpec((tm, 1), lambda i, j: (i, 0))]
        out_shape += [jax.ShapeDtypeStruct((M, N), BF16), jax.ShapeDtypeStruct((M, 1), F32)]
    outs = pl.pallas_call(
        functools.partial(_mm_kernel, has_gain=next_gain is not None),
        grid=(M // tm, N // tn),
        in_specs=in_specs,
        out_specs=out_specs,
        out_shape=out_shape,
        compiler_params=_cparams("parallel", "arbitrary"),
        name=name,
    )(*args)
    return outs[0] if len(outs) == 1 else tuple(outs)


def _mmw_kernel(*refs, has_res, has_ls, has_gain, has_scale, emit_f32, emit_bf16, head_major, has_tail):
    it = iter(refs)
    x_ref = next(it)
    w_ref = next(it)
    b_ref = next(it) if has_ls else None
    r_ref = next(it) if has_res else None
    g_ref = next(it) if has_gain else None
    sc_ref = next(it) if has_scale else None
    xs_ref = next(it) if has_tail else None
    rs_ref = next(it) if has_tail and has_res else None
    scs_ref = next(it) if has_tail and has_scale else None
    o32_ref = next(it) if emit_f32 else None
    o16_ref = next(it) if emit_bf16 else None
    og_ref, oq_ref = (next(it), next(it)) if has_gain else (None, None)
    os_ref = next(it) if has_tail else None
    osg_ref, osq_ref = (next(it), next(it)) if has_tail and has_gain else (None, None)
    wbf_ref = next(it)

    def project(x, r, sc):
        acc = jnp.dot(x, wbf_ref[...], preferred_element_type=F32)
        if sc is not None:
            acc = acc * sc[...]
        if has_ls:
            acc = _log_sigmoid(acc + b_ref[...])
        if r is not None:
            acc = r[...] + acc
        return acc

    def next_norm(acc, og, oq):
        og[...] = (acc * g_ref[...]).astype(BF16)
        oq[...] = jnp.sum(acc * acc, axis=1, keepdims=True)

    @pl.when(pl.program_id(1) == 0)
    def _():
        wbf_ref[...] = w_ref[...].astype(BF16)
        if has_tail:
            tail = project(xs_ref[...], rs_ref, scs_ref)
            os_ref[...] = tail
            if has_gain:
                next_norm(tail, osg_ref, osq_ref)

    acc = project(x_ref[...], r_ref, sc_ref)
    if has_gain:
        next_norm(acc, og_ref, oq_ref)
    if emit_f32:
        o32_ref[...] = acc
    if emit_bf16:
        if head_major:
            for hh in range(o16_ref.shape[0]):
                o16_ref[hh] = acc[:, hh * LANE:(hh + 1) * LANE].astype(BF16)
        else:
            o16_ref[...] = acc.astype(BF16)


def _mmw(x, w, layer, *, xs=None, res=None, res_s=None, ls_bias=None, scale=None, next_gain=None, emit_f32=True,
         emit_bf16=False, head_major=False, tm=ROW_TILE, tn=COL_TILE, name="mmw"):
    M, K = x.shape
    N = w.shape[2]
    tm = min(tm, M)
    tn = min(tn, N)
    assert M % tm == 0 and N % tn == 0, (M, N, tm, tn)
    nj = N // tn
    has_tail = xs is not None
    in_specs = [pl.BlockSpec((tm, K), lambda j, i: (i, 0)), pl.BlockSpec((None, K, tn), lambda j, i: (layer, 0, j))]
    args = [x, w]
    if ls_bias is not None:
        in_specs.append(pl.BlockSpec((1, tn), lambda j, i: (0, j)))
        args.append(ls_bias.reshape(1, N).astype(F32))
    if res is not None:
        in_specs.append(pl.BlockSpec((tm, tn), lambda j, i: (i, j)))
        args.append(res)
    if next_gain is not None:
        in_specs.append(pl.BlockSpec((1, tn), lambda j, i: (0, j)))
        args.append(next_gain.reshape(1, N).astype(F32))
    if scale is not None:
        in_specs.append(pl.BlockSpec((tm, 1), lambda j, i: (i, 0)))
        args.append(scale[0])
    if has_tail:
        Ms = xs.shape[0]
        in_specs.append(pl.BlockSpec((Ms, K), lambda j, i: (0, 0)))
        args.append(xs)
        if res is not None:
            in_specs.append(pl.BlockSpec((Ms, tn), lambda j, i: (0, j)))
            args.append(res_s)
        if scale is not None:
            in_specs.append(pl.BlockSpec((Ms, 1), lambda j, i: (0, 0)))
            args.append(scale[1])
    out_specs, out_shape = [], []
    if emit_f32:
        out_specs.append(pl.BlockSpec((tm, tn), lambda j, i: (i, j)))
        out_shape.append(jax.ShapeDtypeStruct((M, N), F32))
    if emit_bf16:
        if head_major:
            out_specs.append(pl.BlockSpec((tn // LANE, tm, LANE), lambda j, i: (j, i, 0)))
            out_shape.append(jax.ShapeDtypeStruct((N // LANE, M, LANE), BF16))
        else:
            out_specs.append(pl.BlockSpec((tm, tn), lambda j, i: (i, j)))
            out_shape.append(jax.ShapeDtypeStruct((M, N), BF16))
    if next_gain is not None:
        out_specs += [pl.BlockSpec((tm, tn), lambda j, i: (i, j)), pl.BlockSpec((None, tm, 1), lambda j, i: (j, i, 0))]
        out_shape += [jax.ShapeDtypeStruct((M, N), BF16), jax.ShapeDtypeStruct((nj, M, 1), F32)]
    if has_tail:
        out_specs.append(pl.BlockSpec((Ms, tn), lambda j, i: (0, j)))
        out_shape.append(jax.ShapeDtypeStruct((Ms, N), F32))
        if next_gain is not None:
            out_specs += [pl.BlockSpec((Ms, tn), lambda j, i: (0, j)),
                          pl.BlockSpec((None, Ms, 1), lambda j, i: (j, 0, 0))]
            out_shape += [jax.ShapeDtypeStruct((Ms, N), BF16), jax.ShapeDtypeStruct((nj, Ms, 1), F32)]
    outs = pl.pallas_call(
        functools.partial(_mmw_kernel, has_res=res is not None, has_ls=ls_bias is not None,
                          has_gain=next_gain is not None, has_scale=scale is not None, emit_f32=emit_f32,
                          emit_bf16=emit_bf16, head_major=head_major, has_tail=has_tail),
        grid=(N // tn, M // tm),
        in_specs=in_specs,
        out_specs=out_specs,
        out_shape=out_shape,
        scratch_shapes=[pltpu.VMEM((K, tn), BF16)],
        compiler_params=_cparams("arbitrary", "arbitrary"),
        name=name,
    )(*args)
    return outs[0] if len(outs) == 1 else tuple(outs)


def _gconv_kernel(x_ref, *refs, mode, nw, tm, seq_blocks, tail_len, row_chunks):
    w_refs = refs[:nw]
    (wc_ref, xs_ref, p1_ref, p2_ref, sc_ref, scs_ref, o_ref, st_ref, os_ref, sts_ref,
     wbf_ref, buf_ref, carry_ref, bufs_ref) = refs[nw:]
    i = pl.program_id(1)
    ms = xs_ref.shape[0]

    def gated(x, sc):
        tn = wc_ref.shape[1]
        y = jnp.dot(x, wbf_ref[...], preferred_element_type=F32) * sc
        ys = [y[:, k * tn:(k + 1) * tn] for k in range(nw)]
        if mode == "ffn":
            return ys[0], ys[1]
        return ys[1] * ys[2], ys[0]

    def finish(cin, other, x1, x2):
        wc = wc_ref[...]
        y = wc[0:1, :] * x2 + wc[1:2, :] * x1 + wc[2:3, :] * cin
        if mode == "ffn":
            return ((y * (0.5 * jnp.tanh(0.5 * y) + 0.5)) * other).astype(BF16)
        return (other * y).astype(BF16)

    @pl.when(i == 0)
    def _():
        for k in range(nw):
            wbf_ref[:, k * wc_ref.shape[1]:(k + 1) * wc_ref.shape[1]] = w_refs[k][...].astype(BF16)
        cin, other = gated(xs_ref[...], scs_ref[...])
        bufs_ref[0:SUBLANE, :] = jnp.zeros((SUBLANE, cin.shape[1]), F32)
        bufs_ref[SUBLANE:SUBLANE + ms, :] = cin
        rmod = lax.broadcasted_iota(I32, (ms, 1), 0) % tail_len
        x1 = jnp.where(rmod >= 1, bufs_ref[SUBLANE - 1:SUBLANE - 1 + ms, :], p1_ref[...])
        x2 = jnp.where(rmod >= 2, bufs_ref[SUBLANE - 2:SUBLANE - 2 + ms, :], p2_ref[...])
        sts_ref[...] = cin
        os_ref[...] = finish(cin, other, x1, x2)

    first = (i % seq_blocks) == 0

    @pl.when(first)
    def _():
        buf_ref[0:SUBLANE, :] = jnp.zeros((SUBLANE, buf_ref.shape[1]), F32)

    @pl.when(jnp.logical_not(first))
    def _():
        buf_ref[0:SUBLANE, :] = carry_ref[...]

    rc = tm // row_chunks
    for c in range(row_chunks):
        lo = c * rc
        cin, other = gated(x_ref[lo:lo + rc, :], sc_ref[lo:lo + rc, :])
        buf_ref[SUBLANE + lo:SUBLANE + lo + rc, :] = cin
        o_ref[lo:lo + rc, :] = finish(cin, other, buf_ref[SUBLANE - 1 + lo:SUBLANE - 1 + lo + rc, :],
                                      buf_ref[SUBLANE - 2 + lo:SUBLANE - 2 + lo + rc, :])
    tail = buf_ref[tm:tm + SUBLANE, :]
    carry_ref[...] = tail
    st_ref[...] = tail


def _gconv(x, xs, scale, w, layer, col_offsets, nc, wconv, prefix, *, mode, seq_len, tail_len, tm,
           tn=GATED_COL_TILE, name="gconv"):
    M, K = x.shape
    Ms = xs.shape[0]
    tm = min(tm, seq_len)
    tn = min(tn, nc)
    assert M % tm == 0 and nc % tn == 0 and seq_len % tm == 0
    nw = len(col_offsets)
    in_specs = [pl.BlockSpec((tm, K), lambda j, i: (i, 0))]
    args = [x]
    for off in col_offsets:
        assert off % tn == 0
        in_specs.append(pl.BlockSpec((None, K, tn), functools.partial(lambda j, i, o: (layer, 0, o + j), o=off // tn)))
        args.append(w)
    in_specs.append(pl.BlockSpec((None, CONV_WIDTH, tn), lambda j, i: (layer, 0, j)))
    args.append(wconv)
    in_specs.append(pl.BlockSpec((Ms, K), lambda j, i: (0, 0)))
    args.append(xs)
    for p in prefix:
        in_specs.append(pl.BlockSpec((Ms, tn), lambda j, i: (0, j)))
        args.append(p)
    in_specs += [pl.BlockSpec((tm, 1), lambda j, i: (i, 0)), pl.BlockSpec((Ms, 1), lambda j, i: (0, 0))]
    args += [scale[0], scale[1]]
    out_specs = [pl.BlockSpec((tm, tn), lambda j, i: (i, j)),
                 pl.BlockSpec((None, SUBLANE, tn), lambda j, i: (i, 0, j)),
                 pl.BlockSpec((Ms, tn), lambda j, i: (0, j)),
                 pl.BlockSpec((Ms, tn), lambda j, i: (0, j))]
    out_shape = [jax.ShapeDtypeStruct((M, nc), BF16), jax.ShapeDtypeStruct((M // tm, SUBLANE, nc), F32),
                 jax.ShapeDtypeStruct((Ms, nc), BF16), jax.ShapeDtypeStruct((Ms, nc), F32)]
    return pl.pallas_call(
        functools.partial(_gconv_kernel, mode=mode, nw=nw, tm=tm, seq_blocks=seq_len // tm, tail_len=tail_len,
                          row_chunks=max(tm // GATED_ROW_CHUNK, 1)),
        grid=(nc // tn, M // tm),
        in_specs=in_specs,
        out_specs=out_specs,
        out_shape=out_shape,
        scratch_shapes=[pltpu.VMEM((K, nw * tn), BF16), pltpu.VMEM((SUBLANE + tm, tn), F32),
                        pltpu.VMEM((SUBLANE, tn), F32), pltpu.VMEM((SUBLANE + Ms, tn), F32)],
        compiler_params=_cparams("arbitrary", "arbitrary"),
        name=name,
    )(*args)


def _conv_prefix(state, seq_len):
    Bn, _, C = state.shape
    z = jnp.zeros((Bn, seq_len, C), F32)
    p1 = z.at[:, 0].set(state[:, 1])
    p2 = z.at[:, 0].set(state[:, 0]).at[:, 1].set(state[:, 1])
    return p1.reshape(Bn * seq_len, C), p2.reshape(Bn * seq_len, C)


def _cum_kernel(u_ref, sc_ref, wft_ref, bf_ref, o_ref, carry_ref):
    t = pl.program_id(1)

    @pl.when(t == 0)
    def _():
        carry_ref[...] = jnp.zeros_like(carry_ref)

    z = lax.dot_general(wft_ref[...], u_ref[...], _NT, preferred_element_type=F32) * sc_ref[...] + bf_ref[...]
    c = _tri_cumsum(_log_sigmoid(z)) + carry_ref[...]
    o_ref[...] = c
    carry_ref[...] = c[:, c.shape[1] - 1:c.shape[1]]


def _fox_cum(u, scale, wft, bf, Bn, T, tk):
    M, D = u.shape
    H = wft.shape[0]
    nt = T // tk
    return pl.pallas_call(
        _cum_kernel,
        grid=(Bn, nt),
        in_specs=[pl.BlockSpec((tk, D), lambda b, t: (b * nt + t, 0)),
                  pl.BlockSpec((None, 1, tk), lambda b, t: (b * nt + t, 0, 0)),
                  pl.BlockSpec((H, D), lambda b, t: (0, 0)),
                  pl.BlockSpec((H, 1), lambda b, t: (0, 0))],
        out_specs=pl.BlockSpec((None, None, H, tk), lambda b, t: (b, t, 0, 0)),
        out_shape=jax.ShapeDtypeStruct((Bn, nt, H, tk), F32),
        scratch_shapes=[pltpu.VMEM((H, 1), F32)],
        compiler_params=_cparams("arbitrary", "arbitrary"),
        name="fox_cum",
    )(u, scale.reshape(Bn * nt, 1, tk), wft, bf.reshape(H, 1).astype(F32))


def _fox_attn_kernel(q_ref, k_ref, v_ref, cum_ref, o_ref, m_ref, l_ref, acc_ref, *, G, tq, tk, scale):
    qi = pl.program_id(2)
    hd = q_ref.shape[-1]
    reps = tk // LANE
    m_ref[...] = jnp.full(m_ref.shape, NEG_INF, F32)
    l_ref[...] = jnp.zeros(l_ref.shape, F32)
    acc_ref[...] = jnp.zeros(acc_ref.shape, F32)
    n_full = (qi * tq) // tk

    q = q_ref[...].reshape(G * tq, hd)

    def tile(j, masked):
        start = pl.multiple_of(j * tk, tk)
        kt = k_ref[pl.ds(start, tk), :]
        vt = v_ref[pl.ds(start, tk), :]
        s = lax.dot_general(q, kt, _NT, preferred_element_type=F32) * (scale * LOG2E)
        s = s.reshape(G, tq, tk) - cum_ref[j] * LOG2E
        if masked:
            row = lax.broadcasted_iota(I32, (tq, tk), 0) + qi * tq
            col = lax.broadcasted_iota(I32, (tq, tk), 1) + j * tk
            s = jnp.where((col <= row)[None], s, NEG_INF)
        s = s.reshape(G * tq, tk)
        m_prev = m_ref[...]
        m_new = jnp.maximum(m_prev, jnp.max(s, axis=1, keepdims=True))
        p = jnp.exp2(s - jnp.concatenate([m_new] * reps, axis=1))
        alpha = jnp.exp2(m_prev - m_new)
        l_ref[...] = alpha * l_ref[...] + jnp.sum(p, axis=1, keepdims=True)
        acc_ref[...] = alpha * acc_ref[...] + jnp.dot(p.astype(BF16), vt, preferred_element_type=F32)
        m_ref[...] = m_new

    def body(j, carry):
        tile(j, False)
        return carry

    lax.fori_loop(0, n_full, body, 0)
    tile(n_full, True)
    out = acc_ref[...] / l_ref[...]
    for g in range(G):
        o_ref[:, g * hd:(g + 1) * hd] = out[g * tq:(g + 1) * tq, :].astype(o_ref.dtype)


def _fox_attn_prompt(q_hm, k16, v16, cum, Bn, T, G, scale, tq, tk):
    H, M, hd = q_hm.shape
    assert hd == LANE and tk % tq == 0
    KVH = H // G
    nq = T // tq
    return pl.pallas_call(
        functools.partial(_fox_attn_kernel, G=G, tq=tq, tk=tk, scale=scale),
        grid=(Bn, KVH, nq),
        in_specs=[pl.BlockSpec((G, tq, hd), lambda b, h, i: (h, b * nq + i, 0)),
                  pl.BlockSpec((T, hd), lambda b, h, i: (b, h)),
                  pl.BlockSpec((T, hd), lambda b, h, i: (b, h)),
                  pl.BlockSpec((None, None, T // tk, G, 1, tk), lambda b, h, i: (b, h, 0, 0, 0, 0))],
        out_specs=pl.BlockSpec((tq, G * hd), lambda b, h, i: (b * nq + i, h)),
        out_shape=jax.ShapeDtypeStruct((M, H * hd), BF16),
        scratch_shapes=[pltpu.VMEM((G * tq, LANE), F32), pltpu.VMEM((G * tq, LANE), F32),
                        pltpu.VMEM((G * tq, hd), F32)],
        compiler_params=_cparams("parallel", "parallel", "arbitrary"),
        name="fox_attn_prompt",
    )(q_hm, k16, v16, cum)


def _paged_attn_kernel(*refs, kind, P, KVH, G, Q, page, hd, scale):
    pt_ref = refs[0]
    del pt_ref
    pos = 1
    q_ref = refs[pos]; pos += 1
    kp_refs = refs[pos:pos + P]; pos += P
    vp_refs = refs[pos:pos + P]; pos += P
    kn_ref, vn_ref = refs[pos], refs[pos + 1]; pos += 2
    if kind == "fox":
        lf_refs = refs[pos:pos + P]; pos += P
        lfn_ref = refs[pos]; pos += 1
    else:
        mp_ref, mn_ref, tl_ref, tn_ref = refs[pos:pos + 4]; pos += 4
    o_ref = refs[pos]; pos += 1
    kb_ref, vb_ref, m_ref, l_ref, acc_ref = refs[pos:pos + 5]; pos += 5
    if kind == "fox":
        cum_ref, carry_ref = refs[pos:pos + 2]

    jb = pl.program_id(1)
    nb = pl.num_programs(1)
    R = G * Q

    @pl.when(jb == 0)
    def _():
        m_ref[...] = jnp.full(m_ref.shape, NEG_INF, F32)
        l_ref[...] = jnp.zeros(l_ref.shape, F32)
        acc_ref[...] = jnp.zeros(acc_ref.shape, F32)
        if kind == "fox":
            carry_ref[...] = jnp.zeros(carry_ref.shape, F32)

    for p in range(P):
        for kvh in range(KVH):
            kb_ref[kvh, p * page:(p + 1) * page, :] = kp_refs[p][pl.ds(kvh, page, stride=KVH), :].astype(BF16)
            vb_ref[kvh, p * page:(p + 1) * page, :] = vp_refs[p][pl.ds(kvh, page, stride=KVH), :].astype(BF16)
        if kind == "fox":
            c = _tri_cumsum(lf_refs[p][...]) + carry_ref[...]
            cum_ref[:, p * page:(p + 1) * page] = c
            carry_ref[...] = c[:, page - 1:page]

    def logits(keys_of):
        return jnp.concatenate(
            [lax.dot_general(q_ref[kvh], keys_of(kvh), _NT, preferred_element_type=F32) for kvh in range(KVH)],
            axis=0) * (scale * LOG2E)

    def update(s, vals_of):
        m_prev = m_ref[...]
        m_new = jnp.maximum(m_prev, jnp.max(s, axis=1, keepdims=True))
        pr = jnp.exp2(s - jnp.concatenate([m_new] * (s.shape[1] // LANE), axis=1))
        alpha = jnp.exp2(m_prev - m_new)
        l_ref[...] = alpha * l_ref[...] + jnp.sum(pr, axis=1, keepdims=True)
        p16 = pr.astype(BF16)
        pv = jnp.concatenate(
            [jnp.dot(p16[kvh * R:(kvh + 1) * R, :], vals_of(kvh), preferred_element_type=F32) for kvh in range(KVH)],
            axis=0)
        acc_ref[...] = alpha * acc_ref[...] + pv
        m_ref[...] = m_new

    def head_rows(tile):
        return jnp.concatenate(
            [jnp.broadcast_to(tile[h:h + 1, :], (Q, tile.shape[1])) for h in range(KVH * G)], axis=0)

    def query_rows(tile):
        return jnp.concatenate([tile] * (KVH * G), axis=0)

    n = P * page
    s = logits(lambda kvh: kb_ref[kvh])
    if kind == "fox":
        s = s - head_rows(cum_ref[...]) * LOG2E
    else:
        s = s + query_rows(mp_ref[...])
        is_last = jnp.where(jb == nb - 1, LOG2E, 0.0)
        tail = s[:, n - page:] + is_last * tl_ref[...]
        s = tail if n == page else jnp.concatenate([s[:, :n - page], tail], axis=1)
    update(s, lambda kvh: vb_ref[kvh])

    @pl.when(jb == nb - 1)
    def _():
        kn = kn_ref[...].astype(BF16)
        vn = vn_ref[...].astype(BF16)
        qpos = lax.broadcasted_iota(I32, (KVH * R, page), 0) % Q
        kpos = lax.broadcasted_iota(I32, (KVH * R, page), 1)
        s = logits(lambda kvh: kn[:, kvh * hd:(kvh + 1) * hd])
        if kind == "fox":
            s = s - head_rows(_tri_cumsum(lfn_ref[...]) + carry_ref[...]) * LOG2E
        else:
            s = s + query_rows(mn_ref[...]) + tn_ref[...] * LOG2E
        s = jnp.where(kpos <= qpos, s, NEG_INF)
        update(s, lambda kvh: vn[:, kvh * hd:(kvh + 1) * hd])
        out = acc_ref[...] / l_ref[...]
        for kvh in range(KVH):
            o_ref[kvh] = out[kvh * R:(kvh + 1) * R, :]


def _paged_attn(kind, q, pool_k, pool_v, layer, k_new, v_new, page_table, extras, scale):
    Bn, KVH, R, hd = q.shape
    page = pool_k.shape[2] // KVH
    n_pages = page_table.shape[1]
    P = min(PAGES_PER_STEP, n_pages)
    assert n_pages % P == 0
    nb = n_pages // P
    H = None
    if kind == "fox":
        H = extras[0].shape[2]
        G = H // KVH
    else:
        G = extras[2].shape[0] // (KVH * extras[0].shape[1])
    Q = R // G

    def page_map(p, nd):
        return lambda b, j, pt: (layer, pt[b, j * P + p]) + (0,) * nd

    in_specs = [pl.BlockSpec((None, KVH, R, hd), lambda b, j, pt: (b, 0, 0, 0))]
    args = [q]
    for pool in (pool_k, pool_v):
        for p in range(P):
            in_specs.append(pl.BlockSpec((None, None, page * KVH, hd), page_map(p, 2)))
            args.append(pool)
    for new in (k_new, v_new):
        in_specs.append(pl.BlockSpec((None, page, KVH * hd), lambda b, j, pt: (b, 0, 0)))
        args.append(new)
    scratch = [pltpu.VMEM((KVH, P * page, hd), BF16), pltpu.VMEM((KVH, P * page, hd), BF16),
               pltpu.VMEM((KVH * R, LANE), F32), pltpu.VMEM((KVH * R, LANE), F32),
               pltpu.VMEM((KVH * R, hd), F32)]
    if kind == "fox":
        pool_lft, lf_newt = extras
        for p in range(P):
            in_specs.append(pl.BlockSpec((None, None, H, page), page_map(p, 2)))
            args.append(pool_lft)
        in_specs.append(pl.BlockSpec((None, H, page), lambda b, j, pt: (b, 0, 0)))
        args.append(lf_newt)
        scratch += [pltpu.VMEM((H, P * page), F32), pltpu.VMEM((H, 1), F32)]
    else:
        madd_past, madd_new, bias_last, bias_new = extras
        in_specs.append(pl.BlockSpec((None, Q, P * page), lambda b, j, pt: (b, 0, j)))
        args.append(madd_past)
        in_specs.append(pl.BlockSpec((None, Q, page), lambda b, j, pt: (b, 0, 0)))
        args.append(madd_new)
        for t in (bias_last, bias_new):
            in_specs.append(pl.BlockSpec((KVH * R, page), lambda b, j, pt: (0, 0)))
            args.append(t)
    grid_spec = pltpu.PrefetchScalarGridSpec(
        num_scalar_prefetch=1,
        grid=(Bn, nb),
        in_specs=in_specs,
        out_specs=pl.BlockSpec((None, KVH, R, hd), lambda b, j, pt: (b, 0, 0, 0)),
        scratch_shapes=scratch,
    )
    return pl.pallas_call(
        functools.partial(_paged_attn_kernel, kind=kind, P=P, KVH=KVH, G=G, Q=Q, page=page, hd=hd, scale=scale),
        grid_spec=grid_spec,
        out_shape=jax.ShapeDtypeStruct((Bn, KVH, R, hd), F32),
        compiler_params=_cparams("arbitrary", "arbitrary"),
        name=kind + "_attn_sample",
    )(page_table, *args)


def _sortable(s):
    bits = pltpu.bitcast(s + 0.0, I32)
    return jnp.where(bits >= 0, bits, bits ^ jnp.int32(0x7FFFFFFF))


def _kth_largest_key(count_ge, rows, k):
    sign = jnp.int32(-2 ** 31)

    def body(i, t):
        cand = t | jnp.left_shift(jnp.int32(1), 31 - i)
        cnt = count_ge(cand ^ sign)
        return jnp.where(cnt >= k, cand, t)

    t = lax.fori_loop(0, 32, body, jnp.zeros((rows, 1), I32))
    return t ^ sign


def _last_tie_index(ties_before, need, rows, n_keys):
    nbits = max(int(n_keys - 1).bit_length(), 1)

    def body(i, m):
        cand = m | jnp.left_shift(jnp.int32(1), nbits - 1 - i)
        return jnp.where(ties_before(cand) < need, cand, m)

    return lax.fori_loop(0, nbits, body, jnp.zeros((rows, 1), I32))


def _dsa_score_kernel(qi_ref, ki_ref, wi_ref, o_ref, acc_ref, wb_ref, *, HI, tq, T, q0, k_top, scale):
    wi = wi_ref[...]
    for h in range(HI):
        wb_ref[h] = jnp.broadcast_to(wi[:, h:h + 1], (tq, LANE))
    acc_ref[...] = jnp.zeros(acc_ref.shape, F32)
    ki = ki_ref[...]
    reps = T // LANE

    def body(h, carry):
        d = lax.dot_general(qi_ref[h], ki, _NT, preferred_element_type=F32)
        w = jnp.concatenate([wb_ref[h]] * reps, axis=1)
        acc_ref[...] += w * jnp.maximum(d, 0.0)
        return carry

    lax.fori_loop(0, HI, body, 0)
    qpos = lax.broadcasted_iota(I32, (tq, T), 0) + q0
    kpos = lax.broadcasted_iota(I32, (tq, T), 1)
    adm = kpos <= qpos
    key = _sortable(jnp.where(adm, acc_ref[...] * scale, NEG_INF))

    def count_ge(thr):
        return jnp.sum(jnp.where(key >= thr, 1, 0), axis=-1, keepdims=True)

    thr = _kth_largest_key(count_ge, tq, k_top)
    sel = jnp.logical_and(key >= thr, adm)
    o_ref[...] = jnp.where(sel, 0.0, NEG_INF)

    @pl.when(jnp.max(jnp.sum(jnp.where(sel, 1, 0), axis=-1, keepdims=True)) > k_top)
    def _():
        gt = key > thr
        tie = key == thr
        need = k_top - jnp.sum(jnp.where(gt, 1, 0), axis=-1, keepdims=True)

        def ties_before(m):
            return jnp.sum(jnp.where(jnp.logical_and(tie, kpos < m), 1, 0), axis=-1, keepdims=True)

        last = _last_tie_index(ties_before, need, tq, T)
        keep = jnp.logical_or(gt, jnp.logical_and(tie, kpos <= last))
        o_ref[...] = jnp.where(jnp.logical_and(keep, adm), 0.0, NEG_INF)


def _dsa_select_prompt(qi_hm, ki16, wi, Bn, T, c, tq, k_top, scale):
    HI, M, DI = qi_hm.shape
    nq = T // tq
    tk = (c + 1) * tq
    wl = wi.shape[1]
    return pl.pallas_call(
        functools.partial(_dsa_score_kernel, HI=HI, tq=tq, T=tk, q0=c * tq, k_top=k_top, scale=scale),
        grid=(Bn,),
        in_specs=[pl.BlockSpec((HI, tq, DI), lambda b: (0, b * nq + c, 0)),
                  pl.BlockSpec((None, tk, DI), lambda b: (b, 0, 0)),
                  pl.BlockSpec((tq, wl), lambda b: (b * nq + c, 0))],
        out_specs=pl.BlockSpec((None, tq, tk), lambda b: (b, 0, 0)),
        out_shape=jax.ShapeDtypeStruct((Bn, tq, tk), F32),
        scratch_shapes=[pltpu.VMEM((tq, tk), F32), pltpu.VMEM((HI, tq, LANE), F32)],
        compiler_params=_cparams("parallel"),
        name="dsa_select_prompt",
    )(qi_hm, ki16, wi)


def _dsa_attn_kernel(q_ref, k_ref, v_ref, madd_ref, td_ref, tl_ref, o_ref, s_ref, *, G, tq, iq0, scale):
    iq = iq0 + pl.program_id(1)
    hd = q_ref.shape[-1]
    k = k_ref[...]
    v = v_ref[...]
    madd = madd_ref[...]
    bt = td_ref.shape[-1]
    for g in range(G):
        s_ref[g] = lax.dot_general(q_ref[g], k, _NT, preferred_element_type=F32) * (scale * LOG2E) + madd
        for r in range(tq // bt):
            rows = slice(r * bt, (r + 1) * bt)
            a = iq * (tq // bt) + r
            s_ref[g, rows, pl.ds(pl.multiple_of(a * bt, bt), bt)] += td_ref[g] * LOG2E

            @pl.when(a > 0)
            def _():
                s_ref[g, rows, pl.ds(pl.multiple_of(jnp.maximum(a - 1, 0) * bt, bt), bt)] += tl_ref[g] * LOG2E

        s = s_ref[g]
        m = jnp.max(s, axis=-1, keepdims=True)
        p = jnp.exp2(s - m)
        l = jnp.sum(p, axis=-1, keepdims=True)
        o = jnp.dot(p.astype(BF16), v, preferred_element_type=F32) / l
        o_ref[:, g * hd:(g + 1) * hd] = o.astype(o_ref.dtype)


def _dsa_attn_prompt(q_hm, k16, v16, madd, t_diag, t_left, Bn, T, c, tqs, tq, G, scale):
    H, M, hd = q_hm.shape
    KVH = H // G
    tk = (c + 1) * tqs
    nsub = tqs // tq
    nq = T // tq
    return pl.pallas_call(
        functools.partial(_dsa_attn_kernel, G=G, tq=tq, iq0=c * nsub, scale=scale),
        grid=(Bn, nsub, KVH),
        in_specs=[pl.BlockSpec((G, tq, hd), lambda b, i, h: (h, b * nq + c * nsub + i, 0)),
                  pl.BlockSpec((None, tk, hd), lambda b, i, h: (b, 0, h)),
                  pl.BlockSpec((None, tk, hd), lambda b, i, h: (b, 0, h)),
                  pl.BlockSpec((None, tq, tk), lambda b, i, h: (b, i, 0)),
                  pl.BlockSpec((G,) + t_diag.shape[1:], lambda b, i, h: (h, 0, 0)),
                  pl.BlockSpec((G,) + t_left.shape[1:], lambda b, i, h: (h, 0, 0))],
        out_specs=pl.BlockSpec((None, tq, G * hd), lambda b, i, h: (b, i, h)),
        out_shape=jax.ShapeDtypeStruct((Bn, tqs, H * hd), BF16),
        scratch_shapes=[pltpu.VMEM((G, tq, tk), F32)],
        compiler_params=_cparams("parallel", "parallel", "arbitrary"),
        name="dsa_attn_prompt",
    )(q_hm, k16, v16, madd, t_diag, t_left)


def _dsa_score_sample_kernel(*refs, P, HI, Q, page, k_top, scale):
    pt_ref = refs[0]
    del pt_ref
    qi_ref, wi_ref = refs[1], refs[2]
    kp_refs = refs[3:3 + P]
    kn_ref = refs[3 + P]
    mp_ref, mn_ref = refs[4 + P], refs[5 + P]
    kb_ref, sp_ref, sn_ref = refs[6 + P:9 + P]

    jb = pl.program_id(1)
    nb = pl.num_programs(1)
    n = P * page
    qi = qi_ref[...]
    wcol = wi_ref[...]

    def score(keys16):
        d = lax.dot_general(qi, keys16, _NT, preferred_element_type=F32)
        d = jnp.broadcast_to(wcol, d.shape) * jnp.maximum(d, 0.0)
        tot = d[0:Q]
        for h in range(1, HI):
            tot = tot + d[h * Q:(h + 1) * Q]
        return tot * scale

    for p in range(P):
        kb_ref[p * page:(p + 1) * page, :] = kp_refs[p][...].astype(BF16)
    sp_ref[jb] = score(kb_ref[...])

    @pl.when(jb == nb - 1)
    def _():
        qpos = lax.broadcasted_iota(I32, (Q, page), 0)
        kpos = lax.broadcasted_iota(I32, (Q, page), 1)
        adm = kpos <= qpos
        sn_ref[...] = jnp.where(adm, score(kn_ref[...].astype(BF16)), NEG_INF)
        key_p = _sortable(sp_ref[...])
        key_n = _sortable(sn_ref[...])

        def count_ge(thr):
            cp = jnp.sum(jnp.where(key_p >= thr[None], 1, 0), axis=-1, keepdims=True)
            cn = jnp.sum(jnp.where(key_n >= thr, 1, 0), axis=-1, keepdims=True)
            return jnp.sum(cp, axis=0) + cn

        thr = _kth_largest_key(count_ge, Q, k_top)
        sel_n = jnp.logical_and(key_n >= thr, adm)
        mp_ref[...] = jnp.where(key_p >= thr[None], 0.0, NEG_INF)
        mn_ref[...] = jnp.where(sel_n, 0.0, NEG_INF)

        def count(mask_p, mask_n):
            cp = jnp.sum(jnp.where(mask_p, 1, 0), axis=-1, keepdims=True)
            return jnp.sum(cp, axis=0) + jnp.sum(jnp.where(mask_n, 1, 0), axis=-1, keepdims=True)

        @pl.when(jnp.max(count(key_p >= thr[None], sel_n)) > k_top)
        def _():
            idx_p = (lax.broadcasted_iota(I32, key_p.shape, 0) * n + lax.broadcasted_iota(I32, key_p.shape, 2))
            n_past = key_p.shape[0] * n
            idx_n = kpos + n_past
            gt_p, gt_n = key_p > thr[None], key_n > thr
            tie_p, tie_n = key_p == thr[None], key_n == thr
            need = k_top - count(gt_p, gt_n)

            def ties_before(m):
                return count(jnp.logical_and(tie_p, idx_p < m[None]), jnp.logical_and(tie_n, idx_n < m))

            last = _last_tie_index(ties_before, need, Q, n_past + page)
            keep_p = jnp.logical_or(gt_p, jnp.logical_and(tie_p, idx_p <= last[None]))
            keep_n = jnp.logical_or(gt_n, jnp.logical_and(tie_n, idx_n <= last))
            mp_ref[...] = jnp.where(keep_p, 0.0, NEG_INF)
            mn_ref[...] = jnp.where(jnp.logical_and(keep_n, adm), 0.0, NEG_INF)


def _dsa_select_sample(qi, wi, pool_ki, layer, ki_new, page_table, Q, k_top, scale):
    Bn, RQ, DI = qi.shape
    page = pool_ki.shape[2]
    n_pages = page_table.shape[1]
    P = min(PAGES_PER_STEP, n_pages)
    nb = n_pages // P
    HI = RQ // Q
    in_specs = [pl.BlockSpec((None, RQ, DI), lambda b, j, pt: (b, 0, 0)),
                pl.BlockSpec((None, RQ, 1), lambda b, j, pt: (b, 0, 0))]
    args = [qi, wi]
    for p in range(P):
        in_specs.append(pl.BlockSpec((None, None, page, DI), functools.partial(
            lambda b, j, pt, p: (layer, pt[b, j * P + p], 0, 0), p=p)))
        args.append(pool_ki)
    in_specs.append(pl.BlockSpec((None, page, DI), lambda b, j, pt: (b, 0, 0)))
    args.append(ki_new)
    grid_spec = pltpu.PrefetchScalarGridSpec(
        num_scalar_prefetch=1,
        grid=(Bn, nb),
        in_specs=in_specs,
        out_specs=[pl.BlockSpec((None, nb, Q, P * page), lambda b, j, pt: (b, 0, 0, 0)),
                   pl.BlockSpec((None, Q, page), lambda b, j, pt: (b, 0, 0))],
        scratch_shapes=[pltpu.VMEM((P * page, DI), BF16), pltpu.VMEM((nb, Q, P * page), F32),
                        pltpu.VMEM((Q, page), F32)],
    )
    return pl.pallas_call(
        functools.partial(_dsa_score_sample_kernel, P=P, HI=HI, Q=Q, page=page, k_top=k_top, scale=scale),
        grid_spec=grid_spec,
        out_shape=[jax.ShapeDtypeStruct((Bn, nb, Q, P * page), F32), jax.ShapeDtypeStruct((Bn, Q, page), F32)],
        compiler_params=_cparams("arbitrary", "arbitrary"),
        name="dsa_select_sample",
    )(page_table, *args)


def _t5_bucket_np(dist):
    max_exact = N_BUCKETS // 2
    d = np.maximum(dist, 0)
    ratio = np.log(np.maximum(d, 1).astype(np.float32) / np.float32(max_exact)) / np.float32(
        math.log(MAX_DISTANCE / max_exact))
    large = np.minimum(max_exact + (ratio * (N_BUCKETS - max_exact)).astype(np.int32), N_BUCKETS - 1)
    return np.where(d < max_exact, d, large).astype(np.int32)


def _pad_cols(w, n):
    return jnp.pad(w, ((0, 0), (0, n - w.shape[1])))


def _to_rows(o, Bn, Q, KVH, G, hd):
    o = o.reshape(Bn, KVH, G, Q, hd)
    return jnp.transpose(o, (0, 3, 1, 2, 4)).reshape(Bn * Q, KVH * G * hd)


def _to_heads(q, Bn, Q, KVH, G, hd):
    q = q.reshape(Bn, Q, KVH, G, hd)
    return jnp.transpose(q, (0, 2, 3, 1, 4)).reshape(Bn, KVH, G * Q, hd)


def _pad_page(x, page):
    return jnp.pad(x, ((0, 0), (0, page - x.shape[1]), (0, 0)))


def _next_norm(hp, pg, psq, hs, sg, ssq):
    n = hp.shape[1]
    if psq.ndim == 3:
        psq, ssq = jnp.sum(psq, axis=0), jnp.sum(ssq, axis=0)
    return hp, hs, (pg, sg, (lax.rsqrt(psq / n + RMS_EPS), lax.rsqrt(ssq / n + RMS_EPS)))


def _fox_layer(hp, hs, xin, next_gain, dims, layer, pool_k, pool_v, pool_lft, page_table,
               w_q, w_k, w_v, w_f, b_f, w_o):
    up, us, rs = xin
    Bp, T, Bs, Q, H, KVH, hd = dims
    G = H // KVH
    scale = hd ** -0.5
    page = pool_k.shape[2] // KVH
    wf_pad = _pad_cols(w_f[layer], LANE)[None]
    bf_pad = jnp.pad(b_f[layer], (0, LANE - H))
    wft = w_f[layer].T.astype(BF16)
    tq, tk = min(ATTN_TILE, T // 2), min(ATTN_TILE, T)

    q_hm, qs = _mmw(up, w_q, layer, xs=us, scale=rs, emit_f32=False, emit_bf16=True, head_major=True, name="fox_q")
    k32, k16, ks = _mmw(up, w_k, layer, xs=us, scale=rs, emit_bf16=True, name="fox_k")
    v32, v16, vs = _mmw(up, w_v, layer, xs=us, scale=rs, emit_bf16=True, name="fox_v")
    lf_pad, lfs_pad = _mmw(up, wf_pad, 0, xs=us, scale=rs, ls_bias=bf_pad, tn=LANE, name="fox_logf")
    lfs = lfs_pad[:, :H]

    cum = _fox_cum(up, rs[0], wft, b_f[layer], Bp, T, tk)
    cum = jnp.transpose(cum.reshape(Bp, T // tk, KVH, G, 1, tk), (0, 2, 1, 3, 4, 5))
    o = _fox_attn_prompt(q_hm, k16, v16, cum, Bp, T, G, scale, tq, tk)

    q_t = _to_heads(qs, Bs, Q, KVH, G, hd).astype(BF16)
    lf_newt = jnp.transpose(_pad_page(lfs.reshape(Bs, Q, H), page), (0, 2, 1))
    o_s = _paged_attn("fox", q_t, pool_k, pool_v, layer,
                      _pad_page(ks.reshape(Bs, Q, KVH * hd), page), _pad_page(vs.reshape(Bs, Q, KVH * hd), page),
                      page_table, (pool_lft, lf_newt), scale)
    o_s = _to_rows(o_s, Bs, Q, KVH, G, hd).astype(BF16)

    hp, hs, xnext = _next_norm(*_mmw(o, w_o, layer, xs=o_s, res=hp, res_s=hs, next_gain=next_gain, name="fox_o"))
    rows = (k32.reshape(Bp, T, KVH, hd), v32.reshape(Bp, T, KVH, hd), lf_pad[:, :H].reshape(Bp, T, H),
            ks.reshape(Bs, Q, KVH, hd), vs.reshape(Bs, Q, KVH, hd), lfs.reshape(Bs, Q, H))
    return hp, hs, xnext, rows


def _sconv_layer(hp, hs, xin, next_gain, dims, layer, state, w_in, w_conv, w_out):
    up, us, rs = xin
    Bp, T, Bs, Q = dims[:4]
    D = w_out.shape[1]
    zp, stp, zs, sts = _gconv(up, us, rs, w_in, layer, (0, D, 2 * D), D, w_conv, _conv_prefix(state[layer], Q),
                              mode="sconv", seq_len=T, tail_len=Q, tm=SCONV_ROW_TILE, name="sconv_in")
    hp, hs, xnext = _next_norm(*_mmw(zp, w_out, layer, xs=zs, res=hp, res_s=hs, next_gain=next_gain,
                                     name="sconv_out"))
    nblk = stp.shape[0] // Bp
    sp = stp.reshape(Bp, nblk, SUBLANE, D)[:, -1, SUBLANE - 2:, :]
    ss = sts.reshape(Bs, Q, D)[:, Q - 2:, :]
    return hp, hs, xnext, sp, ss


def _dsa_layer(hp, hs, xin, next_gain, dims, layer, pool_k, pool_v, pool_ki, page_table, rel_bias,
               w_q, w_k, w_v, w_o, w_qi, w_ki, w_wi):
    up, us, rs = xin
    Bp, T, Bs, Q, H, KVH, hd = dims
    G = H // KVH
    scale = hd ** -0.5
    page = pool_k.shape[2] // KVH
    DI = w_ki.shape[2]
    HI = w_wi.shape[2]
    idx_scale = (DI * HI) ** -0.5
    past = page_table.shape[1] * page
    wkw = jnp.concatenate([w_ki[layer], _pad_cols(w_wi[layer], LANE)], axis=1)[None]
    c_far = rel_bias[N_BUCKETS - 1]

    q_hm, qs = _mmw(up, w_q, layer, xs=us, scale=rs, emit_f32=False, emit_bf16=True, head_major=True, name="dsa_q")
    k32, k16, ks = _mmw(up, w_k, layer, xs=us, scale=rs, emit_bf16=True, name="dsa_k")
    v32, v16, vs = _mmw(up, w_v, layer, xs=us, scale=rs, emit_bf16=True, name="dsa_v")
    qi_hm, qis = _mmw(up, w_qi, layer, xs=us, scale=rs, emit_f32=False, emit_bf16=True, head_major=True,
                      name="dsa_qi")
    kw, kws = _mmw(up, wkw, 0, xs=us, scale=rs, tn=DI + LANE, name="dsa_kiwi")

    tqs = min(ATTN_TILE, T // 2)
    tq = min(ATTN_TILE, tqs)
    assert tq >= MAX_DISTANCE and tqs % tq == 0 and T % tqs == 0
    ki32 = kw[:, :DI]
    wi = kw[:, DI:]
    ki16 = ki32.astype(BF16).reshape(Bp, T, DI)
    k16 = k16.reshape(Bp, T, KVH * hd)
    v16 = v16.reshape(Bp, T, KVH * hd)
    bt = MAX_DISTANCE
    ii = np.arange(bt)[:, None]
    jj = np.arange(bt)[None, :]
    t_diag = jnp.transpose(rel_bias[_t5_bucket_np(ii - jj)] - c_far, (2, 0, 1))
    t_left = jnp.transpose(rel_bias[_t5_bucket_np(bt + ii - jj)] - c_far, (2, 0, 1))
    k_top = min(TOPK_MAX, T // 4)
    o_blocks = []
    for c in range(T // tqs):
        madd = _dsa_select_prompt(qi_hm, ki16, wi, Bp, T, c, tqs, k_top, idx_scale)
        o_blocks.append(_dsa_attn_prompt(q_hm, k16, v16, madd, t_diag, t_left, Bp, T, c, tqs, tq, G, scale))
    o = jnp.concatenate(o_blocks, axis=1).reshape(Bp * T, H * hd)

    kis = kws[:, :DI]
    wis = kws[:, DI:DI + HI]
    qi_t = jnp.transpose(qis.reshape(Bs, Q, HI, DI), (0, 2, 1, 3)).reshape(Bs, HI * Q, DI).astype(BF16)
    wi_t = jnp.transpose(wis.reshape(Bs, Q, HI), (0, 2, 1)).reshape(Bs, HI * Q, 1)
    mp, mn = _dsa_select_sample(qi_t, wi_t, pool_ki, layer, _pad_page(kis.reshape(Bs, Q, DI), page), page_table,
                                Q, min(TOPK_MAX, (past + Q) // 4), idx_scale)
    nb, n = mp.shape[1], mp.shape[3]
    madd_past = jnp.transpose(mp, (0, 2, 1, 3)).reshape(Bs, Q, nb * n)
    qq = np.arange(Q)[:, None]
    cc = np.arange(page)[None, :]

    def sample_bias(dist):
        t = rel_bias[_t5_bucket_np(dist)] - c_far
        t = jnp.transpose(t.reshape(Q, page, KVH, G), (2, 3, 0, 1))
        return t.reshape(KVH * G * Q, page)

    bias_last = sample_bias(past + qq - (past - page + cc))
    bias_new = sample_bias(qq - cc)
    q_t = _to_heads(qs, Bs, Q, KVH, G, hd).astype(BF16)
    o_s = _paged_attn("dsa", q_t, pool_k, pool_v, layer,
                      _pad_page(ks.reshape(Bs, Q, KVH * hd), page), _pad_page(vs.reshape(Bs, Q, KVH * hd), page),
                      page_table, (madd_past, mn, bias_last, bias_new), scale)
    o_s = _to_rows(o_s, Bs, Q, KVH, G, hd).astype(BF16)

    hp, hs, xnext = _next_norm(*_mmw(o, w_o, layer, xs=o_s, res=hp, res_s=hs, next_gain=next_gain, name="dsa_o"))
    rows = (k32.reshape(Bp, T, KVH, hd), v32.reshape(Bp, T, KVH, hd), ki32.reshape(Bp, T, DI),
            ks.reshape(Bs, Q, KVH, hd), vs.reshape(Bs, Q, KVH, hd), kis.reshape(Bs, Q, DI))
    return hp, hs, xnext, rows


def _ffn_layer(hp, hs, xin, next_gain, dims, layer, state, w_up, w_conv, w_down16):
    vp, vs, rs = xin
    Bp, T, Bs, Q = dims[:4]
    DFF = w_down16.shape[1]
    ap, stp, a_s, sts = _gconv(vp, vs, rs, w_up, layer, (0, DFF), DFF, w_conv, _conv_prefix(state[layer], Q),
                               mode="ffn", seq_len=T, tail_len=Q, tm=ROW_TILE, name="ffn_up")
    if next_gain is None:
        hp = _mm(ap, w_down16, layer, hp, name="ffn_down")
        hs = _mm(a_s, w_down16, layer, hs, name="ffn_down_s")
        xnext = None
    else:
        hp, pg, psq = _mm(ap, w_down16, layer, hp, next_gain=next_gain, name="ffn_down")
        hs, sg, ssq = _mm(a_s, w_down16, layer, hs, next_gain=next_gain, name="ffn_down_s")
        hp, hs, xnext = _next_norm(hp, pg, psq, hs, sg, ssq)
    nblk = stp.shape[0] // Bp
    cp = stp.reshape(Bp, nblk, SUBLANE, DFF)[:, -1, SUBLANE - 2:, :]
    cs = sts.reshape(Bs, Q, DFF)[:, Q - 2:, :]
    return hp, hs, xnext, cp, cs


def kernel(x_prompt, x_sample, cache_fox_k, cache_fox_v, cache_fox_logf, state_sconv, cache_dsa_k, cache_dsa_v,
           cache_dsa_kidx, state_ffn_conv, page_table, rel_bias, norm_mix, norm_ffn, norm_final,
           fox_w_q, fox_w_k, fox_w_v, fox_w_f, fox_b_f, fox_w_o, sc_w_in, sc_w_conv, sc_w_out,
           dsa_w_q, dsa_w_k, dsa_w_v, dsa_w_o, dsa_w_qi, dsa_w_ki, dsa_w_wi, ffn_w_up, ffn_w_conv, ffn_w_down):
    Bp, T, D = x_prompt.shape
    Bs, Q, _ = x_sample.shape
    depth = norm_mix.shape[0]
    KVH, hd = cache_fox_k.shape[3], cache_fox_k.shape[4]
    H = fox_w_f.shape[2]
    dims = (Bp, T, Bs, Q, H, KVH, hd)
    hp = x_prompt.reshape(Bp * T, D)
    hs = x_sample.reshape(Bs * Q, D)

    def key_head_rows(c):
        return c.reshape(c.shape[0], c.shape[1], c.shape[2] * KVH, hd)

    fox_k, fox_v, dsa_k, dsa_v = (key_head_rows(c) for c in (cache_fox_k, cache_fox_v, cache_dsa_k, cache_dsa_v))
    fox_lft = jnp.transpose(cache_fox_logf, (0, 1, 3, 2))
    w_down16 = ffn_w_down.astype(BF16)
    fox_new = ([], [], [], [], [], [])
    sc_new = ([], [])
    dsa_new = ([], [], [], [], [], [])
    ffn_new = ([], [])
    ones = (jnp.ones((Bp * T, 1), F32), jnp.ones((Bs * Q, 1), F32))
    xin = (_rmsnorm(hp, norm_mix[0], BF16), _rmsnorm(hs, norm_mix[0], BF16), ones)
    for i in range(depth):
        j, kind = i // 3, i % 3
        if kind == 0:
            hp, hs, xin, rows = _fox_layer(hp, hs, xin, norm_ffn[i], dims, j, fox_k, fox_v, fox_lft, page_table,
                                           fox_w_q, fox_w_k, fox_w_v, fox_w_f, fox_b_f, fox_w_o)
            for lst, a in zip(fox_new, rows):
                lst.append(a)
        elif kind == 1:
            hp, hs, xin, sp, ss = _sconv_layer(hp, hs, xin, norm_ffn[i], dims, j, state_sconv, sc_w_in, sc_w_conv,
                                               sc_w_out)
            sc_new[0].append(sp)
            sc_new[1].append(ss)
        else:
            hp, hs, xin, rows = _dsa_layer(hp, hs, xin, norm_ffn[i], dims, j, dsa_k, dsa_v, cache_dsa_kidx,
                                           page_table, rel_bias, dsa_w_q, dsa_w_k, dsa_w_v, dsa_w_o, dsa_w_qi,
                                           dsa_w_ki, dsa_w_wi)
            for lst, a in zip(dsa_new, rows):
                lst.append(a)
        hp, hs, xin, cp, cs = _ffn_layer(hp, hs, xin, norm_mix[i + 1] if i + 1 < depth else None, dims, i,
                                         state_ffn_conv, ffn_w_up, ffn_w_conv, w_down16)
        ffn_new[0].append(cp)
        ffn_new[1].append(cs)
    y_prompt = _rmsnorm(hp, norm_final, F32).reshape(Bp, T, D)
    y_sample = _rmsnorm(hs, norm_final, F32).reshape(Bs, Q, D)
    return (y_prompt, y_sample,
            jnp.stack(fox_new[0]), jnp.stack(fox_new[1]), jnp.stack(fox_new[2]),
            jnp.stack(fox_new[3]), jnp.stack(fox_new[4]), jnp.stack(fox_new[5]),
            jnp.stack(sc_new[0]), jnp.stack(sc_new[1]),
            jnp.stack(dsa_new[0]), jnp.stack(dsa_new[1]), jnp.stack(dsa_new[2]),
            jnp.stack(dsa_new[3]), jnp.stack(dsa_new[4]), jnp.stack(dsa_new[5]),
            jnp.stack(ffn_new[0]), jnp.stack(ffn_new[1]))
```

```python
import functools
import math

import numpy as np
import jax
import jax.numpy as jnp
from jax import lax
from jax.experimental import pallas as pl
from jax.experimental.pallas import tpu as pltpu

F32 = jnp.float32
BF16 = jnp.bfloat16
I32 = jnp.int32

RMS_EPS = 1e-6
NEG_INF = -1e30
TOPK_MAX = 256
N_BUCKETS = 32
MAX_DISTANCE = 128
CONV_WIDTH = 3
LOG2E = 1.4426950408889634

LANE = 128
SUBLANE = 8
VMEM_LIMIT_BYTES = 56 * 1024 * 1024
PAGES_PER_STEP = 16

ROW_TILE = 1024
COL_TILE = 512
GATED_COL_TILE = 256
GATED_ROW_CHUNK = 512
SCONV_ROW_TILE = 512
DOWN_TILE = 512
NORM_ROW_TILE = 512
ATTN_TILE = 512

_NT = (((1,), (1,)), ((), ()))


def _cparams(*sem):
    return pltpu.CompilerParams(dimension_semantics=sem, vmem_limit_bytes=VMEM_LIMIT_BYTES)


def _log_sigmoid(x):
    return jnp.minimum(x, 0.0) - jnp.log1p(jnp.exp(-jnp.abs(x)))


def _split3(x):
    hi = x.astype(BF16)
    r1 = x - hi.astype(F32)
    mid = r1.astype(BF16)
    lo = (r1 - mid.astype(F32)).astype(BF16)
    return hi, mid, lo


def _tri_cumsum(x):
    n = x.shape[-1]
    r = lax.broadcasted_iota(I32, (n, n), 0)
    c = lax.broadcasted_iota(I32, (n, n), 1)
    tri = jnp.where(r <= c, 1.0, 0.0).astype(BF16)
    hi, mid, lo = _split3(x)
    out = jnp.dot(hi, tri, preferred_element_type=F32)
    out = out + jnp.dot(mid, tri, preferred_element_type=F32)
    out = out + jnp.dot(lo, tri, preferred_element_type=F32)
    return out


def _rms_kernel(x_ref, g_ref, o_ref):
    x = x_ref[...]
    ms = jnp.mean(x * x, axis=-1, keepdims=True)
    o_ref[...] = ((x * lax.rsqrt(ms + RMS_EPS)) * g_ref[...]).astype(o_ref.dtype)


def _rmsnorm(h, g, out_dtype):
    M, D = h.shape
    tm = min(NORM_ROW_TILE, M)
    return pl.pallas_call(
        _rms_kernel,
        grid=(M // tm,),
        in_specs=[pl.BlockSpec((tm, D), lambda i: (i, 0)), pl.BlockSpec((1, D), lambda i: (0, 0))],
        out_specs=pl.BlockSpec((tm, D), lambda i: (i, 0)),
        out_shape=jax.ShapeDtypeStruct((M, D), out_dtype),
        compiler_params=_cparams("parallel"),
        name="rmsnorm",
    )(h, g.reshape(1, D).astype(F32))


def _mmw_kernel(*refs, has_res, has_ls, has_gain, has_scale, emit_f32, emit_bf16, head_major, has_tail):
    it = iter(refs)
    x_ref = next(it)
    w_ref = next(it)
    b_ref = next(it) if has_ls else None
    r_ref = next(it) if has_res else None
    g_ref = next(it) if has_gain else None
    sc_ref = next(it) if has_scale else None
    xs_ref = next(it) if has_tail else None
    rs_ref = next(it) if has_tail and has_res else None
    scs_ref = next(it) if has_tail and has_scale else None
    o32_ref = next(it) if emit_f32 else None
    o16_ref = next(it) if emit_bf16 else None
    og_ref, oq_ref = (next(it), next(it)) if has_gain else (None, None)
    os_ref = next(it) if has_tail else None
    osg_ref, osq_ref = (next(it), next(it)) if has_tail and has_gain else (None, None)
    cast = w_ref.dtype != BF16
    wbf_ref = next(it) if cast else w_ref

    def project(x, r, sc):
        acc = jnp.dot(x, wbf_ref[...], preferred_element_type=F32)
        if sc is not None:
            acc = acc * sc[...]
        if has_ls:
            acc = _log_sigmoid(acc + b_ref[...])
        if r is not None:
            acc = r[...] + acc
        return acc

    def next_norm(acc, og, oq):
        og[...] = (acc * g_ref[...]).astype(BF16)
        oq[...] = jnp.sum(acc * acc, axis=1, keepdims=True)

    @pl.when(pl.program_id(1) == 0)
    def _():
        if cast:
            wbf_ref[...] = w_ref[...].astype(BF16)
        if has_tail:
            tail = project(xs_ref[...], rs_ref, scs_ref)
            os_ref[...] = tail
            if has_gain:
                next_norm(tail, osg_ref, osq_ref)

    acc = project(x_ref[...], r_ref, sc_ref)
    if has_gain:
        next_norm(acc, og_ref, oq_ref)
    if emit_f32:
        o32_ref[...] = acc
    if emit_bf16:
        if head_major:
            for hh in range(o16_ref.shape[0]):
                o16_ref[hh] = acc[:, hh * LANE:(hh + 1) * LANE].astype(BF16)
        else:
            o16_ref[...] = acc.astype(BF16)


def _mmw(x, w, layer, *, xs=None, res=None, res_s=None, ls_bias=None, scale=None, next_gain=None, emit_f32=True,
         emit_bf16=False, head_major=False, tm=ROW_TILE, tn=COL_TILE, name="mmw"):
    M, K = x.shape
    N = w.shape[2]
    tm = min(tm, M)
    tn = min(tn, N)
    assert M % tm == 0 and N % tn == 0, (M, N, tm, tn)
    nj = N // tn
    has_tail = xs is not None
    in_specs = [pl.BlockSpec((tm, K), lambda j, i: (i, 0)), pl.BlockSpec((None, K, tn), lambda j, i: (layer, 0, j))]
    args = [x, w]
    if ls_bias is not None:
        in_specs.append(pl.BlockSpec((1, tn), lambda j, i: (0, j)))
        args.append(ls_bias.reshape(1, N).astype(F32))
    if res is not None:
        in_specs.append(pl.BlockSpec((tm, tn), lambda j, i: (i, j)))
        args.append(res)
    if next_gain is not None:
        in_specs.append(pl.BlockSpec((1, tn), lambda j, i: (0, j)))
        args.append(next_gain.reshape(1, N).astype(F32))
    if scale is not None:
        in_specs.append(pl.BlockSpec((tm, 1), lambda j, i: (i, 0)))
        args.append(scale[0])
    if has_tail:
        Ms = xs.shape[0]
        in_specs.append(pl.BlockSpec((Ms, K), lambda j, i: (0, 0)))
        args.append(xs)
        if res is not None:
            in_specs.append(pl.BlockSpec((Ms, tn), lambda j, i: (0, j)))
            args.append(res_s)
        if scale is not None:
            in_specs.append(pl.BlockSpec((Ms, 1), lambda j, i: (0, 0)))
            args.append(scale[1])
    out_specs, out_shape = [], []
    if emit_f32:
        out_specs.append(pl.BlockSpec((tm, tn), lambda j, i: (i, j)))
        out_shape.append(jax.ShapeDtypeStruct((M, N), F32))
    if emit_bf16:
        if head_major:
            out_specs.append(pl.BlockSpec((tn // LANE, tm, LANE), lambda j, i: (j, i, 0)))
            out_shape.append(jax.ShapeDtypeStruct((N // LANE, M, LANE), BF16))
        else:
            out_specs.append(pl.BlockSpec((tm, tn), lambda j, i: (i, j)))
            out_shape.append(jax.ShapeDtypeStruct((M, N), BF16))
    if next_gain is not None:
        out_specs += [pl.BlockSpec((tm, tn), lambda j, i: (i, j)), pl.BlockSpec((None, tm, 1), lambda j, i: (j, i, 0))]
        out_shape += [jax.ShapeDtypeStruct((M, N), BF16), jax.ShapeDtypeStruct((nj, M, 1), F32)]
    if has_tail:
        out_specs.append(pl.BlockSpec((Ms, tn), lambda j, i: (0, j)))
        out_shape.append(jax.ShapeDtypeStruct((Ms, N), F32))
        if next_gain is not None:
            out_specs += [pl.BlockSpec((Ms, tn), lambda j, i: (0, j)),
                          pl.BlockSpec((None, Ms, 1), lambda j, i: (j, 0, 0))]
            out_shape += [jax.ShapeDtypeStruct((Ms, N), BF16), jax.ShapeDtypeStruct((nj, Ms, 1), F32)]
    outs = pl.pallas_call(
        functools.partial(_mmw_kernel, has_res=res is not None, has_ls=ls_bias is not None,
                          has_gain=next_gain is not None, has_scale=scale is not None, emit_f32=emit_f32,
                          emit_bf16=emit_bf16, head_major=head_major, has_tail=has_tail),
        grid=(N // tn, M // tm),
        in_specs=in_specs,
        out_specs=out_specs,
        out_shape=out_shape,
        scratch_shapes=[pltpu.VMEM((K, tn), BF16)] if w.dtype != BF16 else [],
        compiler_params=_cparams("arbitrary", "arbitrary"),
        name=name,
    )(*args)
    return outs[0] if len(outs) == 1 else tuple(outs)


def _gconv_kernel(x_ref, *refs, mode, nw, tm, seq_blocks, tail_len, row_chunks):
    w_refs = refs[:nw]
    (wc_ref, xs_ref, p1_ref, p2_ref, sc_ref, scs_ref, o_ref, st_ref, os_ref, sts_ref,
     wbf_ref, buf_ref, carry_ref, bufs_ref) = refs[nw:]
    i = pl.program_id(1)
    ms = xs_ref.shape[0]

    def gated(x, sc):
        tn = wc_ref.shape[1]
        y = jnp.dot(x, wbf_ref[...], preferred_element_type=F32) * sc
        ys = [y[:, k * tn:(k + 1) * tn] for k in range(nw)]
        if mode == "ffn":
            return ys[0], ys[1]
        return ys[1] * ys[2], ys[0]

    def finish(cin, other, x1, x2):
        wc = wc_ref[...]
        y = wc[0:1, :] * x2 + wc[1:2, :] * x1 + wc[2:3, :] * cin
        if mode == "ffn":
            return ((y * (0.5 * jnp.tanh(0.5 * y) + 0.5)) * other).astype(BF16)
        return (other * y).astype(BF16)

    @pl.when(i == 0)
    def _():
        for k in range(nw):
            wbf_ref[:, k * wc_ref.shape[1]:(k + 1) * wc_ref.shape[1]] = w_refs[k][...].astype(BF16)
        cin, other = gated(xs_ref[...], scs_ref[...])
        bufs_ref[0:SUBLANE, :] = jnp.zeros((SUBLANE, cin.shape[1]), F32)
        bufs_ref[SUBLANE:SUBLANE + ms, :] = cin
        rmod = lax.broadcasted_iota(I32, (ms, 1), 0) % tail_len
        x1 = jnp.where(rmod >= 1, bufs_ref[SUBLANE - 1:SUBLANE - 1 + ms, :], p1_ref[...])
        x2 = jnp.where(rmod >= 2, bufs_ref[SUBLANE - 2:SUBLANE - 2 + ms, :], p2_ref[...])
        sts_ref[...] = cin
        os_ref[...] = finish(cin, other, x1, x2)

    first = (i % seq_blocks) == 0

    @pl.when(first)
    def _():
        buf_ref[0:SUBLANE, :] = jnp.zeros((SUBLANE, buf_ref.shape[1]), F32)

    @pl.when(jnp.logical_not(first))
    def _():
        buf_ref[0:SUBLANE, :] = carry_ref[...]

    rc = tm // row_chunks
    for c in range(row_chunks):
        lo = c * rc
        cin, other = gated(x_ref[lo:lo + rc, :], sc_ref[lo:lo + rc, :])
        buf_ref[SUBLANE + lo:SUBLANE + lo + rc, :] = cin
        o_ref[lo:lo + rc, :] = finish(cin, other, buf_ref[SUBLANE - 1 + lo:SUBLANE - 1 + lo + rc, :],
                                      buf_ref[SUBLANE - 2 + lo:SUBLANE - 2 + lo + rc, :])
    tail = buf_ref[tm:tm + SUBLANE, :]
    carry_ref[...] = tail
    st_ref[...] = tail


def _gconv(x, xs, scale, w, layer, col_offsets, nc, wconv, prefix, *, mode, seq_len, tail_len, tm,
           tn=GATED_COL_TILE, name="gconv"):
    M, K = x.shape
    Ms = xs.shape[0]
    tm = min(tm, seq_len)
    tn = min(tn, nc)
    assert M % tm == 0 and nc % tn == 0 and seq_len % tm == 0
    nw = len(col_offsets)
    in_specs = [pl.BlockSpec((tm, K), lambda j, i: (i, 0))]
    args = [x]
    for off in col_offsets:
        assert off % tn == 0
        in_specs.append(pl.BlockSpec((None, K, tn), functools.partial(lambda j, i, o: (layer, 0, o + j), o=off // tn)))
        args.append(w)
    in_specs.append(pl.BlockSpec((None, CONV_WIDTH, tn), lambda j, i: (layer, 0, j)))
    args.append(wconv)
    in_specs.append(pl.BlockSpec((Ms, K), lambda j, i: (0, 0)))
    args.append(xs)
    for p in prefix:
        in_specs.append(pl.BlockSpec((Ms, tn), lambda j, i: (0, j)))
        args.append(p)
    in_specs += [pl.BlockSpec((tm, 1), lambda j, i: (i, 0)), pl.BlockSpec((Ms, 1), lambda j, i: (0, 0))]
    args += [scale[0], scale[1]]
    out_specs = [pl.BlockSpec((tm, tn), lambda j, i: (i, j)),
                 pl.BlockSpec((None, SUBLANE, tn), lambda j, i: (i, 0, j)),
                 pl.BlockSpec((Ms, tn), lambda j, i: (0, j)),
                 pl.BlockSpec((Ms, tn), lambda j, i: (0, j))]
    out_shape = [jax.ShapeDtypeStruct((M, nc), BF16), jax.ShapeDtypeStruct((M // tm, SUBLANE, nc), F32),
                 jax.ShapeDtypeStruct((Ms, nc), BF16), jax.ShapeDtypeStruct((Ms, nc), F32)]
    return pl.pallas_call(
        functools.partial(_gconv_kernel, mode=mode, nw=nw, tm=tm, seq_blocks=seq_len // tm, tail_len=tail_len,
                          row_chunks=max(tm // GATED_ROW_CHUNK, 1)),
        grid=(nc // tn, M // tm),
        in_specs=in_specs,
        out_specs=out_specs,
        out_shape=out_shape,
        scratch_shapes=[pltpu.VMEM((K, nw * tn), BF16), pltpu.VMEM((SUBLANE + tm, tn), F32),
                        pltpu.VMEM((SUBLANE, tn), F32), pltpu.VMEM((SUBLANE + Ms, tn), F32)],
        compiler_params=_cparams("arbitrary", "arbitrary"),
        name=name,
    )(*args)


def _conv_prefix(state, seq_len):
    Bn, _, C = state.shape
    z = jnp.zeros((Bn, seq_len, C), F32)
    p1 = z.at[:, 0].set(state[:, 1])
    p2 = z.at[:, 0].set(state[:, 0]).at[:, 1].set(state[:, 1])
    return p1.reshape(Bn * seq_len, C), p2.reshape(Bn * seq_len, C)


def _cum_kernel(u_ref, sc_ref, wft_ref, bf_ref, o_ref, carry_ref):
    t = pl.program_id(1)

    @pl.when(t == 0)
    def _():
        carry_ref[...] = jnp.zeros_like(carry_ref)

    z = lax.dot_general(wft_ref[...], u_ref[...], _NT, preferred_element_type=F32) * sc_ref[...] + bf_ref[...]
    c = _tri_cumsum(_log_sigmoid(z)) + carry_ref[...]
    o_ref[...] = c
    carry_ref[...] = c[:, c.shape[1] - 1:c.shape[1]]


def _fox_cum(u, scale, wft, bf, Bn, T, tk):
    M, D = u.shape
    H = wft.shape[0]
    nt = T // tk
    return pl.pallas_call(
        _cum_kernel,
        grid=(Bn, nt),
        in_specs=[pl.BlockSpec((tk, D), lambda b, t: (b * nt + t, 0)),
                  pl.BlockSpec((None, 1, tk), lambda b, t: (b * nt + t, 0, 0)),
                  pl.BlockSpec((H, D), lambda b, t: (0, 0)),
                  pl.BlockSpec((H, 1), lambda b, t: (0, 0))],
        out_specs=pl.BlockSpec((None, None, H, tk), lambda b, t: (b, t, 0, 0)),
        out_shape=jax.ShapeDtypeStruct((Bn, nt, H, tk), F32),
        scratch_shapes=[pltpu.VMEM((H, 1), F32)],
        compiler_params=_cparams("arbitrary", "arbitrary"),
        name="fox_cum",
    )(u, scale.reshape(Bn * nt, 1, tk), wft, bf.reshape(H, 1).astype(F32))


def _fox_attn_kernel(q_ref, k_ref, v_ref, cum_ref, o_ref, m_ref, l_ref, acc_ref, *, G, tq, tk, scale):
    qi = pl.program_id(2)
    hd = q_ref.shape[-1]
    reps = tk // LANE
    m_ref[...] = jnp.full(m_ref.shape, NEG_INF, F32)
    l_ref[...] = jnp.zeros(l_ref.shape, F32)
    acc_ref[...] = jnp.zeros(acc_ref.shape, F32)
    n_full = (qi * tq) // tk

    q = q_ref[...].reshape(G * tq, hd)

    def tile(j, masked):
        start = pl.multiple_of(j * tk, tk)
        kt = k_ref[pl.ds(start, tk), :]
        vt = v_ref[pl.ds(start, tk), :]
        s = lax.dot_general(q, kt, _NT, preferred_element_type=F32) * (scale * LOG2E)
        s = s.reshape(G, tq, tk) - cum_ref[j] * LOG2E
        if masked:
            row = lax.broadcasted_iota(I32, (tq, tk), 0) + qi * tq
            col = lax.broadcasted_iota(I32, (tq, tk), 1) + j * tk
            s = jnp.where((col <= row)[None], s, NEG_INF)
        s = s.reshape(G * tq, tk)
        m_prev = m_ref[...]
        m_new = jnp.maximum(m_prev, jnp.max(s, axis=1, keepdims=True))
        p = jnp.exp2(s - jnp.concatenate([m_new] * reps, axis=1))
        alpha = jnp.exp2(m_prev - m_new)
        l_ref[...] = alpha * l_ref[...] + jnp.sum(p, axis=1, keepdims=True)
        acc_ref[...] = alpha * acc_ref[...] + jnp.dot(p.astype(BF16), vt, preferred_element_type=F32)
        m_ref[...] = m_new

    def body(j, carry):
        tile(j, False)
        return carry

    lax.fori_loop(0, n_full, body, 0)
    tile(n_full, True)
    out = acc_ref[...] / l_ref[...]
    for g in range(G):
        o_ref[:, g * hd:(g + 1) * hd] = out[g * tq:(g + 1) * tq, :].astype(o_ref.dtype)


def _fox_attn_prompt(q_hm, k16, v16, cum, Bn, T, G, scale, tq, tk):
    H, M, hd = q_hm.shape
    assert hd == LANE and tk % tq == 0
    KVH = H // G
    nq = T // tq
    return pl.pallas_call(
        functools.partial(_fox_attn_kernel, G=G, tq=tq, tk=tk, scale=scale),
        grid=(Bn, KVH, nq),
        in_specs=[pl.BlockSpec((G, tq, hd), lambda b, h, i: (h, b * nq + i, 0)),
                  pl.BlockSpec((T, hd), lambda b, h, i: (b, h)),
                  pl.BlockSpec((T, hd), lambda b, h, i: (b, h)),
                  pl.BlockSpec((None, None, T // tk, G, 1, tk), lambda b, h, i: (b, h, 0, 0, 0, 0))],
        out_specs=pl.BlockSpec((tq, G * hd), lambda b, h, i: (b * nq + i, h)),
        out_shape=jax.ShapeDtypeStruct((M, H * hd), BF16),
        scratch_shapes=[pltpu.VMEM((G * tq, LANE), F32), pltpu.VMEM((G * tq, LANE), F32),
                        pltpu.VMEM((G * tq, hd), F32)],
        compiler_params=_cparams("parallel", "parallel", "arbitrary"),
        name="fox_attn_prompt",
    )(q_hm, k16, v16, cum)


def _paged_attn_kernel(*refs, kind, P, KVH, G, Q, page, hd, scale):
    pt_ref = refs[0]
    del pt_ref
    pos = 1
    q_ref = refs[pos]; pos += 1
    kp_refs = refs[pos:pos + P]; pos += P
    vp_refs = refs[pos:pos + P]; pos += P
    kn_ref, vn_ref = refs[pos], refs[pos + 1]; pos += 2
    if kind == "fox":
        lf_refs = refs[pos:pos + P]; pos += P
        lfn_ref = refs[pos]; pos += 1
    else:
        mp_ref, mn_ref, tl_ref, tn_ref = refs[pos:pos + 4]; pos += 4
    o_ref = refs[pos]; pos += 1
    kb_ref, vb_ref, m_ref, l_ref, acc_ref = refs[pos:pos + 5]; pos += 5
    if kind == "fox":
        cum_ref, carry_ref = refs[pos:pos + 2]

    jb = pl.program_id(1)
    nb = pl.num_programs(1)
    R = G * Q

    @pl.when(jb == 0)
    def _():
        m_ref[...] = jnp.full(m_ref.shape, NEG_INF, F32)
        l_ref[...] = jnp.zeros(l_ref.shape, F32)
        acc_ref[...] = jnp.zeros(acc_ref.shape, F32)
        if kind == "fox":
            carry_ref[...] = jnp.zeros(carry_ref.shape, F32)

    for p in range(P):
        for kvh in range(KVH):
            kb_ref[kvh, p * page:(p + 1) * page, :] = kp_refs[p][pl.ds(kvh, page, stride=KVH), :].astype(BF16)
            vb_ref[kvh, p * page:(p + 1) * page, :] = vp_refs[p][pl.ds(kvh, page, stride=KVH), :].astype(BF16)
        if kind == "fox":
            c = _tri_cumsum(lf_refs[p][...]) + carry_ref[...]
            cum_ref[:, p * page:(p + 1) * page] = c
            carry_ref[...] = c[:, page - 1:page]

    def logits(keys_of):
        return jnp.concatenate(
            [lax.dot_general(q_ref[kvh], keys_of(kvh), _NT, preferred_element_type=F32) for kvh in range(KVH)],
            axis=0) * (scale * LOG2E)

    def update(s, vals_of):
        m_prev = m_ref[...]
        m_new = jnp.maximum(m_prev, jnp.max(s, axis=1, keepdims=True))
        pr = jnp.exp2(s - jnp.concatenate([m_new] * (s.shape[1] // LANE), axis=1))
        alpha = jnp.exp2(m_prev - m_new)
        l_ref[...] = alpha * l_ref[...] + jnp.sum(pr, axis=1, keepdims=True)
        p16 = pr.astype(BF16)
        pv = jnp.concatenate(
            [jnp.dot(p16[kvh * R:(kvh + 1) * R, :], vals_of(kvh), preferred_element_type=F32) for kvh in range(KVH)],
            axis=0)
        acc_ref[...] = alpha * acc_ref[...] + pv
        m_ref[...] = m_new

    def head_rows(tile):
        return jnp.concatenate(
            [jnp.broadcast_to(tile[h:h + 1, :], (Q, tile.shape[1])) for h in range(KVH * G)], axis=0)

    def query_rows(tile):
        return jnp.concatenate([tile] * (KVH * G), axis=0)

    n = P * page
    s = logits(lambda kvh: kb_ref[kvh])
    if kind == "fox":
        s = s - head_rows(cum_ref[...]) * LOG2E
    else:
        s = s + query_rows(mp_ref[...])
        is_last = jnp.where(jb == nb - 1, LOG2E, 0.0)
        tail = s[:, n - page:] + is_last * tl_ref[...]
        s = tail if n == page else jnp.concatenate([s[:, :n - page], tail], axis=1)
    update(s, lambda kvh: vb_ref[kvh])

    @pl.when(jb == nb - 1)
    def _():
        kn = kn_ref[...].astype(BF16)
        vn = vn_ref[...].astype(BF16)
        qpos = lax.broadcasted_iota(I32, (KVH * R, page), 0) % Q
        kpos = lax.broadcasted_iota(I32, (KVH * R, page), 1)
        s = logits(lambda kvh: kn[:, kvh * hd:(kvh + 1) * hd])
        if kind == "fox":
            s = s - head_rows(_tri_cumsum(lfn_ref[...]) + carry_ref[...]) * LOG2E
        else:
            s = s + query_rows(mn_ref[...]) + tn_ref[...] * LOG2E
        s = jnp.where(kpos <= qpos, s, NEG_INF)
        update(s, lambda kvh: vn[:, kvh * hd:(kvh + 1) * hd])
        out = acc_ref[...] / l_ref[...]
        for kvh in range(KVH):
            o_ref[kvh] = out[kvh * R:(kvh + 1) * R, :]


def _paged_attn(kind, q, pool_k, pool_v, layer, k_new, v_new, page_table, extras, scale):
    Bn, KVH, R, hd = q.shape
    page = pool_k.shape[2] // KVH
    n_pages = page_table.shape[1]
    P = min(PAGES_PER_STEP, n_pages)
    assert n_pages % P == 0
    nb = n_pages // P
    H = None
    if kind == "fox":
        H = extras[0].shape[2]
        G = H // KVH
    else:
        G = extras[2].shape[0] // (KVH * extras[0].shape[1])
    Q = R // G

    def page_map(p, nd):
        return lambda b, j, pt: (layer, pt[b, j * P + p]) + (0,) * nd

    in_specs = [pl.BlockSpec((None, KVH, R, hd), lambda b, j, pt: (b, 0, 0, 0))]
    args = [q]
    for pool in (pool_k, pool_v):
        for p in range(P):
            in_specs.append(pl.BlockSpec((None, None, page * KVH, hd), page_map(p, 2)))
            args.append(pool)
    for new in (k_new, v_new):
        in_specs.append(pl.BlockSpec((None, page, KVH * hd), lambda b, j, pt: (b, 0, 0)))
        args.append(new)
    scratch = [pltpu.VMEM((KVH, P * page, hd), BF16), pltpu.VMEM((KVH, P * page, hd), BF16),
               pltpu.VMEM((KVH * R, LANE), F32), pltpu.VMEM((KVH * R, LANE), F32),
               pltpu.VMEM((KVH * R, hd), F32)]
    if kind == "fox":
        pool_lft, lf_newt = extras
        for p in range(P):
            in_specs.append(pl.BlockSpec((None, None, H, page), page_map(p, 2)))
            args.append(pool_lft)
        in_specs.append(pl.BlockSpec((None, H, page), lambda b, j, pt: (b, 0, 0)))
        args.append(lf_newt)
        scratch += [pltpu.VMEM((H, P * page), F32), pltpu.VMEM((H, 1), F32)]
    else:
        madd_past, madd_new, bias_last, bias_new = extras
        in_specs.append(pl.BlockSpec((None, Q, P * page), lambda b, j, pt: (b, 0, j)))
        args.append(madd_past)
        in_specs.append(pl.BlockSpec((None, Q, page), lambda b, j, pt: (b, 0, 0)))
        args.append(madd_new)
        for t in (bias_last, bias_new):
            in_specs.append(pl.BlockSpec((KVH * R, page), lambda b, j, pt: (0, 0)))
            args.append(t)
    grid_spec = pltpu.PrefetchScalarGridSpec(
        num_scalar_prefetch=1,
        grid=(Bn, nb),
        in_specs=in_specs,
        out_specs=pl.BlockSpec((None, KVH, R, hd), lambda b, j, pt: (b, 0, 0, 0)),
        scratch_shapes=scratch,
    )
    return pl.pallas_call(
        functools.partial(_paged_attn_kernel, kind=kind, P=P, KVH=KVH, G=G, Q=Q, page=page, hd=hd, scale=scale),
        grid_spec=grid_spec,
        out_shape=jax.ShapeDtypeStruct((Bn, KVH, R, hd), F32),
        compiler_params=_cparams("arbitrary", "arbitrary"),
        name=kind + "_attn_sample",
    )(page_table, *args)


def _sortable(s):
    bits = pltpu.bitcast(s + 0.0, I32)
    return jnp.where(bits >= 0, bits, bits ^ jnp.int32(0x7FFFFFFF))


def _kth_largest_key(count_ge, rows, k):
    sign = jnp.int32(-2 ** 31)

    def body(i, t):
        cand = t | jnp.left_shift(jnp.int32(1), 31 - i)
        cnt = count_ge(cand ^ sign)
        return jnp.where(cnt >= k, cand, t)

    t = lax.fori_loop(0, 32, body, jnp.zeros((rows, 1), I32))
    return t ^ sign


def _last_tie_index(ties_before, need, rows, n_keys):
    nbits = max(int(n_keys - 1).bit_length(), 1)

    def body(i, m):
        cand = m | jnp.left_shift(jnp.int32(1), nbits - 1 - i)
        return jnp.where(ties_before(cand) < need, cand, m)

    return lax.fori_loop(0, nbits, body, jnp.zeros((rows, 1), I32))


def _dsa_score_kernel(qi_ref, ki_ref, wi_ref, o_ref, acc_ref, wb_ref, *, HI, tq, T, q0, k_top, scale):
    wi = wi_ref[...]
    for h in range(HI):
        wb_ref[h] = jnp.broadcast_to(wi[:, h:h + 1], (tq, LANE))
    acc_ref[...] = jnp.zeros(acc_ref.shape, F32)
    ki = ki_ref[...]
    reps = T // LANE

    def body(h, carry):
        d = lax.dot_general(qi_ref[h], ki, _NT, preferred_element_type=F32)
        w = jnp.concatenate([wb_ref[h]] * reps, axis=1)
        acc_ref[...] += w * jnp.maximum(d, 0.0)
        return carry

    lax.fori_loop(0, HI, body, 0)
    qpos = lax.broadcasted_iota(I32, (tq, T), 0) + q0
    kpos = lax.broadcasted_iota(I32, (tq, T), 1)
    adm = kpos <= qpos
    key = _sortable(jnp.where(adm, acc_ref[...] * scale, NEG_INF))

    def count_ge(thr):
        return jnp.sum(jnp.where(key >= thr, 1, 0), axis=-1, keepdims=True)

    thr = _kth_largest_key(count_ge, tq, k_top)
    sel = jnp.logical_and(key >= thr, adm)
    o_ref[...] = jnp.where(sel, 0.0, NEG_INF)

    @pl.when(jnp.max(jnp.sum(jnp.where(sel, 1, 0), axis=-1, keepdims=True)) > k_top)
    def _():
        gt = key > thr
        tie = key == thr
        need = k_top - jnp.sum(jnp.where(gt, 1, 0), axis=-1, keepdims=True)

        def ties_before(m):
            return jnp.sum(jnp.where(jnp.logical_and(tie, kpos < m), 1, 0), axis=-1, keepdims=True)

        last = _last_tie_index(ties_before, need, tq, T)
        keep = jnp.logical_or(gt, jnp.logical_and(tie, kpos <= last))
        o_ref[...] = jnp.where(jnp.logical_and(keep, adm), 0.0, NEG_INF)


def _dsa_select_prompt(qi_hm, ki16, wi, Bn, T, c, tq, k_top, scale):
    HI, M, DI = qi_hm.shape
    nq = T // tq
    tk = (c + 1) * tq
    wl = wi.shape[1]
    return pl.pallas_call(
        functools.partial(_dsa_score_kernel, HI=HI, tq=tq, T=tk, q0=c * tq, k_top=k_top, scale=scale),
        grid=(Bn,),
        in_specs=[pl.BlockSpec((HI, tq, DI), lambda b: (0, b * nq + c, 0)),
                  pl.BlockSpec((None, tk, DI), lambda b: (b, 0, 0)),
                  pl.BlockSpec((tq, wl), lambda b: (b * nq + c, 0))],
        out_specs=pl.BlockSpec((None, tq, tk), lambda b: (b, 0, 0)),
        out_shape=jax.ShapeDtypeStruct((Bn, tq, tk), F32),
        scratch_shapes=[pltpu.VMEM((tq, tk), F32), pltpu.VMEM((HI, tq, LANE), F32)],
        compiler_params=_cparams("parallel"),
        name="dsa_select_prompt",
    )(qi_hm, ki16, wi)


def _dsa_attn_kernel(q_ref, k_ref, v_ref, madd_ref, td_ref, tl_ref, o_ref, s_ref, *, G, tq, iq0, scale):
    iq = iq0 + pl.program_id(1)
    hd = q_ref.shape[-1]
    k = k_ref[...]
    v = v_ref[...]
    madd = madd_ref[...]
    bt = td_ref.shape[-1]
    for g in range(G):
        s_ref[g] = lax.dot_general(q_ref[g], k, _NT, preferred_element_type=F32) * (scale * LOG2E) + madd
        for r in range(tq // bt):
            rows = slice(r * bt, (r + 1) * bt)
            a = iq * (tq // bt) + r
            s_ref[g, rows, pl.ds(pl.multiple_of(a * bt, bt), bt)] += td_ref[g] * LOG2E

            @pl.when(a > 0)
            def _():
                s_ref[g, rows, pl.ds(pl.multiple_of(jnp.maximum(a - 1, 0) * bt, bt), bt)] += tl_ref[g] * LOG2E

        s = s_ref[g]
        m = jnp.max(s, axis=-1, keepdims=True)
        p = jnp.exp2(s - m)
        l = jnp.sum(p, axis=-1, keepdims=True)
        o = jnp.dot(p.astype(BF16), v, preferred_element_type=F32) / l
        o_ref[:, g * hd:(g + 1) * hd] = o.astype(o_ref.dtype)


def _dsa_attn_prompt(q_hm, k16, v16, madd, t_diag, t_left, Bn, T, c, tqs, tq, G, scale):
    H, M, hd = q_hm.shape
    KVH = H // G
    tk = (c + 1) * tqs
    nsub = tqs // tq
    nq = T // tq
    return pl.pallas_call(
        functools.partial(_dsa_attn_kernel, G=G, tq=tq, iq0=c * nsub, scale=scale),
        grid=(Bn, nsub, KVH),
        in_specs=[pl.BlockSpec((G, tq, hd), lambda b, i, h: (h, b * nq + c * nsub + i, 0)),
                  pl.BlockSpec((None, tk, hd), lambda b, i, h: (b, 0, h)),
                  pl.BlockSpec((None, tk, hd), lambda b, i, h: (b, 0, h)),
                  pl.BlockSpec((None, tq, tk), lambda b, i, h: (b, i, 0)),
                  pl.BlockSpec((G,) + t_diag.shape[1:], lambda b, i, h: (h, 0, 0)),
                  pl.BlockSpec((G,) + t_left.shape[1:], lambda b, i, h: (h, 0, 0))],
        out_specs=pl.BlockSpec((None, tq, G * hd), lambda b, i, h: (b, i, h)),
        out_shape=jax.ShapeDtypeStruct((Bn, tqs, H * hd), BF16),
        scratch_shapes=[pltpu.VMEM((G, tq, tk), F32)],
        compiler_params=_cparams("parallel", "parallel", "arbitrary"),
        name="dsa_attn_prompt",
    )(q_hm, k16, v16, madd, t_diag, t_left)


def _dsa_score_sample_kernel(*refs, P, HI, Q, page, k_top, scale):
    pt_ref = refs[0]
    del pt_ref
    qi_ref, wi_ref = refs[1], refs[2]
    kp_refs = refs[3:3 + P]
    kn_ref = refs[3 + P]
    mp_ref, mn_ref = refs[4 + P], refs[5 + P]
    kb_ref, sp_ref, sn_ref = refs[6 + P:9 + P]

    jb = pl.program_id(1)
    nb = pl.num_programs(1)
    n = P * page
    qi = qi_ref[...]
    wcol = wi_ref[...]

    def score(keys16):
        d = lax.dot_general(qi, keys16, _NT, preferred_element_type=F32)
        d = jnp.broadcast_to(wcol, d.shape) * jnp.maximum(d, 0.0)
        tot = d[0:Q]
        for h in range(1, HI):
            tot = tot + d[h * Q:(h + 1) * Q]
        return tot * scale

    for p in range(P):
        kb_ref[p * page:(p + 1) * page, :] = kp_refs[p][...].astype(BF16)
    sp_ref[jb] = score(kb_ref[...])

    @pl.when(jb == nb - 1)
    def _():
        qpos = lax.broadcasted_iota(I32, (Q, page), 0)
        kpos = lax.broadcasted_iota(I32, (Q, page), 1)
        adm = kpos <= qpos
        sn_ref[...] = jnp.where(adm, score(kn_ref[...].astype(BF16)), NEG_INF)
        key_p = _sortable(sp_ref[...])
        key_n = _sortable(sn_ref[...])

        def count_ge(thr):
            cp = jnp.sum(jnp.where(key_p >= thr[None], 1, 0), axis=-1, keepdims=True)
            cn = jnp.sum(jnp.where(key_n >= thr, 1, 0), axis=-1, keepdims=True)
            return jnp.sum(cp, axis=0) + cn

        thr = _kth_largest_key(count_ge, Q, k_top)
        sel_n = jnp.logical_and(key_n >= thr, adm)
        mp_ref[...] = jnp.where(key_p >= thr[None], 0.0, NEG_INF)
        mn_ref[...] = jnp.where(sel_n, 0.0, NEG_INF)

        def count(mask_p, mask_n):
            cp = jnp.sum(jnp.where(mask_p, 1, 0), axis=-1, keepdims=True)
            return jnp.sum(cp, axis=0) + jnp.sum(jnp.where(mask_n, 1, 0), axis=-1, keepdims=True)

        @pl.when(jnp.max(count(key_p >= thr[None], sel_n)) > k_top)
        def _():
            idx_p = (lax.broadcasted_iota(I32, key_p.shape, 0) * n + lax.broadcasted_iota(I32, key_p.shape, 2))
            n_past = key_p.shape[0] * n
            idx_n = kpos + n_past
            gt_p, gt_n = key_p > thr[None], key_n > thr
            tie_p, tie_n = key_p == thr[None], key_n == thr
            need = k_top - count(gt_p, gt_n)

            def ties_before(m):
                return count(jnp.logical_and(tie_p, idx_p < m[None]), jnp.logical_and(tie_n, idx_n < m))

            last = _last_tie_index(ties_before, need, Q, n_past + page)
            keep_p = jnp.logical_or(gt_p, jnp.logical_and(tie_p, idx_p <= last[None]))
            keep_n = jnp.logical_or(gt_n, jnp.logical_and(tie_n, idx_n <= last))
            mp_ref[...] = jnp.where(keep_p, 0.0, NEG_INF)
            mn_ref[...] = jnp.where(jnp.logical_and(keep_n, adm), 0.0, NEG_INF)


def _dsa_select_sample(qi, wi, pool_ki, layer, ki_new, page_table, Q, k_top, scale):
    Bn, RQ, DI = qi.shape
    page = pool_ki.shape[2]
    n_pages = page_table.shape[1]
    P = min(PAGES_PER_STEP, n_pages)
    nb = n_pages // P
    HI = RQ // Q
    in_specs = [pl.BlockSpec((None, RQ, DI), lambda b, j, pt: (b, 0, 0)),
                pl.BlockSpec((None, RQ, 1), lambda b, j, pt: (b, 0, 0))]
    args = [qi, wi]
    for p in range(P):
        in_specs.append(pl.BlockSpec((None, None, page, DI), functools.partial(
            lambda b, j, pt, p: (layer, pt[b, j * P + p], 0, 0), p=p)))
        args.append(pool_ki)
    in_specs.append(pl.BlockSpec((None, page, DI), lambda b, j, pt: (b, 0, 0)))
    args.append(ki_new)
    grid_spec = pltpu.PrefetchScalarGridSpec(
        num_scalar_prefetch=1,
        grid=(Bn, nb),
        in_specs=in_specs,
        out_specs=[pl.BlockSpec((None, nb, Q, P * page), lambda b, j, pt: (b, 0, 0, 0)),
                   pl.BlockSpec((None, Q, page), lambda b, j, pt: (b, 0, 0))],
        scratch_shapes=[pltpu.VMEM((P * page, DI), BF16), pltpu.VMEM((nb, Q, P * page), F32),
                        pltpu.VMEM((Q, page), F32)],
    )
    return pl.pallas_call(
        functools.partial(_dsa_score_sample_kernel, P=P, HI=HI, Q=Q, page=page, k_top=k_top, scale=scale),
        grid_spec=grid_spec,
        out_shape=[jax.ShapeDtypeStruct((Bn, nb, Q, P * page), F32), jax.ShapeDtypeStruct((Bn, Q, page), F32)],
        compiler_params=_cparams("arbitrary", "arbitrary"),
        name="dsa_select_sample",
    )(page_table, *args)


def _t5_bucket_np(dist):
    max_exact = N_BUCKETS // 2
    d = np.maximum(dist, 0)
    ratio = np.log(np.maximum(d, 1).astype(np.float32) / np.float32(max_exact)) / np.float32(
        math.log(MAX_DISTANCE / max_exact))
    large = np.minimum(max_exact + (ratio * (N_BUCKETS - max_exact)).astype(np.int32), N_BUCKETS - 1)
    return np.where(d < max_exact, d, large).astype(np.int32)


def _pad_cols(w, n):
    return jnp.pad(w, ((0, 0), (0, n - w.shape[1])))


def _to_rows(o, Bn, Q, KVH, G, hd):
    o = o.reshape(Bn, KVH, G, Q, hd)
    return jnp.transpose(o, (0, 3, 1, 2, 4)).reshape(Bn * Q, KVH * G * hd)


def _to_heads(q, Bn, Q, KVH, G, hd):
    q = q.reshape(Bn, Q, KVH, G, hd)
    return jnp.transpose(q, (0, 2, 3, 1, 4)).reshape(Bn, KVH, G * Q, hd)


def _pad_page(x, page):
    return jnp.pad(x, ((0, 0), (0, page - x.shape[1]), (0, 0)))


def _next_norm(hp, pg, psq, hs, sg, ssq):
    n = hp.shape[1]
    if psq.ndim == 3:
        psq, ssq = jnp.sum(psq, axis=0), jnp.sum(ssq, axis=0)
    return hp, hs, (pg, sg, (lax.rsqrt(psq / n + RMS_EPS), lax.rsqrt(ssq / n + RMS_EPS)))


def _fox_layer(hp, hs, xin, next_gain, dims, layer, pool_k, pool_v, pool_lft, page_table,
               w_q, w_k, w_v, w_f, b_f, w_o):
    up, us, rs = xin
    Bp, T, Bs, Q, H, KVH, hd = dims
    G = H // KVH
    scale = hd ** -0.5
    page = pool_k.shape[2] // KVH
    wf_pad = _pad_cols(w_f[layer], LANE)[None]
    bf_pad = jnp.pad(b_f[layer], (0, LANE - H))
    wft = w_f[layer].T.astype(BF16)
    tq, tk = min(ATTN_TILE, T // 2), min(ATTN_TILE, T)

    q_hm, qs = _mmw(up, w_q, layer, xs=us, scale=rs, emit_f32=False, emit_bf16=True, head_major=True, name="fox_q")
    k32, k16, ks = _mmw(up, w_k, layer, xs=us, scale=rs, emit_bf16=True, name="fox_k")
    v32, v16, vs = _mmw(up, w_v, layer, xs=us, scale=rs, emit_bf16=True, name="fox_v")
    lf_pad, lfs_pad = _mmw(up, wf_pad, 0, xs=us, scale=rs, ls_bias=bf_pad, tn=LANE, name="fox_logf")
    lfs = lfs_pad[:, :H]

    cum = _fox_cum(up, rs[0], wft, b_f[layer], Bp, T, tk)
    cum = jnp.transpose(cum.reshape(Bp, T // tk, KVH, G, 1, tk), (0, 2, 1, 3, 4, 5))
    o = _fox_attn_prompt(q_hm, k16, v16, cum, Bp, T, G, scale, tq, tk)

    q_t = _to_heads(qs, Bs, Q, KVH, G, hd).astype(BF16)
    lf_newt = jnp.transpose(_pad_page(lfs.reshape(Bs, Q, H), page), (0, 2, 1))
    o_s = _paged_attn("fox", q_t, pool_k, pool_v, layer,
                      _pad_page(ks.reshape(Bs, Q, KVH * hd), page), _pad_page(vs.reshape(Bs, Q, KVH * hd), page),
                      page_table, (pool_lft, lf_newt), scale)
    o_s = _to_rows(o_s, Bs, Q, KVH, G, hd).astype(BF16)

    hp, hs, xnext = _next_norm(*_mmw(o, w_o, layer, xs=o_s, res=hp, res_s=hs, next_gain=next_gain, name="fox_o"))
    rows = (k32.reshape(Bp, T, KVH, hd), v32.reshape(Bp, T, KVH, hd), lf_pad[:, :H].reshape(Bp, T, H),
            ks.reshape(Bs, Q, KVH, hd), vs.reshape(Bs, Q, KVH, hd), lfs.reshape(Bs, Q, H))
    return hp, hs, xnext, rows


def _sconv_layer(hp, hs, xin, next_gain, dims, layer, state, w_in, w_conv, w_out):
    up, us, rs = xin
    Bp, T, Bs, Q = dims[:4]
    D = w_out.shape[1]
    zp, stp, zs, sts = _gconv(up, us, rs, w_in, layer, (0, D, 2 * D), D, w_conv, _conv_prefix(state[layer], Q),
                              mode="sconv", seq_len=T, tail_len=Q, tm=SCONV_ROW_TILE, name="sconv_in")
    hp, hs, xnext = _next_norm(*_mmw(zp, w_out, layer, xs=zs, res=hp, res_s=hs, next_gain=next_gain,
                                     name="sconv_out"))
    nblk = stp.shape[0] // Bp
    sp = stp.reshape(Bp, nblk, SUBLANE, D)[:, -1, SUBLANE - 2:, :]
    ss = sts.reshape(Bs, Q, D)[:, Q - 2:, :]
    return hp, hs, xnext, sp, ss


def _dsa_layer(hp, hs, xin, next_gain, dims, layer, pool_k, pool_v, pool_ki, page_table, rel_bias,
               w_q, w_k, w_v, w_o, w_qi, w_ki, w_wi):
    up, us, rs = xin
    Bp, T, Bs, Q, H, KVH, hd = dims
    G = H // KVH
    scale = hd ** -0.5
    page = pool_k.shape[2] // KVH
    DI = w_ki.shape[2]
    HI = w_wi.shape[2]
    idx_scale = (DI * HI) ** -0.5
    past = page_table.shape[1] * page
    wkw = jnp.concatenate([w_ki[layer], _pad_cols(w_wi[layer], LANE)], axis=1)[None]
    c_far = rel_bias[N_BUCKETS - 1]

    q_hm, qs = _mmw(up, w_q, layer, xs=us, scale=rs, emit_f32=False, emit_bf16=True, head_major=True, name="dsa_q")
    k32, k16, ks = _mmw(up, w_k, layer, xs=us, scale=rs, emit_bf16=True, name="dsa_k")
    v32, v16, vs = _mmw(up, w_v, layer, xs=us, scale=rs, emit_bf16=True, name="dsa_v")
    qi_hm, qis = _mmw(up, w_qi, layer, xs=us, scale=rs, emit_f32=False, emit_bf16=True, head_major=True,
                      name="dsa_qi")
    kw, kws = _mmw(up, wkw, 0, xs=us, scale=rs, tn=DI + LANE, name="dsa_kiwi")

    tqs = min(ATTN_TILE, T // 2)
    tq = min(ATTN_TILE, tqs)
    assert tq >= MAX_DISTANCE and tqs % tq == 0 and T % tqs == 0
    ki32 = kw[:, :DI]
    wi = kw[:, DI:]
    ki16 = ki32.astype(BF16).reshape(Bp, T, DI)
    k16 = k16.reshape(Bp, T, KVH * hd)
    v16 = v16.reshape(Bp, T, KVH * hd)
    bt = MAX_DISTANCE
    ii = np.arange(bt)[:, None]
    jj = np.arange(bt)[None, :]
    t_diag = jnp.transpose(rel_bias[_t5_bucket_np(ii - jj)] - c_far, (2, 0, 1))
    t_left = jnp.transpose(rel_bias[_t5_bucket_np(bt + ii - jj)] - c_far, (2, 0, 1))
    k_top = min(TOPK_MAX, T // 4)
    o_blocks = []
    for c in range(T // tqs):
        madd = _dsa_select_prompt(qi_hm, ki16, wi, Bp, T, c, tqs, k_top, idx_scale)
        o_blocks.append(_dsa_attn_prompt(q_hm, k16, v16, madd, t_diag, t_left, Bp, T, c, tqs, tq, G, scale))
    o = jnp.concatenate(o_blocks, axis=1).reshape(Bp * T, H * hd)

    kis = kws[:, :DI]
    wis = kws[:, DI:DI + HI]
    qi_t = jnp.transpose(qis.reshape(Bs, Q, HI, DI), (0, 2, 1, 3)).reshape(Bs, HI * Q, DI).astype(BF16)
    wi_t = jnp.transpose(wis.reshape(Bs, Q, HI), (0, 2, 1)).reshape(Bs, HI * Q, 1)
    mp, mn = _dsa_select_sample(qi_t, wi_t, pool_ki, layer, _pad_page(kis.reshape(Bs, Q, DI), page), page_table,
                                Q, min(TOPK_MAX, (past + Q) // 4), idx_scale)
    nb, n = mp.shape[1], mp.shape[3]
    madd_past = jnp.transpose(mp, (0, 2, 1, 3)).reshape(Bs, Q, nb * n)
    qq = np.arange(Q)[:, None]
    cc = np.arange(page)[None, :]

    def sample_bias(dist):
        t = rel_bias[_t5_bucket_np(dist)] - c_far
        t = jnp.transpose(t.reshape(Q, page, KVH, G), (2, 3, 0, 1))
        return t.reshape(KVH * G * Q, page)

    bias_last = sample_bias(past + qq - (past - page + cc))
    bias_new = sample_bias(qq - cc)
    q_t = _to_heads(qs, Bs, Q, KVH, G, hd).astype(BF16)
    o_s = _paged_attn("dsa", q_t, pool_k, pool_v, layer,
                      _pad_page(ks.reshape(Bs, Q, KVH * hd), page), _pad_page(vs.reshape(Bs, Q, KVH * hd), page),
                      page_table, (madd_past, mn, bias_last, bias_new), scale)
    o_s = _to_rows(o_s, Bs, Q, KVH, G, hd).astype(BF16)

    hp, hs, xnext = _next_norm(*_mmw(o, w_o, layer, xs=o_s, res=hp, res_s=hs, next_gain=next_gain, name="dsa_o"))
    rows = (k32.reshape(Bp, T, KVH, hd), v32.reshape(Bp, T, KVH, hd), ki32.reshape(Bp, T, DI),
            ks.reshape(Bs, Q, KVH, hd), vs.reshape(Bs, Q, KVH, hd), kis.reshape(Bs, Q, DI))
    return hp, hs, xnext, rows


def _ffn_layer(hp, hs, xin, next_gain, dims, layer, state, w_up, w_conv, w_down16):
    vp, vs, rs = xin
    Bp, T, Bs, Q = dims[:4]
    DFF = w_down16.shape[1]
    ap, stp, a_s, sts = _gconv(vp, vs, rs, w_up, layer, (0, DFF), DFF, w_conv, _conv_prefix(state[layer], Q),
                               mode="ffn", seq_len=T, tail_len=Q, tm=ROW_TILE, name="ffn_up")
    outs = _mmw(ap, w_down16, layer, xs=a_s, res=hp, res_s=hs, next_gain=next_gain, tm=DOWN_TILE, tn=DOWN_TILE,
                name="ffn_down")
    if next_gain is None:
        hp, hs = outs
        xnext = None
    else:
        hp, hs, xnext = _next_norm(*outs)
    nblk = stp.shape[0] // Bp
    cp = stp.reshape(Bp, nblk, SUBLANE, DFF)[:, -1, SUBLANE - 2:, :]
    cs = sts.reshape(Bs, Q, DFF)[:, Q - 2:, :]
    return hp, hs, xnext, cp, cs


def kernel(x_prompt, x_sample, cache_fox_k, cache_fox_v, cache_fox_logf, state_sconv, cache_dsa_k, cache_dsa_v,
           cache_dsa_kidx, state_ffn_conv, page_table, rel_bias, norm_mix, norm_ffn, norm_final,
           fox_w_q, fox_w_k, fox_w_v, fox_w_f, fox_b_f, fox_w_o, sc_w_in, sc_w_conv, sc_w_out,
           dsa_w_q, dsa_w_k, dsa_w_v, dsa_w_o, dsa_w_qi, dsa_w_ki, dsa_w_wi, ffn_w_up, ffn_w_conv, ffn_w_down):
    Bp, T, D = x_prompt.shape
    Bs, Q, _ = x_sample.shape
    depth = norm_mix.shape[0]
    KVH, hd = cache_fox_k.shape[3], cache_fox_k.shape[4]
    H = fox_w_f.shape[2]
    dims = (Bp, T, Bs, Q, H, KVH, hd)
    hp = x_prompt.reshape(Bp * T, D)
    hs = x_sample.reshape(Bs * Q, D)

    def key_head_rows(c):
        return c.reshape(c.shape[0], c.shape[1], c.shape[2] * KVH, hd)

    fox_k, fox_v, dsa_k, dsa_v = (key_head_rows(c) for c in (cache_fox_k, cache_fox_v, cache_dsa_k, cache_dsa_v))
    fox_lft = jnp.transpose(cache_fox_logf, (0, 1, 3, 2))
    w_down16 = ffn_w_down.astype(BF16)
    fox_new = ([], [], [], [], [], [])
    sc_new = ([], [])
    dsa_new = ([], [], [], [], [], [])
    ffn_new = ([], [])
    ones = (jnp.ones((Bp * T, 1), F32), jnp.ones((Bs * Q, 1), F32))
    xin = (_rmsnorm(hp, norm_mix[0], BF16), _rmsnorm(hs, norm_mix[0], BF16), ones)
    for i in range(depth):
        j, kind = i // 3, i % 3
        if kind == 0:
            hp, hs, xin, rows = _fox_layer(hp, hs, xin, norm_ffn[i], dims, j, fox_k, fox_v, fox_lft, page_table,
                                           fox_w_q, fox_w_k, fox_w_v, fox_w_f, fox_b_f, fox_w_o)
            for lst, a in zip(fox_new, rows):
                lst.append(a)
        elif kind == 1:
            hp, hs, xin, sp, ss = _sconv_layer(hp, hs, xin, norm_ffn[i], dims, j, state_sconv, sc_w_in, sc_w_conv,
                                               sc_w_out)
            sc_new[0].append(sp)
            sc_new[1].append(ss)
        else:
            hp, hs, xin, rows = _dsa_layer(hp, hs, xin, norm_ffn[i], dims, j, dsa_k, dsa_v, cache_dsa_kidx,
                                           page_table, rel_bias, dsa_w_q, dsa_w_k, dsa_w_v, dsa_w_o, dsa_w_qi,
                                           dsa_w_ki, dsa_w_wi)
            for lst, a in zip(dsa_new, rows):
                lst.append(a)
        hp, hs, xin, cp, cs = _ffn_layer(hp, hs, xin, norm_mix[i + 1] if i + 1 < depth else None, dims, i,
                                         state_ffn_conv, ffn_w_up, ffn_w_conv, w_down16)
        ffn_new[0].append(cp)
        ffn_new[1].append(cs)
    y_prompt = _rmsnorm(hp, norm_final, F32).reshape(Bp, T, D)
    y_sample = _rmsnorm(hs, norm_final, F32).reshape(Bs, Q, D)
    return (y_prompt, y_sample,
            jnp.stack(fox_new[0]), jnp.stack(fox_new[1]), jnp.stack(fox_new[2]),
            jnp.stack(fox_new[3]), jnp.stack(fox_new[4]), jnp.stack(fox_new[5]),
            jnp.stack(sc_new[0]), jnp.stack(sc_new[1]),
            jnp.stack(dsa_new[0]), jnp.stack(dsa_new[1]), jnp.stack(dsa_new[2]),
            jnp.stack(dsa_new[3]), jnp.stack(dsa_new[4]), jnp.stack(dsa_new[5]),
            jnp.stack(ffn_new[0]), jnp.stack(ffn_new[1]))
```
